```python
import jax, jax.numpy as jnp
from jax import lax
import numpy as np

D_MODEL = 4096
BATCH = 2
SEQ = 8192
DEPTH = 2

CHUNK = 64
EPS = 1e-6
CONV_W = 4
HG_DK = 128
HG_DV = 128
HG_HEADS = D_MODEL // HG_DK
HG_WIDTH = HG_HEADS * HG_DK
LRU_WIDTH = D_MODEL
LRU_BLOCK = 128
LRU_BLOCKS = LRU_WIDTH // LRU_BLOCK
LRU_C = 8.0
EVEN_IN = 4 * HG_WIDTH + 2 * LRU_WIDTH
EVEN_MIX = HG_WIDTH + LRU_WIDTH
SSD_INNER = 2 * D_MODEL
SSD_HEADDIM = 64
SSD_HEADS = SSD_INNER // SSD_HEADDIM
SSD_GROUPS = 8
SSD_HPG = SSD_HEADS // SSD_GROUPS
SSD_STATE = 128
SSD_CONV_DIM = SSD_INNER + 2 * SSD_GROUPS * SSD_STATE
ODD_IN = SSD_INNER + SSD_CONV_DIM + SSD_HEADS
N_EVEN = (DEPTH + 1) // 2
N_ODD = DEPTH // 2

kernel_name = "hybrid_hgrn2_rglru_mamba2_trunk"


def rmsnorm(x, w):
    xf = x.astype(jnp.float32)
    y = xf * lax.rsqrt(jnp.mean(xf * xf, axis=-1, keepdims=True) + EPS)
    return y * w.astype(jnp.float32)


def causal_conv(x, w, b):
    k_w = w.shape[0]
    s = x.shape[1]
    xp = jnp.pad(x, ((0, 0), (k_w - 1, 0), (0, 0)))
    out = b.astype(jnp.float32) + xp[:, 0:s] * w[0]
    for k in range(1, k_w):
        out = out + xp[:, k:k + s] * w[k]
    return out


def hgrn2(q_pre, f_pre, i_val, lb):
    bsz, s, _ = q_pre.shape
    n = s // CHUNK
    q = jax.nn.silu(q_pre.astype(jnp.float32))
    z = f_pre.astype(jnp.float32)
    lb = lb.astype(jnp.float32)
    logf = jnp.logaddexp(jnp.log(lb), jnp.log1p(-lb) + jax.nn.log_sigmoid(z))
    k = (1.0 - lb) * jax.nn.sigmoid(-z)

    def heads(t, d):
        return t.reshape(bsz, n, CHUNK, HG_HEADS, d).transpose(0, 3, 1, 2, 4)

    q, k, logf = heads(q, HG_DK), heads(k, HG_DK), heads(logf, HG_DK)
    v = heads(i_val.astype(jnp.float32), HG_DV)
    b = jnp.cumsum(logf, axis=3)
    ref = b[:, :, :, CHUNK // 2 - 1:CHUNK // 2]
    q_in = q * jnp.exp(b - ref)
    k_in = k * jnp.exp(ref - b)
    scores = jnp.einsum('bhnld,bhnsd->bhnls', q_in, k_in)
    causal = jnp.tril(jnp.ones((CHUNK, CHUNK), dtype=bool))
    scores = jnp.where(causal, scores, 0.0)
    o_intra = jnp.einsum('bhnls,bhnsv->bhnlv', scores, v)
    q_out = q * jnp.exp(b)
    k_st = k * jnp.exp(b[:, :, :, -1:] - b)
    g_last = jnp.exp(b[:, :, :, -1])

    def step(state, inp):
        qo, ks, vv, gl = inp
        o = jnp.einsum('bhld,bhdv->bhlv', qo, state)
        state = state * gl[..., None] + jnp.einsum('bhld,bhlv->bhdv', ks, vv)
        return state, o

    s0 = jnp.zeros((bsz, HG_HEADS, HG_DK, HG_DV), jnp.float32)
    xs = (jnp.moveaxis(q_out, 2, 0), jnp.moveaxis(k_st, 2, 0),
          jnp.moveaxis(v, 2, 0), jnp.moveaxis(g_last, 2, 0))
    _, o_inter = lax.scan(step, s0, xs)
    o = o_intra + jnp.moveaxis(o_inter, 0, 2)
    return o.transpose(0, 2, 3, 1, 4).reshape(bsz, s, HG_HEADS * HG_DV)


def _lin_combine(c1, c2):
    a1, b1 = c1
    a2, b2 = c2
    return a1 * a2, a2 * b1 + b2


def rglru(x, wa, ba, wx, bx, lam):
    bsz, s, _ = x.shape
    xf = x.astype(jnp.float32)
    xb = xf.reshape(bsz, s, LRU_BLOCKS, LRU_BLOCK)
    r = jax.nn.sigmoid(jnp.einsum('bsnj,njk->bsnk', xb, wa).reshape(bsz, s, LRU_WIDTH) + ba)
    ig = jax.nn.sigmoid(jnp.einsum('bsnj,njk->bsnk', xb, wx).reshape(bsz, s, LRU_WIDTH) + bx)
    log_a = -LRU_C * r * jax.nn.softplus(-lam.astype(jnp.float32))
    a = jnp.exp(log_a)
    u = jnp.sqrt(-jnp.expm1(2.0 * log_a)) * (ig * xf)
    _, h = lax.associative_scan(_lin_combine, (a, u), axis=1)
    return h


def ssd(x, dt, a_log, bm, cm, d_skip):
    bsz, s = x.shape[:2]
    n = s // CHUNK
    a = -jnp.exp(a_log.astype(jnp.float32))
    adt = (dt * a).reshape(bsz, n, CHUNK, SSD_GROUPS, SSD_HPG)
    cs = jnp.cumsum(adt, axis=2)
    xg = x.reshape(bsz, n, CHUNK, SSD_GROUPS, SSD_HPG, SSD_HEADDIM)
    xdt = xg * dt.reshape(bsz, n, CHUNK, SSD_GROUPS, SSD_HPG)[..., None]
    bc = bm.reshape(bsz, n, CHUNK, SSD_GROUPS, SSD_STATE)
    cc = cm.reshape(bsz, n, CHUNK, SSD_GROUPS, SSD_STATE)
    seg = cs[:, :, :, None] - cs[:, :, None, :]
    causal = jnp.tril(jnp.ones((CHUNK, CHUNK), dtype=bool))[:, :, None, None]
    decay = jnp.exp(jnp.where(causal, seg, -jnp.inf))
    cb = jnp.einsum('bnlgk,bnmgk->bnlmg', cc, bc)
    y_diag = jnp.einsum('bnlmgh,bnmghp->bnlghp', cb[..., None] * decay, xdt)
    dec_out = jnp.exp(cs)
    dec_st = jnp.exp(cs[:, :, -1:] - cs)
    g_last = jnp.exp(cs[:, :, -1])

    def step(state, inp):
        c_c, b_c, x_c, d_o, d_s, g_l = inp
        y = jnp.einsum('blgk,bghpk,blgh->blghp', c_c, state, d_o)
        state = state * g_l[..., None, None] + jnp.einsum('blgk,blgh,blghp->bghpk', b_c, d_s, x_c)
        return state, y

    s0 = jnp.zeros((bsz, SSD_GROUPS, SSD_HPG, SSD_HEADDIM, SSD_STATE), jnp.float32)
    xs = tuple(jnp.moveaxis(t, 1, 0) for t in (cc, bc, xdt, dec_out, dec_st, g_last))
    _, y_off = lax.scan(step, s0, xs)
    y = y_diag + jnp.moveaxis(y_off, 0, 1)
    y = y + xg * d_skip.reshape(SSD_GROUPS, SSD_HPG)[..., None]
    return y.reshape(bsz, s, SSD_INNER)


def even_layer(h, w_in, lb, a_norm_w, conv_w, conv_b, wa, ba, wx, bx, lam, w_out):
    proj = (h @ w_in).astype(jnp.float32)
    w = HG_WIDTH
    q = proj[..., 0:w]
    f = proj[..., w:2 * w]
    iv = proj[..., 2 * w:3 * w]
    g_a = proj[..., 3 * w:4 * w]
    x_b = proj[..., 4 * w:4 * w + LRU_WIDTH]
    g_b = proj[..., 4 * w + LRU_WIDTH:]
    o_a = rmsnorm(hgrn2(q, f, iv, lb), a_norm_w) * jax.nn.silu(g_a)
    o_b = rglru(causal_conv(x_b, conv_w, conv_b), wa, ba, wx, bx, lam) * jax.nn.silu(g_b)
    mix = jnp.concatenate([o_a, o_b], axis=-1).astype(h.dtype)
    return mix @ w_out


def odd_layer(h, w_in, conv_w, conv_b, dt_bias, a_log, d_skip, norm_w, w_out):
    bsz, s, _ = h.shape
    proj = (h @ w_in).astype(jnp.float32)
    z = proj[..., 0:SSD_INNER]
    xbc = proj[..., SSD_INNER:SSD_INNER + SSD_CONV_DIM]
    dt = proj[..., SSD_INNER + SSD_CONV_DIM:]
    xbc = jax.nn.silu(causal_conv(xbc, conv_w, conv_b))
    gk = SSD_GROUPS * SSD_STATE
    xs = xbc[..., 0:SSD_INNER].reshape(bsz, s, SSD_HEADS, SSD_HEADDIM)
    bm = xbc[..., SSD_INNER:SSD_INNER + gk].reshape(bsz, s, SSD_GROUPS, SSD_STATE)
    cm = xbc[..., SSD_INNER + gk:].reshape(bsz, s, SSD_GROUPS, SSD_STATE)
    dt = jax.nn.softplus(dt + dt_bias.astype(jnp.float32))
    y = ssd(xs, dt, a_log, bm, cm, d_skip.astype(jnp.float32))
    y = rmsnorm(y * jax.nn.silu(z), norm_w).astype(h.dtype)
    return y @ w_out


def setup_inputs(seed: int = 0) -> dict:
    key = jax.random.key(seed)
    ks = jax.random.split(key, 24)
    f32 = jnp.float32

    def nrm(k, shape, scale):
        return jax.random.normal(k, shape, f32) * scale

    x = jax.random.normal(ks[0], (BATCH, SEQ, D_MODEL), f32)
    norm_w = 1.0 + nrm(ks[1], (DEPTH, D_MODEL), 0.01)
    final_norm_w = 1.0 + nrm(ks[2], (D_MODEL,), 0.01)
    e_w_in = nrm(ks[3], (N_EVEN, D_MODEL, EVEN_IN), D_MODEL ** -0.5)
    lb_logits = nrm(ks[4], (DEPTH + 1, HG_WIDTH), 0.1)
    e_a_norm_w = 1.0 + nrm(ks[5], (N_EVEN, HG_WIDTH), 0.01)
    e_conv_w = nrm(ks[6], (N_EVEN, CONV_W, LRU_WIDTH), CONV_W ** -0.5)
    e_conv_b = nrm(ks[7], (N_EVEN, LRU_WIDTH), 0.01)
    e_wa = nrm(ks[8], (N_EVEN, LRU_BLOCKS, LRU_BLOCK, LRU_BLOCK), LRU_BLOCK ** -0.5)
    e_ba = nrm(ks[9], (N_EVEN, LRU_WIDTH), 0.01)
    e_wx = nrm(ks[10], (N_EVEN, LRU_BLOCKS, LRU_BLOCK, LRU_BLOCK), LRU_BLOCK ** -0.5)
    e_bx = nrm(ks[11], (N_EVEN, LRU_WIDTH), 0.01)
    a_pow = jax.random.uniform(ks[12], (N_EVEN, LRU_WIDTH), f32, 0.9, 0.999)
    a0 = a_pow ** (1.0 / LRU_C)
    e_lambda = jnp.log(a0) - jnp.log1p(-a0)
    e_w_out = nrm(ks[13], (N_EVEN, EVEN_MIX, D_MODEL), EVEN_MIX ** -0.5)
    o_w_in = nrm(ks[14], (N_ODD, D_MODEL, ODD_IN), D_MODEL ** -0.5)
    o_conv_w = nrm(ks[15], (N_ODD, CONV_W, SSD_CONV_DIM), CONV_W ** -0.5)
    o_conv_b = nrm(ks[16], (N_ODD, SSD_CONV_DIM), 0.01)
    dt0 = jnp.exp(jax.random.uniform(ks[17], (N_ODD, SSD_HEADS), f32, np.log(1e-3), np.log(1e-1)))
    o_dt_bias = dt0 + jnp.log(-jnp.expm1(-dt0))
    o_a_log = jnp.log(jax.random.uniform(ks[18], (N_ODD, SSD_HEADS), f32, 1.0, 16.0))
    o_d = 1.0 + nrm(ks[19], (N_ODD, SSD_HEADS), 0.1)
    o_norm_w = 1.0 + nrm(ks[20], (N_ODD, SSD_INNER), 0.01)
    o_w_out = nrm(ks[21], (N_ODD, SSD_INNER, D_MODEL), SSD_INNER ** -0.5)
    return {"x": x, "norm_w": norm_w, "e_w_in": e_w_in, "lb_logits": lb_logits,
            "e_a_norm_w": e_a_norm_w, "e_conv_w": e_conv_w, "e_conv_b": e_conv_b,
            "e_wa": e_wa, "e_ba": e_ba, "e_wx": e_wx, "e_bx": e_bx, "e_lambda": e_lambda,
            "e_w_out": e_w_out, "o_w_in": o_w_in, "o_conv_w": o_conv_w, "o_conv_b": o_conv_b,
            "o_dt_bias": o_dt_bias, "o_a_log": o_a_log, "o_d": o_d, "o_norm_w": o_norm_w,
            "o_w_out": o_w_out, "final_norm_w": final_norm_w}


def reference(x, norm_w, e_w_in, lb_logits, e_a_norm_w, e_conv_w, e_conv_b, e_wa, e_ba,
              e_wx, e_bx, e_lambda, e_w_out, o_w_in, o_conv_w, o_conv_b, o_dt_bias,
              o_a_log, o_d, o_norm_w, o_w_out, final_norm_w):
    lb_all = jnp.cumsum(jax.nn.softmax(lb_logits.astype(jnp.float32), axis=0), axis=0)
    h = x
    for l in range(DEPTH):
        hn = rmsnorm(h, norm_w[l]).astype(x.dtype)
        j = l // 2
        if l % 2 == 0:
            out = even_layer(hn, e_w_in[j], lb_all[l], e_a_norm_w[j], e_conv_w[j], e_conv_b[j],
                             e_wa[j], e_ba[j], e_wx[j], e_bx[j], e_lambda[j], e_w_out[j])
        else:
            out = odd_layer(hn, o_w_in[j], o_conv_w[j], o_conv_b[j], o_dt_bias[j], o_a_log[j],
                            o_d[j], o_norm_w[j], o_w_out[j])
        h = h + out.astype(x.dtype)
    return rmsnorm(h, final_norm_w).astype(x.dtype)
```

```python
import functools

import jax
import jax.numpy as jnp
from jax import lax
from jax.experimental import pallas as pl
from jax.experimental.pallas import tpu as pltpu

F32 = jnp.float32
BF16 = jnp.bfloat16

EPS = 1e-6
CHUNK = 64
LANES = 128
SUBLANES = 8
D_MODEL = 4096
HG_HEADS = 32
HG_DK = 128
LRU_BLOCKS = 32
LRU_BLOCK = 128
LRU_C = 8.0
CONV_W = 4
SSD_INNER = 8192
SSD_HEADS = 128
SSD_HEADDIM = 64
SSD_GROUPS = 8
SSD_STATE = 128
SSD_BC = SSD_GROUPS * SSD_STATE
SSD_CONV_DIM = SSD_INNER + 2 * SSD_BC
PAIRS_PER_GROUP = SSD_HEADS // SSD_GROUPS // 2
MIB = 1024 * 1024


def _cparams(semantics, vmem_mib):
    return pltpu.CompilerParams(dimension_semantics=semantics, vmem_limit_bytes=vmem_mib * MIB)


def _sigmoid(x):
    return jax.nn.sigmoid(x)


def _softplus(x):
    return jnp.maximum(x, 0.0) + jnp.log1p(jnp.exp(-jnp.abs(x)))


def _split3(x):
    hi = x.astype(BF16)
    r1 = x - hi.astype(F32)
    mid = r1.astype(BF16)
    lo = (r1 - mid.astype(F32)).astype(BF16)
    return hi, mid, lo


def _chunk_cumsum(tri, x):
    hi, mid, lo = _split3(x)
    return (jnp.dot(tri, hi, preferred_element_type=F32)
            + jnp.dot(tri, mid, preferred_element_type=F32)
            + jnp.dot(tri, lo, preferred_element_type=F32))


def _tri(n):
    r = lax.broadcasted_iota(jnp.int32, (n, n), 0)
    c = lax.broadcasted_iota(jnp.int32, (n, n), 1)
    return r >= c


def _norm_kernel(x_ref, w_ref, o_ref):
    x = x_ref[...]
    ms = jnp.mean(x * x, axis=-1, keepdims=True)
    o_ref[...] = (x * lax.rsqrt(ms + EPS) * w_ref[...]).astype(o_ref.dtype)


def _rmsnorm(x, w, out_dtype, tm=256):
    m, d = x.shape
    return pl.pallas_call(
        _norm_kernel,
        grid=(m // tm,),
        in_specs=[pl.BlockSpec((tm, d), lambda i: (i, 0)),
                  pl.BlockSpec((1, d), lambda i: (0, 0))],
        out_specs=pl.BlockSpec((tm, d), lambda i: (i, 0)),
        out_shape=jax.ShapeDtypeStruct((m, d), out_dtype),
        compiler_params=_cparams(("parallel",), 32),
        name="rmsnorm",
    )(x, w.reshape(1, d))


def _matmul_kernel(*refs, n_a, has_res):
    a_refs = refs[:n_a]
    b_ref = refs[n_a]
    r_ref = refs[n_a + 1] if has_res else None
    o_ref = refs[-1]
    k = pl.program_id(2)
    for idx in range(n_a):
        @pl.when(k == idx)
        def _(idx=idx):
            p = jnp.dot(a_refs[idx][...], b_ref[...], preferred_element_type=F32)
            if idx == 0:
                o_ref[...] = p
            else:
                o_ref[...] += p
            if has_res and idx == n_a - 1:
                o_ref[...] += r_ref[...]


def _matmul(a_list, w, residual=None, *, tm, tn, vmem_mib, name):
    tk = w.shape[0] // len(a_list)
    m = a_list[0][0].shape[0]
    n = w.shape[1]
    n_a = len(a_list)
    in_specs = [pl.BlockSpec((tm, tk), lambda i, j, k, cb=cb: (i, cb)) for _, cb in a_list]
    in_specs.append(pl.BlockSpec((tk, tn), lambda i, j, k: (k, j)))
    args = [a for a, _ in a_list] + [w]
    if residual is not None:
        in_specs.append(pl.BlockSpec((tm, tn), lambda i, j, k: (i, j)))
        args.append(residual)
    return pl.pallas_call(
        functools.partial(_matmul_kernel, n_a=n_a, has_res=residual is not None),
        grid=(m // tm, n // tn, n_a),
        in_specs=in_specs,
        out_specs=pl.BlockSpec((tm, tn), lambda i, j, k: (i, j)),
        out_shape=jax.ShapeDtypeStruct((m, n), F32),
        compiler_params=_cparams(("parallel", "parallel", "arbitrary"), vmem_mib),
        name=name,
    )(*args)


def _hgrn_kernel(q_ref, f_ref, i_ref, g_ref, lbl_ref, nw_ref, o_ref, st_ref, acc_ref, *, tb):
    @pl.when(pl.program_id(1) == 0)
    def _():
        st_ref[...] = jnp.zeros_like(st_ref)

    tri = _tri(CHUNK)
    tri_bf = tri.astype(BF16)
    nt = (((1,), (1,)), ((), ()))
    tn = (((0,), (0,)), ((), ()))

    def head_body(h, carry):
        cols = pl.ds(pl.multiple_of(h * HG_DK, HG_DK), HG_DK)
        lg = lbl_ref[:, cols]
        e = jnp.exp(lg - jnp.max(lg, axis=0, keepdims=True))
        lb = e[0:1] / jnp.sum(e, axis=0, keepdims=True)
        for c in range(tb // CHUNK):
            rows = pl.ds(c * CHUNK, CHUNK)
            qp = q_ref[rows, cols]
            z = f_ref[rows, cols]
            v = i_ref[rows, cols].astype(BF16)
            q = qp * _sigmoid(qp)
            k = (1.0 - lb) * _sigmoid(-z)
            logf = jnp.log1p(-k)
            b = _chunk_cumsum(tri_bf, logf)
            ref = b[CHUNK // 2 - 1:CHUNK // 2]
            blast = b[CHUNK - 1:CHUNK]
            q_in = q * jnp.exp(b - ref)
            k_in = k * jnp.exp(ref - b)
            scores = lax.dot_general(q_in.astype(BF16), k_in.astype(BF16), nt,
                                     preferred_element_type=F32)
            scores = jnp.where(tri, scores, 0.0)
            s_t = st_ref[h]
            q_out = q_in * jnp.exp(ref)
            o = (jnp.dot(scores.astype(BF16), v, preferred_element_type=F32)
                 + lax.dot_general(q_out.astype(BF16), s_t.astype(BF16), nt,
                                   preferred_element_type=F32))
            acc_ref[rows, cols] = o
            k_st = k_in * jnp.exp(blast - ref)
            st_ref[h] = (s_t * jnp.exp(blast)
                         + lax.dot_general(v, k_st.astype(BF16), tn, preferred_element_type=F32))
        return carry

    lax.fori_loop(0, HG_HEADS, head_body, 0)

    o = acc_ref[...]
    ms = jnp.mean(o * o, axis=-1, keepdims=True)
    g = g_ref[...]
    o_ref[...] = (o * lax.rsqrt(ms + EPS) * nw_ref[...] * (g * _sigmoid(g))).astype(o_ref.dtype)


def _hgrn(proj, lb_logits, a_norm_w, bsz, seq, tb=128):
    width = HG_HEADS * HG_DK
    nb = seq // tb

    def col(cb):
        return pl.BlockSpec((tb, width), lambda b, s, cb=cb: (b * nb + s, cb))

    return pl.pallas_call(
        functools.partial(_hgrn_kernel, tb=tb),
        grid=(bsz, nb),
        in_specs=[col(0), col(1), col(2), col(3),
                  pl.BlockSpec(lb_logits.shape, lambda b, s: (0, 0)),
                  pl.BlockSpec((1, width), lambda b, s: (0, 0))],
        out_specs=pl.BlockSpec((tb, width), lambda b, s: (b * nb + s, 0)),
        out_shape=jax.ShapeDtypeStruct((bsz * seq, width), BF16),
        scratch_shapes=[pltpu.VMEM((HG_HEADS, HG_DK, HG_DK), F32),
                        pltpu.VMEM((tb, width), F32)],
        compiler_params=_cparams(("parallel", "arbitrary"), 40),
        name="hgrn2",
    )(proj, proj, proj, proj, lb_logits, a_norm_w.reshape(1, width))


def _neg_expm1(t, exp_t):
    series = -t * (1.0 + t * (0.5 + t * (1.0 / 6.0 + t * (1.0 / 24.0 + t * (1.0 / 120.0)))))
    return jnp.where(t > -0.02, series, 1.0 - exp_t)


def _rglru_kernel(x_ref, g_ref, cw_ref, cb_ref, wg_ref, bg_ref, lam_ref, o_ref, xe_ref, h_ref, *, tb):
    @pl.when(pl.program_id(1) == 0)
    def _():
        xe_ref[0:SUBLANES, :] = jnp.zeros((SUBLANES, xe_ref.shape[1]), F32)
        h_ref[...] = jnp.zeros_like(h_ref)

    xe_ref[SUBLANES:SUBLANES + tb, :] = x_ref[...]
    row = lax.broadcasted_iota(jnp.int32, (SUBLANES, LRU_BLOCK), 0)

    def block_body(n, carry):
        cols = pl.ds(pl.multiple_of(n * LRU_BLOCK, LRU_BLOCK), LRU_BLOCK)
        xc = cb_ref[:, cols]
        for k in range(CONV_W):
            xc = xc + xe_ref[pl.ds(SUBLANES - (CONV_W - 1) + k, tb), cols] * cw_ref[k:k + 1, cols]
        gates = jnp.dot(xc.astype(BF16), wg_ref[n], preferred_element_type=F32)
        r = _sigmoid(gates[:, :LRU_BLOCK] + bg_ref[0:1, cols])
        ig = _sigmoid(gates[:, LRU_BLOCK:] + bg_ref[1:2, cols])
        log_a = (-LRU_C) * r * _softplus(-lam_ref[:, cols])
        a = jnp.exp(log_a)
        u = jnp.sqrt(_neg_expm1(2.0 * log_a, a * a)) * (ig * xc)
        h = h_ref[:, cols]
        outs = []
        for r8 in range(tb // SUBLANES):
            a8 = a[r8 * SUBLANES:(r8 + 1) * SUBLANES]
            u8 = u[r8 * SUBLANES:(r8 + 1) * SUBLANES]
            for sh in (1, 2, 4):
                keep = row >= sh
                a_s = jnp.where(keep, pltpu.roll(a8, sh, 0), 1.0)
                u_s = jnp.where(keep, pltpu.roll(u8, sh, 0), 0.0)
                u8 = a8 * u_s + u8
                a8 = a8 * a_s
            h8 = a8 * h + u8
            h = h8[SUBLANES - 1:SUBLANES]
            outs.append(h8)
        h_ref[:, cols] = h
        g = g_ref[:, cols]
        o_ref[:, cols] = (jnp.concatenate(outs, axis=0) * (g * _sigmoid(g))).astype(o_ref.dtype)
        return carry

    lax.fori_loop(0, LRU_BLOCKS, block_body, 0)
    xe_ref[0:SUBLANES, :] = xe_ref[tb:tb + SUBLANES, :]


def _rglru(proj, conv_w, conv_b, wa, ba, wx, bx, lam, bsz, seq, tb=128):
    width = LRU_BLOCKS * LRU_BLOCK
    nb = seq // tb
    wg = jnp.concatenate([wa, wx], axis=-1).astype(BF16)
    bg = jnp.stack([ba, bx], axis=0)

    def col(cb):
        return pl.BlockSpec((tb, width), lambda b, s, cb=cb: (b * nb + s, cb))

    def full(shape):
        return pl.BlockSpec(shape, lambda b, s: (0,) * len(shape))

    return pl.pallas_call(
        functools.partial(_rglru_kernel, tb=tb),
        grid=(bsz, nb),
        in_specs=[col(4), col(5), full(conv_w.shape), full((1, width)), full(wg.shape),
                  full(bg.shape), full((1, width))],
        out_specs=pl.BlockSpec((tb, width), lambda b, s: (b * nb + s, 0)),
        out_shape=jax.ShapeDtypeStruct((bsz * seq, width), BF16),
        scratch_shapes=[pltpu.VMEM((tb + SUBLANES, width), F32),
                        pltpu.VMEM((1, width), F32)],
        compiler_params=_cparams(("parallel", "arbitrary"), 32),
        name="rglru",
    )(proj, proj, conv_w, conv_b.reshape(1, width), wg, bg, lam.reshape(1, width))


def _ssd_kernel(z_ref, xbc_ref, dt_ref, cw_ref, cb_ref, dtb_ref, alog_ref, dsk_ref, nw_ref, o_ref,
                xe_ref, xc_ref, st_ref, y_ref, tt_ref, *, tb):
    @pl.when(pl.program_id(1) == 0)
    def _():
        xe_ref[0:SUBLANES, :] = jnp.zeros((SUBLANES, xe_ref.shape[1]), F32)
        st_ref[...] = jnp.zeros_like(st_ref)

    xe_ref[SUBLANES:SUBLANES + tb, :] = xbc_ref[...]
    slab = 512

    def conv_body(j, carry):
        cols = pl.ds(pl.multiple_of(j * slab, slab), slab)
        acc = cb_ref[:, cols]
        for k in range(CONV_W):
            acc = acc + xe_ref[pl.ds(SUBLANES - (CONV_W - 1) + k, tb), cols] * cw_ref[k:k + 1, cols]
        xc_ref[:, cols] = acc * _sigmoid(acc)
        return carry

    lax.fori_loop(0, SSD_CONV_DIM // slab, conv_body, 0)
    xe_ref[0:SUBLANES, :] = xe_ref[tb:tb + SUBLANES, :]

    tri_bf = _tri(CHUNK).astype(BF16)
    lane = lax.broadcasted_iota(jnp.int32, (CHUNK, LANES), 1)
    rowi = lax.broadcasted_iota(jnp.int32, (CHUNK, LANES), 0)
    lane_hi = (lane >= SSD_HEADDIM).astype(jnp.int32)
    causal2 = rowi >= (lane - SSD_HEADDIM * lane_hi)
    lane1_lo = lax.broadcasted_iota(jnp.int32, (1, LANES), 1) < SSD_HEADDIM
    lane_lo = lane < SSD_HEADDIM
    nt = (((1,), (1,)), ((), ()))
    a_neg = -jnp.exp(alog_ref[...])

    for c in range(tb // CHUNK):
        r0 = c * CHUNK
        dt = _softplus(dt_ref[r0:r0 + CHUNK, :] + dtb_ref[...])
        cs = _chunk_cumsum(tri_bf, dt * a_neg)
        w = dt * jnp.exp(cs[CHUNK - 1:CHUNK] - cs)
        for t, arr in enumerate((cs, dt, w)):
            tt_ref[t] = jnp.concatenate([arr, arr], axis=0).T

        def group_body(g, carry, r0=r0, cs=cs):
            b_g = xc_ref[r0:r0 + CHUNK, pl.ds(pl.multiple_of(SSD_INNER + g * SSD_STATE, SSD_STATE), SSD_STATE)]
            c_g = xc_ref[r0:r0 + CHUNK,
                         pl.ds(pl.multiple_of(SSD_INNER + SSD_BC + g * SSD_STATE, SSD_STATE), SSD_STATE)]
            c_bf = c_g.astype(BF16)
            b2 = jnp.concatenate([b_g, b_g], axis=0)
            cb2 = lax.dot_general(c_bf, b2.astype(BF16), nt, preferred_element_type=F32)
            bt2 = b2.T
            for i in range(PAIRS_PER_GROUP):
                h_a = g * (2 * PAIRS_PER_GROUP) + 2 * i
                cols = pl.ds(pl.multiple_of(h_a * SSD_HEADDIM, LANES), LANES)

                def pair_row(t, h_a=h_a):
                    return jnp.where(lane1_lo, tt_ref[t, pl.ds(h_a, 1), :], tt_ref[t, pl.ds(h_a + 1, 1), :])

                e2 = jnp.take_along_axis(cs, lane_hi + h_a, axis=1, mode="promise_in_bounds")
                dec = jnp.where(causal2, jnp.exp(e2 - pair_row(0)), 0.0)
                att = cb2 * dec * pair_row(1)
                lhs = jnp.concatenate([att, bt2 * pair_row(2)], axis=0).astype(BF16)
                x2 = xc_ref[r0:r0 + CHUNK, cols]
                rhs = jnp.concatenate([jnp.where(lane_lo, x2, 0.0), jnp.where(lane_lo, 0.0, x2)],
                                      axis=0).astype(BF16)
                res = jnp.dot(lhs, rhs, preferred_element_type=F32)
                s_p = st_ref[:, cols]
                e_out = jnp.exp(e2)
                y_off = e_out * jnp.dot(c_bf, s_p.astype(BF16), preferred_element_type=F32)
                y_ref[r0:r0 + CHUNK, cols] = res[0:CHUNK] + y_off + x2 * dsk_ref[:, cols]
                st_ref[:, cols] = s_p * e_out[CHUNK - 1:CHUNK] + res[CHUNK:3 * CHUNK]
            return carry

        lax.fori_loop(0, SSD_GROUPS, group_body, 0)

    z = z_ref[...]
    v = y_ref[...] * (z * _sigmoid(z))
    ms = jnp.mean(v * v, axis=-1, keepdims=True)
    o_ref[...] = (v * lax.rsqrt(ms + EPS) * nw_ref[...]).astype(o_ref.dtype)


def _ssd(z, xbc, dt, conv_w, conv_b, dt_bias, a_log, d_skip, norm_w, bsz, seq, tb=128):
    nb = seq // tb
    dsk = jnp.repeat(d_skip, SSD_HEADDIM).reshape(1, SSD_INNER)

    def rows(width):
        return pl.BlockSpec((tb, width), lambda b, s: (b * nb + s, 0))

    def full(shape):
        return pl.BlockSpec(shape, lambda b, s: (0,) * len(shape))

    return pl.pallas_call(
        functools.partial(_ssd_kernel, tb=tb),
        grid=(bsz, nb),
        in_specs=[rows(SSD_INNER), rows(SSD_CONV_DIM), rows(SSD_HEADS),
                  full(conv_w.shape), full((1, SSD_CONV_DIM)), full((1, SSD_HEADS)),
                  full((1, SSD_HEADS)), full((1, SSD_INNER)), full((1, SSD_INNER))],
        out_specs=rows(SSD_INNER),
        out_shape=jax.ShapeDtypeStruct((bsz * seq, SSD_INNER), BF16),
        scratch_shapes=[pltpu.VMEM((tb + SUBLANES, SSD_CONV_DIM), F32),
                        pltpu.VMEM((tb, SSD_CONV_DIM), F32),
                        pltpu.VMEM((SSD_STATE, SSD_INNER), F32),
                        pltpu.VMEM((tb, SSD_INNER), F32),
                        pltpu.VMEM((3, SSD_HEADS, LANES), F32)],
        compiler_params=_cparams(("parallel", "arbitrary"), 56),
        name="ssd",
    )(z, xbc, dt, conv_w, conv_b.reshape(1, SSD_CONV_DIM), dt_bias.reshape(1, SSD_HEADS),
      a_log.reshape(1, SSD_HEADS), dsk, norm_w.reshape(1, SSD_INNER))


def _even_layer(h, norm_w, w_in, lb_logits, a_norm_w, conv_w, conv_b, wa, ba, wx, bx, lam, w_out, bsz, seq):
    hn = _rmsnorm(h, norm_w, BF16)
    proj = _matmul([(hn, 0)], w_in.astype(BF16), tm=1024, tn=1024, vmem_mib=48, name="even_in_proj")
    o_a = _hgrn(proj, lb_logits, a_norm_w, bsz, seq)
    o_b = _rglru(proj, conv_w, conv_b, wa, ba, wx, bx, lam, bsz, seq)
    return _matmul([(o_a, 0), (o_b, 0)], w_out.astype(BF16), residual=h, tm=512, tn=1024, vmem_mib=48,
                   name="even_out_proj")


def _odd_layer(h, norm_w, w_in, conv_w, conv_b, dt_bias, a_log, d_skip, ssd_norm_w, w_out, bsz, seq):
    hn = _rmsnorm(h, norm_w, BF16)
    w_z = w_in[:, :SSD_INNER].astype(BF16)
    w_x = w_in[:, SSD_INNER:SSD_INNER + SSD_CONV_DIM].astype(BF16)
    w_dt = w_in[:, SSD_INNER + SSD_CONV_DIM:].astype(BF16)
    z = _matmul([(hn, 0)], w_z, tm=1024, tn=1024, vmem_mib=48, name="odd_in_proj_z")
    xbc = _matmul([(hn, 0)], w_x, tm=1024, tn=1024, vmem_mib=48, name="odd_in_proj_xbc")
    dt = _matmul([(hn, 0)], w_dt, tm=1024, tn=SSD_HEADS, vmem_mib=32, name="odd_in_proj_dt")
    y = _ssd(z, xbc, dt, conv_w, conv_b, dt_bias, a_log, d_skip, ssd_norm_w, bsz, seq)
    return _matmul([(y, 0), (y, 1)], w_out.astype(BF16), residual=h, tm=512, tn=1024, vmem_mib=48,
                   name="odd_out_proj")


def kernel(x, norm_w, e_w_in, lb_logits, e_a_norm_w, e_conv_w, e_conv_b, e_wa, e_ba, e_wx, e_bx, e_lambda,
           e_w_out, o_w_in, o_conv_w, o_conv_b, o_dt_bias, o_a_log, o_d, o_norm_w, o_w_out, final_norm_w):
    bsz, seq, d = x.shape
    h = x.reshape(bsz * seq, d)
    h = _even_layer(h, norm_w[0], e_w_in[0], lb_logits, e_a_norm_w[0], e_conv_w[0], e_conv_b[0], e_wa[0],
                    e_ba[0], e_wx[0], e_bx[0], e_lambda[0], e_w_out[0], bsz, seq)
    h = _odd_layer(h, norm_w[1], o_w_in[0], o_conv_w[0], o_conv_b[0], o_dt_bias[0], o_a_log[0], o_d[0],
                   o_norm_w[0], o_w_out[0], bsz, seq)
    return _rmsnorm(h, final_norm_w, x.dtype).reshape(bsz, seq, d)
```

```python
import functools

import jax
import jax.numpy as jnp
from jax import lax
from jax.experimental import pallas as pl
from jax.experimental.pallas import tpu as pltpu

F32 = jnp.float32
BF16 = jnp.bfloat16

EPS = 1e-6
CHUNK = 64
LANES = 128
SUBLANES = 8
D_MODEL = 4096
HG_HEADS = 32
HG_DK = 128
LRU_BLOCKS = 32
LRU_BLOCK = 128
LRU_C = 8.0
CONV_W = 4
SSD_INNER = 8192
SSD_HEADS = 128
SSD_HEADDIM = 64
SSD_GROUPS = 8
SSD_STATE = 128
SSD_BC = SSD_GROUPS * SSD_STATE
SSD_CONV_DIM = SSD_INNER + 2 * SSD_BC
PAIRS_PER_GROUP = SSD_HEADS // SSD_GROUPS // 2
MIB = 1024 * 1024


def _cparams(semantics, vmem_mib):
    return pltpu.CompilerParams(dimension_semantics=semantics, vmem_limit_bytes=vmem_mib * MIB)


def _sigmoid(x):
    return 0.5 + 0.5 * jnp.tanh(0.5 * x)


def _silu(x):
    return x * _sigmoid(x)


def _softplus(x):
    return jnp.maximum(x, 0.0) + jnp.log1p(jnp.exp(-jnp.abs(x)))


def _chunk_cumsum(tri, x, terms):
    acc = None
    rem = x
    for t in range(terms):
        piece = rem.astype(BF16)
        part = jnp.dot(tri, piece, preferred_element_type=F32)
        acc = part if acc is None else acc + part
        if t + 1 < terms:
            rem = rem - piece.astype(F32)
    return acc


def _causal_conv(xe, w, b):
    out = b + xe[SUBLANES:] * w[CONV_W - 1:CONV_W]
    for k in range(CONV_W - 1):
        shifted = pltpu.roll(xe, CONV_W - 1 - k, 0)
        out = out + shifted[SUBLANES:] * w[k:k + 1]
    return out


def _tri(n):
    r = lax.broadcasted_iota(jnp.int32, (n, n), 0)
    c = lax.broadcasted_iota(jnp.int32, (n, n), 1)
    return r >= c


def _norm_kernel(x_ref, w_ref, o_ref):
    x = x_ref[...]
    ms = jnp.mean(x * x, axis=-1, keepdims=True)
    o_ref[...] = (x * lax.rsqrt(ms + EPS) * w_ref[...]).astype(o_ref.dtype)


def _rmsnorm(x, w, out_dtype, tm=256):
    m, d = x.shape
    return pl.pallas_call(
        _norm_kernel,
        grid=(m // tm,),
        in_specs=[pl.BlockSpec((tm, d), lambda i: (i, 0)),
                  pl.BlockSpec((1, d), lambda i: (0, 0))],
        out_specs=pl.BlockSpec((tm, d), lambda i: (i, 0)),
        out_shape=jax.ShapeDtypeStruct((m, d), out_dtype),
        compiler_params=_cparams(("parallel",), 32),
        name="rmsnorm",
    )(x, w.reshape(1, d))


def _matmul_kernel(*refs, n_a, has_res):
    a_refs = refs[:n_a]
    b_ref = refs[n_a]
    r_ref = refs[n_a + 1] if has_res else None
    o_ref = refs[-1]
    k = pl.program_id(2)
    for idx in range(n_a):
        @pl.when(k == idx)
        def _(idx=idx):
            p = jnp.dot(a_refs[idx][...], b_ref[...], preferred_element_type=F32)
            if idx == 0:
                o_ref[...] = p
            else:
                o_ref[...] += p
            if has_res and idx == n_a - 1:
                o_ref[...] += r_ref[...]


def _matmul(a_list, w, residual=None, *, tm, tn, vmem_mib, name):
    tk = w.shape[0] // len(a_list)
    m = a_list[0][0].shape[0]
    n = w.shape[1]
    n_a = len(a_list)
    in_specs = [pl.BlockSpec((tm, tk), lambda i, j, k, cb=cb: (i, cb)) for _, cb in a_list]
    in_specs.append(pl.BlockSpec((tk, tn), lambda i, j, k: (k, j)))
    args = [a for a, _ in a_list] + [w]
    if residual is not None:
        in_specs.append(pl.BlockSpec((tm, tn), lambda i, j, k: (i, j)))
        args.append(residual)
    return pl.pallas_call(
        functools.partial(_matmul_kernel, n_a=n_a, has_res=residual is not None),
        grid=(m // tm, n // tn, n_a),
        in_specs=in_specs,
        out_specs=pl.BlockSpec((tm, tn), lambda i, j, k: (i, j)),
        out_shape=jax.ShapeDtypeStruct((m, n), F32),
        compiler_params=_cparams(("parallel", "parallel", "arbitrary"), vmem_mib),
        name=name,
    )(*args)


HG_SLAB = 512
ROW_SLAB = 16


def _hgrn_kernel(q_ref, f_ref, i_ref, g_ref, lbl_ref, nw_ref, o_ref,
                 st_ref, acc_ref, qin_ref, kin_ref, v_ref, qout_ref, kst_ref, gl_ref, *, tb):
    @pl.when(pl.program_id(1) == 0)
    def _():
        st_ref[...] = jnp.zeros_like(st_ref)

    n_chunks = tb // CHUNK
    n_slabs = (HG_HEADS * HG_DK) // HG_SLAB
    heads_per_slab = HG_SLAB // HG_DK
    tri = _tri(CHUNK)
    tri_bf = tri.astype(BF16)
    nt = (((1,), (1,)), ((), ()))
    tn = (((0,), (0,)), ((), ()))

    def phase_a(j):
        cols = pl.ds(pl.multiple_of(j * HG_SLAB, HG_SLAB), HG_SLAB)
        lg = lbl_ref[:, cols]
        e = jnp.exp(lg - jnp.max(lg, axis=0, keepdims=True))
        lb = e[0:1] / jnp.sum(e, axis=0, keepdims=True)
        half_k = 0.5 * (1.0 - lb)
        for c in range(n_chunks):
            rows = pl.ds(c * CHUNK, CHUNK)
            qp = q_ref[rows, cols]
            k = half_k - half_k * jnp.tanh(0.5 * f_ref[rows, cols])
            logf = jnp.log(1.0 - k)
            b = _chunk_cumsum(tri_bf, logf, terms=2)
            ref = b[CHUNK // 2 - 1:CHUNK // 2]
            blast = b[CHUNK - 1:CHUNK]
            d = b - ref
            q_in = _silu(qp) * jnp.exp(d)
            k_in = k * jnp.exp(-d)
            qin_ref[rows, cols] = q_in.astype(BF16)
            kin_ref[rows, cols] = k_in.astype(BF16)
            qout_ref[rows, cols] = (q_in * jnp.exp(ref)).astype(BF16)
            kst_ref[rows, cols] = (k_in * jnp.exp(blast - ref)).astype(BF16)
            v_ref[rows, cols] = i_ref[rows, cols].astype(BF16)
            gl_ref[c:c + 1, cols] = jnp.exp(blast)

    def phase_b(j):
        for hh in range(heads_per_slab):
            h = j * heads_per_slab + hh
            cols = pl.ds(pl.multiple_of(h * HG_DK, HG_DK), HG_DK)
            for c in range(n_chunks):
                rows = pl.ds(c * CHUNK, CHUNK)
                v = v_ref[rows, cols]
                scores = lax.dot_general(qin_ref[rows, cols], kin_ref[rows, cols], nt,
                                         preferred_element_type=F32)
                scores = jnp.where(tri, scores, 0.0).astype(BF16)
                s_t = st_ref[h]
                lhs = jnp.concatenate([qout_ref[rows, cols], scores], axis=1)
                rhs = jnp.concatenate([s_t.astype(BF16).T, v], axis=0)
                acc_ref[rows, cols] = jnp.dot(lhs, rhs, preferred_element_type=F32)
                st_ref[h] = (s_t * gl_ref[c:c + 1, cols]
                             + lax.dot_general(v, kst_ref[rows, cols], tn, preferred_element_type=F32))

    phase_a(0)

    def skew_body(j, carry):
        phase_a(j + 1)
        phase_b(j)
        return carry

    lax.fori_loop(0, n_slabs - 1, skew_body, 0)
    phase_b(n_slabs - 1)

    def norm_body(r, carry):
        rows = pl.ds(pl.multiple_of(r * ROW_SLAB, ROW_SLAB), ROW_SLAB)
        o = acc_ref[rows, :]
        ms = jnp.mean(o * o, axis=-1, keepdims=True)
        o_ref[rows, :] = (o * lax.rsqrt(ms + EPS) * nw_ref[...] * _silu(g_ref[rows, :])).astype(o_ref.dtype)
        return carry

    lax.fori_loop(0, tb // ROW_SLAB, norm_body, 0)


def _hgrn(proj, lb_logits, a_norm_w, bsz, seq, tb=128):
    width = HG_HEADS * HG_DK
    nb = seq // tb

    def col(cb):
        return pl.BlockSpec((tb, width), lambda b, s, cb=cb: (b * nb + s, cb))

    return pl.pallas_call(
        functools.partial(_hgrn_kernel, tb=tb),
        grid=(bsz, nb),
        in_specs=[col(0), col(1), col(2), col(3),
                  pl.BlockSpec(lb_logits.shape, lambda b, s: (0, 0)),
                  pl.BlockSpec((1, width), lambda b, s: (0, 0))],
        out_specs=pl.BlockSpec((tb, width), lambda b, s: (b * nb + s, 0)),
        out_shape=jax.ShapeDtypeStruct((bsz * seq, width), BF16),
        scratch_shapes=[pltpu.VMEM((HG_HEADS, HG_DK, HG_DK), F32),
                        pltpu.VMEM((tb, width), F32)]
                       + [pltpu.VMEM((tb, width), BF16)] * 5
                       + [pltpu.VMEM((tb // CHUNK, width), F32)],
        compiler_params=_cparams(("parallel", "arbitrary"), 40),
        name="hgrn2",
    )(proj, proj, proj, proj, lb_logits, a_norm_w.reshape(1, width))


def _rglru_kernel(x_ref, g_ref, cw_ref, cb_ref, wg_ref, bg_ref, lam_ref, o_ref, xe_ref, h_ref, *, tb):
    @pl.when(pl.program_id(1) == 0)
    def _():
        xe_ref[0:SUBLANES, :] = jnp.zeros((SUBLANES, xe_ref.shape[1]), F32)
        h_ref[...] = jnp.zeros_like(h_ref)

    xe_ref[SUBLANES:SUBLANES + tb, :] = x_ref[...]
    row = lax.broadcasted_iota(jnp.int32, (SUBLANES, LRU_BLOCK), 0)

    def block_body(n, carry):
        cols = pl.ds(pl.multiple_of(n * LRU_BLOCK, LRU_BLOCK), LRU_BLOCK)
        xc = _causal_conv(xe_ref[:, cols], cw_ref[:, cols], cb_ref[:, cols])
        gates = jnp.dot(xc.astype(BF16), wg_ref[n], preferred_element_type=F32)
        quarter = (0.25 * LRU_C) * _softplus(-lam_ref[:, cols])
        t_r = jnp.tanh(0.5 * (gates[:, :LRU_BLOCK] + bg_ref[0:1, cols]))
        tau = jnp.tanh(-quarter - quarter * t_r)
        rho = 1.0 / (1.0 - tau)
        a = (1.0 + tau) * rho
        neg_tau = -tau
        root = jnp.where(neg_tau > 0.0, neg_tau * lax.rsqrt(neg_tau), 0.0)
        ig = _sigmoid(gates[:, LRU_BLOCK:] + bg_ref[1:2, cols])
        u = (2.0 * rho * root) * (ig * xc)
        h = h_ref[:, cols]
        outs = []
        for r8 in range(tb // SUBLANES):
            a8 = a[r8 * SUBLANES:(r8 + 1) * SUBLANES]
            u8 = u[r8 * SUBLANES:(r8 + 1) * SUBLANES]
            for sh in (1, 2, 4):
                keep = row >= sh
                a_s = jnp.where(keep, pltpu.roll(a8, sh, 0), 1.0)
                u_s = jnp.where(keep, pltpu.roll(u8, sh, 0), 0.0)
                u8 = a8 * u_s + u8
                a8 = a8 * a_s
            h8 = a8 * h + u8
            h = h8[SUBLANES - 1:SUBLANES]
            outs.append(h8)
        h_ref[:, cols] = h
        o_ref[:, cols] = (jnp.concatenate(outs, axis=0) * _silu(g_ref[:, cols])).astype(o_ref.dtype)
        return carry

    lax.fori_loop(0, LRU_BLOCKS, block_body, 0, unroll=2)
    xe_ref[0:SUBLANES, :] = xe_ref[tb:tb + SUBLANES, :]


def _rglru(proj, conv_w, conv_b, wa, ba, wx, bx, lam, bsz, seq, tb=128):
    width = LRU_BLOCKS * LRU_BLOCK
    nb = seq // tb
    wg = jnp.concatenate([wa, wx], axis=-1).astype(BF16)
    bg = jnp.stack([ba, bx], axis=0)

    def col(cb):
        return pl.BlockSpec((tb, width), lambda b, s, cb=cb: (b * nb + s, cb))

    def full(shape):
        return pl.BlockSpec(shape, lambda b, s: (0,) * len(shape))

    return pl.pallas_call(
        functools.partial(_rglru_kernel, tb=tb),
        grid=(bsz, nb),
        in_specs=[col(4), col(5), full(conv_w.shape), full((1, width)), full(wg.shape),
                  full(bg.shape), full((1, width))],
        out_specs=pl.BlockSpec((tb, width), lambda b, s: (b * nb + s, 0)),
        out_shape=jax.ShapeDtypeStruct((bsz * seq, width), BF16),
        scratch_shapes=[pltpu.VMEM((tb + SUBLANES, width), F32),
                        pltpu.VMEM((1, width), F32)],
        compiler_params=_cparams(("parallel", "arbitrary"), 32),
        name="rglru",
    )(proj, proj, conv_w, conv_b.reshape(1, width), wg, bg, lam.reshape(1, width))


def _ssd_kernel(z_ref, xbc_ref, dt_ref, cw_ref, cb_ref, dtb_ref, alog_ref, dsk_ref, nw_ref, o_ref,
                xe_ref, xc_ref, st_ref, st2_ref, y_ref, tt_ref, *, tb):
    @pl.when(pl.program_id(1) == 0)
    def _():
        xe_ref[0:SUBLANES, :] = jnp.zeros((SUBLANES, xe_ref.shape[1]), F32)
        st_ref[...] = jnp.zeros_like(st_ref)

    xe_ref[SUBLANES:SUBLANES + tb, :] = xbc_ref[...]
    slab = 512

    def conv_body(j, carry):
        cols = pl.ds(pl.multiple_of(j * slab, slab), slab)
        xc_ref[:, cols] = _silu(_causal_conv(xe_ref[:, cols], cw_ref[:, cols], cb_ref[:, cols]))
        return carry

    lax.fori_loop(0, SSD_CONV_DIM // slab, conv_body, 0)
    xe_ref[0:SUBLANES, :] = xe_ref[tb:tb + SUBLANES, :]

    tri_bf = _tri(CHUNK).astype(BF16)
    lane = lax.broadcasted_iota(jnp.int32, (CHUNK, LANES), 1)
    rowi = lax.broadcasted_iota(jnp.int32, (CHUNK, LANES), 0)
    lane_hi = (lane >= SSD_HEADDIM).astype(jnp.int32)
    causal2 = rowi >= (lane - SSD_HEADDIM * lane_hi)
    lane1_lo = lax.broadcasted_iota(jnp.int32, (1, LANES), 1) < SSD_HEADDIM
    lane_lo = lane < SSD_HEADDIM
    nt = (((1,), (1,)), ((), ()))
    a_neg = -jnp.exp(alog_ref[...])

    for c in range(tb // CHUNK):
        r0 = c * CHUNK
        dt = _softplus(dt_ref[r0:r0 + CHUNK, :] + dtb_ref[...])
        cs = _chunk_cumsum(tri_bf, dt * a_neg, terms=3)
        w = dt * jnp.exp(cs[CHUNK - 1:CHUNK] - cs)
        for t, arr in enumerate((cs, dt, w)):
            tt_ref[t] = jnp.concatenate([arr, arr], axis=0).T

        st_in, st_out = (st_ref, st2_ref) if c % 2 == 0 else (st2_ref, st_ref)

        def group_body(g, carry, r0=r0, cs=cs, st_in=st_in, st_out=st_out):
            b_g = xc_ref[r0:r0 + CHUNK, pl.ds(pl.multiple_of(SSD_INNER + g * SSD_STATE, SSD_STATE), SSD_STATE)]
            c_g = xc_ref[r0:r0 + CHUNK,
                         pl.ds(pl.multiple_of(SSD_INNER + SSD_BC + g * SSD_STATE, SSD_STATE), SSD_STATE)]
            c_bf = c_g.astype(BF16)
            b2 = jnp.concatenate([b_g, b_g], axis=0)
            cb2 = lax.dot_general(c_bf, b2.astype(BF16), nt, preferred_element_type=F32)
            bt2 = b2.T
            for i in range(PAIRS_PER_GROUP):
                h_a = g * (2 * PAIRS_PER_GROUP) + 2 * i
                cols = pl.ds(pl.multiple_of(h_a * SSD_HEADDIM, LANES), LANES)

                def pair_row(t, h_a=h_a):
                    return jnp.where(lane1_lo, tt_ref[t, pl.ds(h_a, 1), :], tt_ref[t, pl.ds(h_a + 1, 1), :])

                e2 = jnp.take_along_axis(cs, lane_hi + h_a, axis=1, mode="promise_in_bounds")
                dec = jnp.where(causal2, jnp.exp(e2 - pair_row(0)), 0.0)
                att = cb2 * dec * pair_row(1)
                lhs = jnp.concatenate([att, bt2 * pair_row(2)], axis=0).astype(BF16)
                x2 = xc_ref[r0:r0 + CHUNK, cols]
                rhs = jnp.concatenate([jnp.where(lane_lo, x2, 0.0), jnp.where(lane_lo, 0.0, x2)],
                                      axis=0).astype(BF16)
                res = jnp.dot(lhs, rhs, preferred_element_type=F32)
                s_p = st_in[:, cols]
                e_out = jnp.exp(e2)
                y_off = e_out * jnp.dot(c_bf, s_p.astype(BF16), preferred_element_type=F32)
                y_ref[r0:r0 + CHUNK, cols] = res[0:CHUNK] + y_off + x2 * dsk_ref[:, cols]
                st_out[:, cols] = s_p * e_out[CHUNK - 1:CHUNK] + res[CHUNK:3 * CHUNK]
            return carry

        lax.fori_loop(0, SSD_GROUPS, group_body, 0)

    def norm_body(r, carry):
        rows = pl.ds(pl.multiple_of(r * ROW_SLAB, ROW_SLAB), ROW_SLAB)
        v = y_ref[rows, :] * _silu(z_ref[rows, :])
        ms = jnp.mean(v * v, axis=-1, keepdims=True)
        o_ref[rows, :] = (v * lax.rsqrt(ms + EPS) * nw_ref[...]).astype(o_ref.dtype)
        return carry

    lax.fori_loop(0, tb // ROW_SLAB, norm_body, 0)


def _ssd(z, xbc, dt, conv_w, conv_b, dt_bias, a_log, d_skip, norm_w, bsz, seq, tb=128):
    assert (tb // CHUNK) % 2 == 0, "the state ping-pong needs an even number of chunks per block"
    nb = seq // tb
    dsk = jnp.repeat(d_skip, SSD_HEADDIM).reshape(1, SSD_INNER)

    def rows(width):
        return pl.BlockSpec((tb, width), lambda b, s: (b * nb + s, 0))

    def full(shape):
        return pl.BlockSpec(shape, lambda b, s: (0,) * len(shape))

    return pl.pallas_call(
        functools.partial(_ssd_kernel, tb=tb),
        grid=(bsz, nb),
        in_specs=[rows(SSD_INNER), rows(SSD_CONV_DIM), rows(SSD_HEADS),
                  full(conv_w.shape), full((1, SSD_CONV_DIM)), full((1, SSD_HEADS)),
                  full((1, SSD_HEADS)), full((1, SSD_INNER)), full((1, SSD_INNER))],
        out_specs=rows(SSD_INNER),
        out_shape=jax.ShapeDtypeStruct((bsz * seq, SSD_INNER), BF16),
        scratch_shapes=[pltpu.VMEM((tb + SUBLANES, SSD_CONV_DIM), F32),
                        pltpu.VMEM((tb, SSD_CONV_DIM), F32),
                        pltpu.VMEM((SSD_STATE, SSD_INNER), F32),
                        pltpu.VMEM((SSD_STATE, SSD_INNER), F32),
                        pltpu.VMEM((tb, SSD_INNER), F32),
                        pltpu.VMEM((3, SSD_HEADS, LANES), F32)],
        compiler_params=_cparams(("parallel", "arbitrary"), 56),
        name="ssd",
    )(z, xbc, dt, conv_w, conv_b.reshape(1, SSD_CONV_DIM), dt_bias.reshape(1, SSD_HEADS),
      a_log.reshape(1, SSD_HEADS), dsk, norm_w.reshape(1, SSD_INNER))


def _even_layer(h, norm_w, w_in, lb_logits, a_norm_w, conv_w, conv_b, wa, ba, wx, bx, lam, w_out, bsz, seq):
    hn = _rmsnorm(h, norm_w, BF16)
    proj = _matmul([(hn, 0)], w_in.astype(BF16), tm=1024, tn=1024, vmem_mib=48, name="even_in_proj")
    o_a = _hgrn(proj, lb_logits, a_norm_w, bsz, seq)
    o_b = _rglru(proj, conv_w, conv_b, wa, ba, wx, bx, lam, bsz, seq)
    return _matmul([(o_a, 0), (o_b, 0)], w_out.astype(BF16), residual=h, tm=512, tn=1024, vmem_mib=48,
                   name="even_out_proj")


def _odd_layer(h, norm_w, w_in, conv_w, conv_b, dt_bias, a_log, d_skip, ssd_norm_w, w_out, bsz, seq):
    hn = _rmsnorm(h, norm_w, BF16)
    w_z = w_in[:, :SSD_INNER].astype(BF16)
    w_x = w_in[:, SSD_INNER:SSD_INNER + SSD_CONV_DIM].astype(BF16)
    w_dt = w_in[:, SSD_INNER + SSD_CONV_DIM:].astype(BF16)
    z = _matmul([(hn, 0)], w_z, tm=1024, tn=1024, vmem_mib=48, name="odd_in_proj_z")
    xbc = _matmul([(hn, 0)], w_x, tm=1024, tn=1024, vmem_mib=48, name="odd_in_proj_xbc")
    dt = _matmul([(hn, 0)], w_dt, tm=1024, tn=SSD_HEADS, vmem_mib=32, name="odd_in_proj_dt")
    y = _ssd(z, xbc, dt, conv_w, conv_b, dt_bias, a_log, d_skip, ssd_norm_w, bsz, seq)
    return _matmul([(y, 0), (y, 1)], w_out.astype(BF16), residual=h, tm=512, tn=1024, vmem_mib=48,
                   name="odd_out_proj")


def kernel(x, norm_w, e_w_in, lb_logits, e_a_norm_w, e_conv_w, e_conv_b, e_wa, e_ba, e_wx, e_bx, e_lambda,
           e_w_out, o_w_in, o_conv_w, o_conv_b, o_dt_bias, o_a_log, o_d, o_norm_w, o_w_out, final_norm_w):
    bsz, seq, d = x.shape
    h = x.reshape(bsz * seq, d)
    h = _even_layer(h, norm_w[0], e_w_in[0], lb_logits, e_a_norm_w[0], e_conv_w[0], e_conv_b[0], e_wa[0],
                    e_ba[0], e_wx[0], e_bx[0], e_lambda[0], e_w_out[0], bsz, seq)
    h = _odd_layer(h, norm_w[1], o_w_in[0], o_conv_w[0], o_conv_b[0], o_dt_bias[0], o_a_log[0], o_d[0],
                   o_norm_w[0], o_w_out[0], bsz, seq)
    return _rmsnorm(h, final_norm_w, x.dtype).reshape(bsz, seq, d)
```

```python
import functools

import jax
import jax.numpy as jnp
from jax import lax
from jax.experimental import pallas as pl
from jax.experimental.pallas import tpu as pltpu

F32 = jnp.float32
BF16 = jnp.bfloat16

EPS = 1e-6
CHUNK = 64
LANES = 128
SUBLANES = 8
D_MODEL = 4096
HG_HEADS = 32
HG_DK = 128
LRU_BLOCKS = 32
LRU_BLOCK = 128
LRU_C = 8.0
CONV_W = 4
SSD_INNER = 8192
SSD_HEADS = 128
SSD_HEADDIM = 64
SSD_GROUPS = 8
SSD_STATE = 128
SSD_BC = SSD_GROUPS * SSD_STATE
SSD_CONV_DIM = SSD_INNER + 2 * SSD_BC
PAIRS_PER_GROUP = SSD_HEADS // SSD_GROUPS // 2
MIB = 1024 * 1024


def _cparams(semantics, vmem_mib):
    return pltpu.CompilerParams(dimension_semantics=semantics, vmem_limit_bytes=vmem_mib * MIB)


def _sigmoid(x):
    return 0.5 + 0.5 * jnp.tanh(0.5 * x)


def _silu(x):
    return x * _sigmoid(x)


def _softplus(x):
    return jnp.maximum(x, 0.0) + jnp.log1p(jnp.exp(-jnp.abs(x)))


def _chunk_cumsum(tri, x, terms):
    acc = None
    rem = x
    for t in range(terms):
        piece = rem.astype(BF16)
        part = jnp.dot(tri, piece, preferred_element_type=F32)
        acc = part if acc is None else acc + part
        if t + 1 < terms:
            rem = rem - piece.astype(F32)
    return acc


def _causal_conv(xe, w, b):
    out = b + xe[SUBLANES:] * w[CONV_W - 1:CONV_W]
    for k in range(CONV_W - 1):
        shifted = pltpu.roll(xe, CONV_W - 1 - k, 0)
        out = out + shifted[SUBLANES:] * w[k:k + 1]
    return out


def _causal_conv_rows(hist, x_ref, cols, n_rows, w, b):
    row = lax.broadcasted_iota(jnp.int32, hist.shape, 0)
    shifts = range(1, CONV_W)
    prev_rot = [pltpu.roll(hist, s, 0) for s in shifts]
    for r0 in range(0, n_rows, SUBLANES):
        cur = x_ref[r0:r0 + SUBLANES, cols]
        cur_rot = [pltpu.roll(cur, s, 0) for s in shifts]
        out = b + cur * w[CONV_W - 1:CONV_W]
        for s, c_rot, p_rot in zip(shifts, cur_rot, prev_rot):
            out = out + jnp.where(row >= s, c_rot, p_rot) * w[CONV_W - 1 - s:CONV_W - s]
        prev_rot = cur_rot
        yield r0, out


def _tri(n):
    r = lax.broadcasted_iota(jnp.int32, (n, n), 0)
    c = lax.broadcasted_iota(jnp.int32, (n, n), 1)
    return r >= c


def _norm_kernel(x_ref, w_ref, o_ref):
    x = x_ref[...]
    ms = jnp.mean(x * x, axis=-1, keepdims=True)
    o_ref[...] = (x * lax.rsqrt(ms + EPS) * w_ref[...]).astype(o_ref.dtype)


def _rmsnorm(x, w, out_dtype, tm=256):
    m, d = x.shape
    return pl.pallas_call(
        _norm_kernel,
        grid=(m // tm,),
        in_specs=[pl.BlockSpec((tm, d), lambda i: (i, 0)),
                  pl.BlockSpec((1, d), lambda i: (0, 0))],
        out_specs=pl.BlockSpec((tm, d), lambda i: (i, 0)),
        out_shape=jax.ShapeDtypeStruct((m, d), out_dtype),
        compiler_params=_cparams(("parallel",), 32),
        name="rmsnorm",
    )(x, w.reshape(1, d))


def _matmul_kernel(*refs, n_a, has_res):
    a_refs = refs[:n_a]
    b_ref = refs[n_a]
    r_ref = refs[n_a + 1] if has_res else None
    o_ref = refs[-1]
    k = pl.program_id(2)
    for idx in range(n_a):
        @pl.when(k == idx)
        def _(idx=idx):
            p = jnp.dot(a_refs[idx][...], b_ref[...], preferred_element_type=F32)
            if idx == 0:
                o_ref[...] = p
            else:
                o_ref[...] += p
            if has_res and idx == n_a - 1:
                o_ref[...] += r_ref[...]


def _matmul(a_list, w, residual=None, *, tm, tn, vmem_mib, name):
    tk = w.shape[0] // len(a_list)
    m = a_list[0][0].shape[0]
    n = w.shape[1]
    n_a = len(a_list)
    in_specs = [pl.BlockSpec((tm, tk), lambda i, j, k, cb=cb: (i, cb)) for _, cb in a_list]
    in_specs.append(pl.BlockSpec((tk, tn), lambda i, j, k: (k, j)))
    args = [a for a, _ in a_list] + [w]
    if residual is not None:
        in_specs.append(pl.BlockSpec((tm, tn), lambda i, j, k: (i, j)))
        args.append(residual)
    return pl.pallas_call(
        functools.partial(_matmul_kernel, n_a=n_a, has_res=residual is not None),
        grid=(m // tm, n // tn, n_a),
        in_specs=in_specs,
        out_specs=pl.BlockSpec((tm, tn), lambda i, j, k: (i, j)),
        out_shape=jax.ShapeDtypeStruct((m, n), F32),
        compiler_params=_cparams(("parallel", "parallel", "arbitrary"), vmem_mib),
        name=name,
    )(*args)


HG_SLAB = 512
ROW_SLAB = 16


def _hgrn_kernel(q_ref, f_ref, i_ref, g_ref, lbl_ref, nw_ref, o_ref,
                 st_ref, acc_ref, qin_ref, kin_ref, v_ref, qout_ref, kst_ref, gl_ref, *, tb):
    @pl.when(pl.program_id(1) == 0)
    def _():
        st_ref[...] = jnp.zeros_like(st_ref)

    n_chunks = tb // CHUNK
    n_slabs = (HG_HEADS * HG_DK) // HG_SLAB
    heads_per_slab = HG_SLAB // HG_DK
    tri = _tri(CHUNK)
    tri_bf = tri.astype(BF16)
    nt = (((1,), (1,)), ((), ()))
    tn = (((0,), (0,)), ((), ()))

    def phase_a(j, p):
        cols = pl.ds(pl.multiple_of(j * HG_SLAB, HG_SLAB), HG_SLAB)
        lg = lbl_ref[:, cols]
        e = jnp.exp(lg - jnp.max(lg, axis=0, keepdims=True))
        lb = e[0:1] / jnp.sum(e, axis=0, keepdims=True)
        half_k = 0.5 * (1.0 - lb)
        for c in range(n_chunks):
            rows = pl.ds(c * CHUNK, CHUNK)
            qp = q_ref[rows, cols]
            k = half_k - half_k * jnp.tanh(0.5 * f_ref[rows, cols])
            logf = jnp.log(1.0 - k)
            b = _chunk_cumsum(tri_bf, logf, terms=2)
            ref = b[CHUNK // 2 - 1:CHUNK // 2]
            blast = b[CHUNK - 1:CHUNK]
            d = b - ref
            q_in = _silu(qp) * jnp.exp(d)
            k_in = k * jnp.exp(-d)
            qin_ref[p, rows, :] = q_in.astype(BF16)
            kin_ref[p, rows, :] = k_in.astype(BF16)
            qout_ref[p, rows, :] = (q_in * jnp.exp(ref)).astype(BF16)
            kst_ref[p, rows, :] = (k_in * jnp.exp(blast - ref)).astype(BF16)
            v_ref[p, rows, :] = i_ref[rows, cols].astype(BF16)
            gl_ref[p, c:c + 1, :] = jnp.exp(blast)

    def phase_b(j, p):
        states = [st_ref[j * heads_per_slab + hh] for hh in range(heads_per_slab)]
        for hh in range(heads_per_slab):
            lanes = slice(hh * HG_DK, (hh + 1) * HG_DK)
            out_cols = pl.ds(pl.multiple_of((j * heads_per_slab + hh) * HG_DK, HG_DK), HG_DK)
            s_t = states[hh]
            for c in range(n_chunks):
                rows = pl.ds(c * CHUNK, CHUNK)
                v = v_ref[p, rows, lanes]
                scores = lax.dot_general(qin_ref[p, rows, lanes], kin_ref[p, rows, lanes], nt,
                                         preferred_element_type=F32)
                scores = jnp.where(tri, scores, 0.0).astype(BF16)
                lhs = jnp.concatenate([qout_ref[p, rows, lanes], scores], axis=1)
                rhs = jnp.concatenate([s_t.astype(BF16).T, v], axis=0)
                acc_ref[rows, out_cols] = jnp.dot(lhs, rhs, preferred_element_type=F32)
                s_t = (s_t * gl_ref[p, c:c + 1, lanes]
                       + lax.dot_general(v, kst_ref[p, rows, lanes], tn, preferred_element_type=F32))
            states[hh] = s_t
        for hh in range(heads_per_slab):
            st_ref[j * heads_per_slab + hh] = states[hh]

    phase_a(0, 0)

    def skew_body(jj, carry):
        j = 2 * jj
        phase_a(j + 1, 1)
        phase_b(j, 0)
        phase_a(j + 2, 0)
        phase_b(j + 1, 1)
        return carry

    lax.fori_loop(0, n_slabs // 2 - 1, skew_body, 0)
    phase_a(n_slabs - 1, 1)
    phase_b(n_slabs - 2, 0)
    phase_b(n_slabs - 1, 1)

    def norm_body(r, carry):
        rows = pl.ds(pl.multiple_of(r * ROW_SLAB, ROW_SLAB), ROW_SLAB)
        o = acc_ref[rows, :]
        ms = jnp.mean(o * o, axis=-1, keepdims=True)
        o_ref[rows, :] = (o * lax.rsqrt(ms + EPS) * nw_ref[...] * _silu(g_ref[rows, :])).astype(o_ref.dtype)
        return carry

    lax.fori_loop(0, tb // ROW_SLAB, norm_body, 0)


def _hgrn(proj, lb_logits, a_norm_w, bsz, seq, tb=128):
    width = HG_HEADS * HG_DK
    nb = seq // tb

    def col(cb):
        return pl.BlockSpec((tb, width), lambda b, s, cb=cb: (b * nb + s, cb))

    return pl.pallas_call(
        functools.partial(_hgrn_kernel, tb=tb),
        grid=(bsz, nb),
        in_specs=[col(0), col(1), col(2), col(3),
                  pl.BlockSpec(lb_logits.shape, lambda b, s: (0, 0)),
                  pl.BlockSpec((1, width), lambda b, s: (0, 0))],
        out_specs=pl.BlockSpec((tb, width), lambda b, s: (b * nb + s, 0)),
        out_shape=jax.ShapeDtypeStruct((bsz * seq, width), BF16),
        scratch_shapes=[pltpu.VMEM((HG_HEADS, HG_DK, HG_DK), F32),
                        pltpu.VMEM((tb, width), F32)]
                       + [pltpu.VMEM((2, tb, HG_SLAB), BF16)] * 5
                       + [pltpu.VMEM((2, tb // CHUNK, HG_SLAB), F32)],
        compiler_params=_cparams(("parallel", "arbitrary"), 40),
        name="hgrn2",
    )(proj, proj, proj, proj, lb_logits, a_norm_w.reshape(1, width))


def _rglru_kernel(x_ref, g_ref, cw_ref, cb_ref, wg_ref, bg_ref, lam_ref, o_ref, xe_ref, h_ref, *, tb):
    @pl.when(pl.program_id(1) == 0)
    def _():
        xe_ref[0:SUBLANES, :] = jnp.zeros((SUBLANES, xe_ref.shape[1]), F32)
        h_ref[...] = jnp.zeros_like(h_ref)

    xe_ref[SUBLANES:SUBLANES + tb, :] = x_ref[...]
    row = lax.broadcasted_iota(jnp.int32, (SUBLANES, LRU_BLOCK), 0)

    def block_body(n, carry):
        cols = pl.ds(pl.multiple_of(n * LRU_BLOCK, LRU_BLOCK), LRU_BLOCK)
        xc = _causal_conv(xe_ref[:, cols], cw_ref[:, cols], cb_ref[:, cols])
        gates = jnp.dot(xc.astype(BF16), wg_ref[n], preferred_element_type=F32)
        quarter = (0.25 * LRU_C) * _softplus(-lam_ref[:, cols])
        t_r = jnp.tanh(0.5 * (gates[:, :LRU_BLOCK] + bg_ref[0:1, cols]))
        tau = jnp.tanh(-quarter - quarter * t_r)
        rho = 1.0 / (1.0 - tau)
        a = (1.0 + tau) * rho
        neg_tau = -tau
        root = jnp.where(neg_tau > 0.0, neg_tau * lax.rsqrt(neg_tau), 0.0)
        ig = _sigmoid(gates[:, LRU_BLOCK:] + bg_ref[1:2, cols])
        u = (2.0 * rho * root) * (ig * xc)
        h = h_ref[:, cols]
        outs = []
        for r8 in range(tb // SUBLANES):
            a8 = a[r8 * SUBLANES:(r8 + 1) * SUBLANES]
            u8 = u[r8 * SUBLANES:(r8 + 1) * SUBLANES]
            for sh in (1, 2, 4):
                keep = row >= sh
                a_s = jnp.where(keep, pltpu.roll(a8, sh, 0), 1.0)
                u_s = jnp.where(keep, pltpu.roll(u8, sh, 0), 0.0)
                u8 = a8 * u_s + u8
                a8 = a8 * a_s
            h8 = a8 * h + u8
            h = h8[SUBLANES - 1:SUBLANES]
            outs.append(h8)
        h_ref[:, cols] = h
        o_ref[:, cols] = (jnp.concatenate(outs, axis=0) * _silu(g_ref[:, cols])).astype(o_ref.dtype)
        return carry

    lax.fori_loop(0, LRU_BLOCKS, block_body, 0, unroll=2)
    xe_ref[0:SUBLANES, :] = xe_ref[tb:tb + SUBLANES, :]


def _rglru(proj, conv_w, conv_b, wa, ba, wx, bx, lam, bsz, seq, tb=128):
    width = LRU_BLOCKS * LRU_BLOCK
    nb = seq // tb
    wg = jnp.concatenate([wa, wx], axis=-1).astype(BF16)
    bg = jnp.stack([ba, bx], axis=0)

    def col(cb):
        return pl.BlockSpec((tb, width), lambda b, s, cb=cb: (b * nb + s, cb))

    def full(shape):
        return pl.BlockSpec(shape, lambda b, s: (0,) * len(shape))

    return pl.pallas_call(
        functools.partial(_rglru_kernel, tb=tb),
        grid=(bsz, nb),
        in_specs=[col(4), col(5), full(conv_w.shape), full((1, width)), full(wg.shape),
                  full(bg.shape), full((1, width))],
        out_specs=pl.BlockSpec((tb, width), lambda b, s: (b * nb + s, 0)),
        out_shape=jax.ShapeDtypeStruct((bsz * seq, width), BF16),
        scratch_shapes=[pltpu.VMEM((tb + SUBLANES, width), F32),
                        pltpu.VMEM((1, width), F32)],
        compiler_params=_cparams(("parallel", "arbitrary"), 32),
        name="rglru",
    )(proj, proj, conv_w, conv_b.reshape(1, width), wg, bg, lam.reshape(1, width))


def _ssd_kernel(z_ref, xbc_ref, dt_ref, cw_ref, cb_ref, dtb_ref, alog_ref, dsk_ref, nw_ref, pidx_ref, o_ref,
                hist_ref, xc_ref, st_ref, st2_ref, y_ref, tt_ref, cs_ref, e_ref, cbd_ref, bt_ref, *, tb):
    n_chunks = tb // CHUNK

    @pl.when(pl.program_id(1) == 0)
    def _():
        hist_ref[...] = jnp.zeros_like(hist_ref)
        st_ref[...] = jnp.zeros_like(st_ref)

    tri_bf = _tri(CHUNK).astype(BF16)
    lane = lax.broadcasted_iota(jnp.int32, (CHUNK, LANES), 1)
    rowi = lax.broadcasted_iota(jnp.int32, (CHUNK, LANES), 0)
    lane_hi = (lane >= SSD_HEADDIM).astype(jnp.int32)
    causal2 = rowi >= (lane - SSD_HEADDIM * lane_hi)
    lane1_lo = lax.broadcasted_iota(jnp.int32, (1, LANES), 1) < SSD_HEADDIM
    lane_lo = lane < SSD_HEADDIM
    nt = (((1,), (1,)), ((), ()))
    a_neg = -jnp.exp(alog_ref[...])

    for c in range(n_chunks):
        dt = _softplus(dt_ref[c * CHUNK:(c + 1) * CHUNK, :] + dtb_ref[...])
        cs = _chunk_cumsum(tri_bf, dt * a_neg, terms=3)
        w = dt * jnp.exp(cs[CHUNK - 1:CHUNK] - cs)
        cs_ref[c] = cs
        for t, arr in enumerate((cs - jnp.log(dt), w)):
            tt_ref[c, t] = jnp.concatenate([arr, arr], axis=0).T

    slab = 512
    n_expand = n_chunks * SSD_GROUPS
    assert SSD_CONV_DIM // slab >= n_expand

    def conv_body(j, carry):
        job = lax.rem(j, n_expand)
        cj = job // SSD_GROUPS
        gj = job - cj * SSD_GROUPS
        cs = cs_ref[cj]
        for i in range(PAIRS_PER_GROUP):
            pair = gj * PAIRS_PER_GROUP + i
            idx = jnp.broadcast_to(pidx_ref[pl.ds(pair, 1), :], (CHUNK, LANES))
            e_ref[cj, :, pl.ds(pl.multiple_of(pair * LANES, LANES), LANES)] = jnp.take_along_axis(
                cs, idx, axis=1, mode="promise_in_bounds")
        cols = pl.ds(pl.multiple_of(j * slab, slab), slab)
        for r0, acc in _causal_conv_rows(hist_ref[:, cols], xbc_ref, cols, tb, cw_ref[:, cols], cb_ref[:, cols]):
            xc_ref[r0:r0 + SUBLANES, cols] = _silu(acc)
        return carry

    lax.fori_loop(0, SSD_CONV_DIM // slab, conv_body, 0)
    hist_ref[...] = xbc_ref[tb - SUBLANES:tb, :]

    for c in range(n_chunks):
        for g in range(SSD_GROUPS):
            b_g = xc_ref[c * CHUNK:(c + 1) * CHUNK, SSD_INNER + g * SSD_STATE:SSD_INNER + (g + 1) * SSD_STATE]
            c_g = xc_ref[c * CHUNK:(c + 1) * CHUNK,
                         SSD_INNER + SSD_BC + g * SSD_STATE:SSD_INNER + SSD_BC + (g + 1) * SSD_STATE]
            b2 = jnp.concatenate([b_g, b_g], axis=0)
            cbd_ref[c, g] = lax.dot_general(c_g.astype(BF16), b2.astype(BF16), nt,
                                            preferred_element_type=F32)
            bt_ref[c, g] = b2.T

    for c in range(n_chunks):
        r0 = c * CHUNK
        st_in, st_out = (st_ref, st2_ref) if c % 2 == 0 else (st2_ref, st_ref)

        def group_body(g, carry, c=c, r0=r0, st_in=st_in, st_out=st_out):
            c_g = xc_ref[r0:r0 + CHUNK,
                         pl.ds(pl.multiple_of(SSD_INNER + SSD_BC + g * SSD_STATE, SSD_STATE), SSD_STATE)]
            c_bf = c_g.astype(BF16)
            cb2 = cbd_ref[c, g]
            bt2 = bt_ref[c, g]
            for i in range(PAIRS_PER_GROUP):
                h_a = g * (2 * PAIRS_PER_GROUP) + 2 * i
                cols = pl.ds(pl.multiple_of(h_a * SSD_HEADDIM, LANES), LANES)

                def pair_row(t, h_a=h_a):
                    return jnp.where(lane1_lo, tt_ref[c, t, pl.ds(h_a, 1), :], tt_ref[c, t, pl.ds(h_a + 1, 1), :])

                e2 = e_ref[c, :, cols]
                att = cb2 * jnp.where(causal2, jnp.exp(e2 - pair_row(0)), 0.0)
                lhs = jnp.concatenate([att, bt2 * pair_row(1)], axis=0).astype(BF16)
                x2 = xc_ref[r0:r0 + CHUNK, cols]
                rhs = jnp.concatenate([jnp.where(lane_lo, x2, 0.0), jnp.where(lane_lo, 0.0, x2)],
                                      axis=0).astype(BF16)
                res = jnp.dot(lhs, rhs, preferred_element_type=F32)
                s_p = st_in[:, cols]
                e_out = jnp.exp(e2)
                y_off = e_out * jnp.dot(c_bf, s_p.astype(BF16), preferred_element_type=F32)
                y_ref[r0:r0 + CHUNK, cols] = res[0:CHUNK] + y_off + x2 * dsk_ref[:, cols]
                st_out[:, cols] = s_p * e_out[CHUNK - 1:CHUNK] + res[CHUNK:3 * CHUNK]
            return carry

        lax.fori_loop(0, SSD_GROUPS, group_body, 0)

    def norm_body(r, carry):
        rows = pl.ds(pl.multiple_of(r * ROW_SLAB, ROW_SLAB), ROW_SLAB)
        v = y_ref[rows, :] * _silu(z_ref[rows, :])
        ms = jnp.mean(v * v, axis=-1, keepdims=True)
        o_ref[rows, :] = (v * lax.rsqrt(ms + EPS) * nw_ref[...]).astype(o_ref.dtype)
        return carry

    lax.fori_loop(0, tb // ROW_SLAB, norm_body, 0)


def _ssd(z, xbc, dt, conv_w, conv_b, dt_bias, a_log, d_skip, norm_w, bsz, seq, tb=128):
    assert (tb // CHUNK) % 2 == 0, "the state ping-pong needs an even number of chunks per block"
    nb = seq // tb
    dsk = jnp.repeat(d_skip, SSD_HEADDIM).reshape(1, SSD_INNER)
    pair_idx = (2 * jnp.arange(SSD_HEADS // 2, dtype=jnp.int32)[:, None]
                + (jnp.arange(LANES, dtype=jnp.int32) // SSD_HEADDIM)[None, :])

    def rows(width):
        return pl.BlockSpec((tb, width), lambda b, s: (b * nb + s, 0))

    def full(shape):
        return pl.BlockSpec(shape, lambda b, s: (0,) * len(shape))

    return pl.pallas_call(
        functools.partial(_ssd_kernel, tb=tb),
        grid=(bsz, nb),
        in_specs=[rows(SSD_INNER), rows(SSD_CONV_DIM), rows(SSD_HEADS),
                  full(conv_w.shape), full((1, SSD_CONV_DIM)), full((1, SSD_HEADS)),
                  full((1, SSD_HEADS)), full((1, SSD_INNER)), full((1, SSD_INNER)),
                  full((SSD_HEADS // 2, LANES))],
        out_specs=rows(SSD_INNER),
        out_shape=jax.ShapeDtypeStruct((bsz * seq, SSD_INNER), BF16),
        scratch_shapes=[pltpu.VMEM((SUBLANES, SSD_CONV_DIM), F32),
                        pltpu.VMEM((tb, SSD_CONV_DIM), F32),
                        pltpu.VMEM((SSD_STATE, SSD_INNER), F32),
                        pltpu.VMEM((SSD_STATE, SSD_INNER), F32),
                        pltpu.VMEM((tb, SSD_INNER), F32),
                        pltpu.VMEM((tb // CHUNK, 2, SSD_HEADS, LANES), F32),
                        pltpu.VMEM((tb // CHUNK, CHUNK, SSD_HEADS), F32),
                        pltpu.VMEM((tb // CHUNK, CHUNK, SSD_INNER), F32),
                        pltpu.VMEM((tb // CHUNK, SSD_GROUPS, CHUNK, LANES), F32),
                        pltpu.VMEM((tb // CHUNK, SSD_GROUPS, SSD_STATE, LANES), F32)],
        compiler_params=_cparams(("parallel", "arbitrary"), 56),
        name="ssd",
    )(z, xbc, dt, conv_w, conv_b.reshape(1, SSD_CONV_DIM), dt_bias.reshape(1, SSD_HEADS),
      a_log.reshape(1, SSD_HEADS), dsk, norm_w.reshape(1, SSD_INNER), pair_idx)


def _even_layer(h, norm_w, w_in, lb_logits, a_norm_w, conv_w, conv_b, wa, ba, wx, bx, lam, w_out, bsz, seq):
    hn = _rmsnorm(h, norm_w, BF16)
    proj = _matmul([(hn, 0)], w_in.astype(BF16), tm=1024, tn=1024, vmem_mib=48, name="even_in_proj")
    o_a = _hgrn(proj, lb_logits, a_norm_w, bsz, seq)
    o_b = _rglru(proj, conv_w, conv_b, wa, ba, wx, bx, lam, bsz, seq)
    return _matmul([(o_a, 0), (o_b, 0)], w_out.astype(BF16), residual=h, tm=512, tn=1024, vmem_mib=48,
                   name="even_out_proj")


def _odd_layer(h, norm_w, w_in, conv_w, conv_b, dt_bias, a_log, d_skip, ssd_norm_w, w_out, bsz, seq):
    hn = _rmsnorm(h, norm_w, BF16)
    w_z = w_in[:, :SSD_INNER].astype(BF16)
    w_x = w_in[:, SSD_INNER:SSD_INNER + SSD_CONV_DIM].astype(BF16)
    w_dt = w_in[:, SSD_INNER + SSD_CONV_DIM:].astype(BF16)
    z = _matmul([(hn, 0)], w_z, tm=1024, tn=1024, vmem_mib=48, name="odd_in_proj_z")
    xbc = _matmul([(hn, 0)], w_x, tm=1024, tn=1024, vmem_mib=48, name="odd_in_proj_xbc")
    dt = _matmul([(hn, 0)], w_dt, tm=1024, tn=SSD_HEADS, vmem_mib=32, name="odd_in_proj_dt")
    y = _ssd(z, xbc, dt, conv_w, conv_b, dt_bias, a_log, d_skip, ssd_norm_w, bsz, seq)
    return _matmul([(y, 0), (y, 1)], w_out.astype(BF16), residual=h, tm=512, tn=1024, vmem_mib=48,
                   name="odd_out_proj")


def kernel(x, norm_w, e_w_in, lb_logits, e_a_norm_w, e_conv_w, e_conv_b, e_wa, e_ba, e_wx, e_bx, e_lambda,
           e_w_out, o_w_in, o_conv_w, o_conv_b, o_dt_bias, o_a_log, o_d, o_norm_w, o_w_out, final_norm_w):
    bsz, seq, d = x.shape
    h = x.reshape(bsz * seq, d)
    h = _even_layer(h, norm_w[0], e_w_in[0], lb_logits, e_a_norm_w[0], e_conv_w[0], e_conv_b[0], e_wa[0],
                    e_ba[0], e_wx[0], e_bx[0], e_lambda[0], e_w_out[0], bsz, seq)
    h = _odd_layer(h, norm_w[1], o_w_in[0], o_conv_w[0], o_conv_b[0], o_dt_bias[0], o_a_log[0], o_d[0],
                   o_norm_w[0], o_w_out[0], bsz, seq)
    return _rmsnorm(h, final_norm_w, x.dtype).reshape(bsz, seq, d)
```

```python
import functools

import jax
import jax.numpy as jnp
from jax import lax
from jax.experimental import pallas as pl
from jax.experimental.pallas import tpu as pltpu

F32 = jnp.float32
BF16 = jnp.bfloat16

EPS = 1e-6
CHUNK = 64
LANES = 128
SUBLANES = 8
D_MODEL = 4096
HG_HEADS = 32
HG_DK = 128
LRU_BLOCKS = 32
LRU_BLOCK = 128
LRU_C = 8.0
CONV_W = 4
SSD_INNER = 8192
SSD_HEADS = 128
SSD_HEADDIM = 64
SSD_GROUPS = 8
SSD_STATE = 128
SSD_BC = SSD_GROUPS * SSD_STATE
SSD_CONV_DIM = SSD_INNER + 2 * SSD_BC
PAIRS_PER_GROUP = SSD_HEADS // SSD_GROUPS // 2
MIB = 1024 * 1024


def _cparams(semantics, vmem_mib):
    return pltpu.CompilerParams(dimension_semantics=semantics, vmem_limit_bytes=vmem_mib * MIB)


def _sigmoid(x):
    return 0.5 + 0.5 * jnp.tanh(0.5 * x)


def _silu(x):
    return x * _sigmoid(x)


def _softplus(x):
    return jnp.maximum(x, 0.0) + jnp.log1p(jnp.exp(-jnp.abs(x)))


def _chunk_cumsum(tri, x, terms):
    acc = None
    rem = x
    for t in range(terms):
        piece = rem.astype(BF16)
        part = jnp.dot(tri, piece, preferred_element_type=F32)
        acc = part if acc is None else acc + part
        if t + 1 < terms:
            rem = rem - piece.astype(F32)
    return acc


def _causal_conv(xe, w, b):
    out = b + xe[SUBLANES:] * w[CONV_W - 1:CONV_W]
    for k in range(CONV_W - 1):
        shifted = pltpu.roll(xe, CONV_W - 1 - k, 0)
        out = out + shifted[SUBLANES:] * w[k:k + 1]
    return out


def _causal_conv_rows(hist, x_ref, cols, n_rows, w, b):
    row = lax.broadcasted_iota(jnp.int32, hist.shape, 0)
    shifts = range(1, CONV_W)
    prev_rot = [pltpu.roll(hist, s, 0) for s in shifts]
    for r0 in range(0, n_rows, SUBLANES):
        cur = x_ref[r0:r0 + SUBLANES, cols]
        cur_rot = [pltpu.roll(cur, s, 0) for s in shifts]
        out = b + cur * w[CONV_W - 1:CONV_W]
        for s, c_rot, p_rot in zip(shifts, cur_rot, prev_rot):
            out = out + jnp.where(row >= s, c_rot, p_rot) * w[CONV_W - 1 - s:CONV_W - s]
        prev_rot = cur_rot
        yield r0, out


def _tri(n):
    r = lax.broadcasted_iota(jnp.int32, (n, n), 0)
    c = lax.broadcasted_iota(jnp.int32, (n, n), 1)
    return r >= c


def _norm_kernel(x_ref, w_ref, o_ref):
    x = x_ref[...]
    ms = jnp.mean(x * x, axis=-1, keepdims=True)
    o_ref[...] = (x * lax.rsqrt(ms + EPS) * w_ref[...]).astype(o_ref.dtype)


def _rmsnorm(x, w, out_dtype, tm=256):
    m, d = x.shape
    return pl.pallas_call(
        _norm_kernel,
        grid=(m // tm,),
        in_specs=[pl.BlockSpec((tm, d), lambda i: (i, 0)),
                  pl.BlockSpec((1, d), lambda i: (0, 0))],
        out_specs=pl.BlockSpec((tm, d), lambda i: (i, 0)),
        out_shape=jax.ShapeDtypeStruct((m, d), out_dtype),
        compiler_params=_cparams(("parallel",), 32),
        name="rmsnorm",
    )(x, w.reshape(1, d))


def _matmul_kernel(*refs, n_a, has_res, epilogue):
    a_refs = refs[:n_a]
    b_ref = refs[n_a]
    r_ref = refs[n_a + 1] if has_res else None
    o_ref = refs[-1]
    if n_a == 1:
        p = jnp.dot(a_refs[0][...], b_ref[...], preferred_element_type=F32)
        if has_res:
            p = p + r_ref[...]
        o_ref[...] = p if epilogue is None else epilogue(p)
        return
    k = pl.program_id(2)
    for idx in range(n_a):
        @pl.when(k == idx)
        def _(idx=idx):
            p = jnp.dot(a_refs[idx][...], b_ref[...], preferred_element_type=F32)
            if idx == 0:
                o_ref[...] = p
            else:
                o_ref[...] += p
            if has_res and idx == n_a - 1:
                o_ref[...] += r_ref[...]


def _matmul(a_list, w, residual=None, *, tm, tn, vmem_mib, name, epilogue=None):
    tk = w.shape[0] // len(a_list)
    m = a_list[0][0].shape[0]
    n = w.shape[1]
    n_a = len(a_list)
    assert epilogue is None or n_a == 1
    in_specs = [pl.BlockSpec((tm, tk), lambda i, j, k, cb=cb: (i, cb)) for _, cb in a_list]
    in_specs.append(pl.BlockSpec((tk, tn), lambda i, j, k: (k, j)))
    args = [a for a, _ in a_list] + [w]
    if residual is not None:
        in_specs.append(pl.BlockSpec((tm, tn), lambda i, j, k: (i, j)))
        args.append(residual)
    return pl.pallas_call(
        functools.partial(_matmul_kernel, n_a=n_a, has_res=residual is not None, epilogue=epilogue),
        grid=(m // tm, n // tn, n_a),
        in_specs=in_specs,
        out_specs=pl.BlockSpec((tm, tn), lambda i, j, k: (i, j)),
        out_shape=jax.ShapeDtypeStruct((m, n), F32),
        compiler_params=_cparams(("parallel", "parallel", "arbitrary"), vmem_mib),
        name=name,
    )(*args)


HG_SLAB = 512
ROW_SLAB = 16


def _hgrn_kernel(q_ref, f_ref, i_ref, g_ref, lbl_ref, nw_ref, o_ref,
                 st_ref, acc_ref, qin_ref, kin_ref, v_ref, qout_ref, kst_ref, gl_ref, *, tb):
    @pl.when(pl.program_id(1) == 0)
    def _():
        st_ref[...] = jnp.zeros_like(st_ref)

    n_chunks = tb // CHUNK
    n_slabs = (HG_HEADS * HG_DK) // HG_SLAB
    heads_per_slab = HG_SLAB // HG_DK
    tri = _tri(CHUNK)
    tri_bf = tri.astype(BF16)
    nt = (((1,), (1,)), ((), ()))
    tn = (((0,), (0,)), ((), ()))

    def phase_a(j, p):
        cols = pl.ds(pl.multiple_of(j * HG_SLAB, HG_SLAB), HG_SLAB)
        lg = lbl_ref[:, cols]
        e = jnp.exp(lg - jnp.max(lg, axis=0, keepdims=True))
        lb = e[0:1] / jnp.sum(e, axis=0, keepdims=True)
        half_k = 0.5 * (1.0 - lb)
        for c in range(n_chunks):
            rows = pl.ds(c * CHUNK, CHUNK)
            qp = q_ref[rows, cols]
            k = half_k - half_k * jnp.tanh(0.5 * f_ref[rows, cols])
            logf = jnp.log(1.0 - k)
            b = _chunk_cumsum(tri_bf, logf, terms=2)
            ref = b[CHUNK // 2 - 1:CHUNK // 2]
            blast = b[CHUNK - 1:CHUNK]
            d = b - ref
            q_in = _silu(qp) * jnp.exp(d)
            k_in = k * jnp.exp(-d)
            qin_ref[p, rows, :] = q_in.astype(BF16)
            kin_ref[p, rows, :] = k_in.astype(BF16)
            qout_ref[p, rows, :] = (q_in * jnp.exp(ref)).astype(BF16)
            kst_ref[p, rows, :] = (k_in * jnp.exp(blast - ref)).astype(BF16)
            v_ref[p, rows, :] = i_ref[rows, cols].astype(BF16)
            gl_ref[p, c:c + 1, :] = jnp.exp(blast)

    def phase_b(j, p):
        states = [st_ref[j * heads_per_slab + hh] for hh in range(heads_per_slab)]
        for hh in range(heads_per_slab):
            lanes = slice(hh * HG_DK, (hh + 1) * HG_DK)
            out_cols = pl.ds(pl.multiple_of((j * heads_per_slab + hh) * HG_DK, HG_DK), HG_DK)
            s_t = states[hh]
            for c in range(n_chunks):
                rows = pl.ds(c * CHUNK, CHUNK)
                v = v_ref[p, rows, lanes]
                scores = lax.dot_general(qin_ref[p, rows, lanes], kin_ref[p, rows, lanes], nt,
                                         preferred_element_type=F32)
                scores = jnp.where(tri, scores, 0.0).astype(BF16)
                lhs = jnp.concatenate([qout_ref[p, rows, lanes], scores], axis=1)
                rhs = jnp.concatenate([s_t.astype(BF16).T, v], axis=0)
                acc_ref[rows, out_cols] = jnp.dot(lhs, rhs, preferred_element_type=F32)
                s_t = (s_t * gl_ref[p, c:c + 1, lanes]
                       + lax.dot_general(v, kst_ref[p, rows, lanes], tn, preferred_element_type=F32))
            states[hh] = s_t
        for hh in range(heads_per_slab):
            st_ref[j * heads_per_slab + hh] = states[hh]

    phase_a(0, 0)

    def skew_body(jj, carry):
        j = 2 * jj
        phase_a(j + 1, 1)
        phase_b(j, 0)
        phase_a(j + 2, 0)
        phase_b(j + 1, 1)
        return carry

    lax.fori_loop(0, n_slabs // 2 - 1, skew_body, 0)
    phase_a(n_slabs - 1, 1)
    phase_b(n_slabs - 2, 0)
    phase_b(n_slabs - 1, 1)

    def norm_body(r, carry):
        rows = pl.ds(pl.multiple_of(r * ROW_SLAB, ROW_SLAB), ROW_SLAB)
        o = acc_ref[rows, :]
        ms = jnp.mean(o * o, axis=-1, keepdims=True)
        o_ref[rows, :] = (o * lax.rsqrt(ms + EPS) * nw_ref[...] * _silu(g_ref[rows, :])).astype(o_ref.dtype)
        return carry

    lax.fori_loop(0, tb // ROW_SLAB, norm_body, 0)


def _hgrn(proj, lb_logits, a_norm_w, bsz, seq, tb=128):
    width = HG_HEADS * HG_DK
    nb = seq // tb

    def col(cb):
        return pl.BlockSpec((tb, width), lambda b, s, cb=cb: (b * nb + s, cb))

    return pl.pallas_call(
        functools.partial(_hgrn_kernel, tb=tb),
        grid=(bsz, nb),
        in_specs=[col(0), col(1), col(2), col(3),
                  pl.BlockSpec(lb_logits.shape, lambda b, s: (0, 0)),
                  pl.BlockSpec((1, width), lambda b, s: (0, 0))],
        out_specs=pl.BlockSpec((tb, width), lambda b, s: (b * nb + s, 0)),
        out_shape=jax.ShapeDtypeStruct((bsz * seq, width), BF16),
        scratch_shapes=[pltpu.VMEM((HG_HEADS, HG_DK, HG_DK), F32),
                        pltpu.VMEM((tb, width), F32)]
                       + [pltpu.VMEM((2, tb, HG_SLAB), BF16)] * 5
                       + [pltpu.VMEM((2, tb // CHUNK, HG_SLAB), F32)],
        compiler_params=_cparams(("parallel", "arbitrary"), 40),
        name="hgrn2",
    )(proj, proj, proj, proj, lb_logits, a_norm_w.reshape(1, width))


LRU_TILE = 512


def _lru_block(xc, gate_in, wg, ba, bx, lam, h, row):
    gates = jnp.dot(xc.astype(BF16), wg, preferred_element_type=F32)
    quarter = (0.25 * LRU_C) * _softplus(-lam)
    t_r = jnp.tanh(0.5 * (gates[:, :LRU_BLOCK] + ba))
    tau = jnp.tanh(-quarter - quarter * t_r)
    rho = 1.0 / (1.0 - tau)
    a = (1.0 + tau) * rho
    neg_tau = -tau
    root = jnp.where(neg_tau > 0.0, neg_tau * lax.rsqrt(neg_tau), 0.0)
    ig = _sigmoid(gates[:, LRU_BLOCK:] + bx)
    u = (2.0 * rho * root) * (ig * xc)
    outs = []
    for r8 in range(xc.shape[0] // SUBLANES):
        a8 = a[r8 * SUBLANES:(r8 + 1) * SUBLANES]
        u8 = u[r8 * SUBLANES:(r8 + 1) * SUBLANES]
        for sh in (1, 2, 4):
            keep = row >= sh
            a_s = jnp.where(keep, pltpu.roll(a8, sh, 0), 1.0)
            u_s = jnp.where(keep, pltpu.roll(u8, sh, 0), 0.0)
            u8 = a8 * u_s + u8
            a8 = a8 * a_s
        h8 = a8 * h + u8
        h = h8[SUBLANES - 1:SUBLANES]
        outs.append(h8)
    return jnp.concatenate(outs, axis=0) * _silu(gate_in), h


def _lru_proj_kernel(a_ref, w_ref, cw_ref, cb_ref, wg_ref, bg_ref, lam_ref, o_ref, p_ref, hist_ref, h_ref,
                     *, tm, tiles_per_seq):
    i = pl.program_id(0)
    j = pl.program_id(1)

    @pl.when(lax.rem(i, tiles_per_seq) == 0)
    def _():
        hist_ref[j] = jnp.zeros(hist_ref.shape[1:], F32)
        h_ref[j] = jnp.zeros(h_ref.shape[1:], F32)

    p_ref[...] = jnp.dot(a_ref[...], w_ref[...], preferred_element_type=F32)
    row = lax.broadcasted_iota(jnp.int32, (SUBLANES, LRU_BLOCK), 0)
    hist = hist_ref[j]
    h_in = h_ref[j]
    h_out = []
    for blk in range(LRU_TILE // LRU_BLOCK):
        lanes = slice(blk * LRU_BLOCK, (blk + 1) * LRU_BLOCK)
        n = j * (LRU_TILE // LRU_BLOCK) + blk
        cols = pl.ds(pl.multiple_of(n * LRU_BLOCK, LRU_BLOCK), LRU_BLOCK)
        xc = jnp.concatenate(
            [acc for _, acc in _causal_conv_rows(hist[:, lanes], p_ref, lanes, tm, cw_ref[:, cols], cb_ref[:, cols])],
            axis=0)
        out, h_last = _lru_block(xc, p_ref[:, LRU_TILE + blk * LRU_BLOCK:LRU_TILE + (blk + 1) * LRU_BLOCK],
                                 wg_ref[n], bg_ref[0:1, cols], bg_ref[1:2, cols], lam_ref[:, cols],
                                 h_in[:, lanes], row)
        o_ref[:, lanes] = out.astype(o_ref.dtype)
        h_out.append(h_last)
    hist_ref[j] = p_ref[tm - SUBLANES:tm, 0:LRU_TILE]
    h_ref[j] = jnp.concatenate(h_out, axis=1)


def _lru_proj(hn, w_xb, w_gb, conv_w, conv_b, wa, ba, wx, bx, lam, seq, tm=1024):
    m, k = hn.shape
    width = LRU_BLOCKS * LRU_BLOCK
    n_tiles = width // LRU_TILE
    w = jnp.concatenate([w_xb.reshape(k, n_tiles, LRU_TILE), w_gb.reshape(k, n_tiles, LRU_TILE)],
                        axis=2).reshape(k, 2 * width).astype(BF16)
    wg = jnp.concatenate([wa, wx], axis=-1).astype(BF16)
    bg = jnp.stack([ba, bx], axis=0)

    def full(shape):
        return pl.BlockSpec(shape, lambda i, j: (0,) * len(shape))

    return pl.pallas_call(
        functools.partial(_lru_proj_kernel, tm=tm, tiles_per_seq=seq // tm),
        grid=(m // tm, n_tiles),
        in_specs=[pl.BlockSpec((tm, k), lambda i, j: (i, 0)),
                  pl.BlockSpec((k, 2 * LRU_TILE), lambda i, j: (0, j)),
                  full(conv_w.shape), full((1, width)), full(wg.shape), full(bg.shape), full((1, width))],
        out_specs=pl.BlockSpec((tm, LRU_TILE), lambda i, j: (i, j)),
        out_shape=jax.ShapeDtypeStruct((m, width), BF16),
        scratch_shapes=[pltpu.VMEM((tm, 2 * LRU_TILE), F32),
                        pltpu.VMEM((n_tiles, SUBLANES, LRU_TILE), F32),
                        pltpu.VMEM((n_tiles, 1, LRU_TILE), F32)],
        compiler_params=_cparams(("arbitrary", "arbitrary"), 48),
        name="even_in_proj_rglru",
    )(hn, w, conv_w, conv_b.reshape(1, width), wg, bg, lam.reshape(1, width))


CONV_SLAB = 256


def _conv_proj_kernel(a_ref, w_ref, cw_ref, cb_ref, o_ref, p_ref, hist_ref, *, tm, tn, tiles_per_seq):
    i = pl.program_id(0)
    j = pl.program_id(1)

    @pl.when(lax.rem(i, tiles_per_seq) == 0)
    def _():
        hist_ref[j] = jnp.zeros(hist_ref.shape[1:], F32)

    p_ref[...] = jnp.dot(a_ref[...], w_ref[...], preferred_element_type=F32)
    hist = hist_ref[j]
    for blk in range(tn // CONV_SLAB):
        lanes = slice(blk * CONV_SLAB, (blk + 1) * CONV_SLAB)
        cols = pl.ds(pl.multiple_of(j * tn + blk * CONV_SLAB, CONV_SLAB), CONV_SLAB)
        for r0, acc in _causal_conv_rows(hist[:, lanes], p_ref, lanes, tm, cw_ref[:, cols], cb_ref[:, cols]):
            o_ref[r0:r0 + SUBLANES, lanes] = _silu(acc)
    hist_ref[j] = p_ref[tm - SUBLANES:tm, :]


def _conv_proj(hn, w, conv_w, conv_b, seq, tm=1024, tn=1024):
    m, k = hn.shape
    n = w.shape[1]
    return pl.pallas_call(
        functools.partial(_conv_proj_kernel, tm=tm, tn=tn, tiles_per_seq=seq // tm),
        grid=(m // tm, n // tn),
        in_specs=[pl.BlockSpec((tm, k), lambda i, j: (i, 0)),
                  pl.BlockSpec((k, tn), lambda i, j: (0, j)),
                  pl.BlockSpec(conv_w.shape, lambda i, j: (0, 0)),
                  pl.BlockSpec((1, n), lambda i, j: (0, 0))],
        out_specs=pl.BlockSpec((tm, tn), lambda i, j: (i, j)),
        out_shape=jax.ShapeDtypeStruct((m, n), F32),
        scratch_shapes=[pltpu.VMEM((tm, tn), F32),
                        pltpu.VMEM((n // tn, SUBLANES, tn), F32)],
        compiler_params=_cparams(("arbitrary", "arbitrary"), 56),
        name="odd_in_proj_xbc_conv",
    )(hn, w, conv_w, conv_b.reshape(1, n))


def _ssd_kernel(sz_ref, xc_ref, dt_ref, dtb_ref, alog_ref, dsk_ref, nw_ref, pidx_ref, o_ref,
                st_ref, st2_ref, y_ref, tt_ref, cs_ref, cbd_ref, bt_ref, *e_refs, tb):
    n_chunks = tb // CHUNK

    @pl.when(pl.program_id(1) == 0)
    def _():
        st_ref[...] = jnp.zeros_like(st_ref)

    tri_bf = _tri(CHUNK).astype(BF16)
    lane = lax.broadcasted_iota(jnp.int32, (CHUNK, LANES), 1)
    rowi = lax.broadcasted_iota(jnp.int32, (CHUNK, LANES), 0)
    lane_hi = (lane >= SSD_HEADDIM).astype(jnp.int32)
    causal2 = rowi >= (lane - SSD_HEADDIM * lane_hi)
    lane1_lo = lax.broadcasted_iota(jnp.int32, (1, LANES), 1) < SSD_HEADDIM
    lane_lo = lane < SSD_HEADDIM
    nt = (((1,), (1,)), ((), ()))
    a_neg = -jnp.exp(alog_ref[...])

    for c in range(n_chunks):
        dt = _softplus(dt_ref[c * CHUNK:(c + 1) * CHUNK, :] + dtb_ref[...])
        cs = _chunk_cumsum(tri_bf, dt * a_neg, terms=3)
        w = dt * jnp.exp(cs[CHUNK - 1:CHUNK] - cs)
        cs_ref[c] = cs
        for t, arr in enumerate((cs - jnp.log(dt), w)):
            tt_ref[c, t] = jnp.concatenate([arr, arr], axis=0).T

    def expand_group(c, g):
        cs = cs_ref[c]
        for i in range(PAIRS_PER_GROUP):
            pair = g * PAIRS_PER_GROUP + i
            idx = jnp.broadcast_to(pidx_ref[pl.ds(pair, 1), :], (CHUNK, LANES))
            e_refs[c][:, pl.ds(pl.multiple_of(pair * LANES, LANES), LANES)] = jnp.take_along_axis(
                cs, idx, axis=1, mode="promise_in_bounds")

    def expand_body(g, carry):
        expand_group(0, g)
        return carry

    lax.fori_loop(0, SSD_GROUPS, expand_body, 0)

    for c in range(n_chunks):
        for g in range(SSD_GROUPS):
            b_g = xc_ref[c * CHUNK:(c + 1) * CHUNK, SSD_INNER + g * SSD_STATE:SSD_INNER + (g + 1) * SSD_STATE]
            c_g = xc_ref[c * CHUNK:(c + 1) * CHUNK,
                         SSD_INNER + SSD_BC + g * SSD_STATE:SSD_INNER + SSD_BC + (g + 1) * SSD_STATE]
            b2 = jnp.concatenate([b_g, b_g], axis=0)
            cbd_ref[c, g] = lax.dot_general(c_g.astype(BF16), b2.astype(BF16), nt,
                                            preferred_element_type=F32)
            bt_ref[c, g] = b2.T

    for c in range(n_chunks):
        r0 = c * CHUNK
        st_in, st_out = (st_ref, st2_ref) if c % 2 == 0 else (st2_ref, st_ref)

        def group_body(g, carry, c=c, r0=r0, st_in=st_in, st_out=st_out):
            c_g = xc_ref[r0:r0 + CHUNK,
                         pl.ds(pl.multiple_of(SSD_INNER + SSD_BC + g * SSD_STATE, SSD_STATE), SSD_STATE)]
            c_bf = c_g.astype(BF16)
            cb2 = cbd_ref[c, g]
            bt2 = bt_ref[c, g]
            for i in range(PAIRS_PER_GROUP):
                h_a = g * (2 * PAIRS_PER_GROUP) + 2 * i
                cols = pl.ds(pl.multiple_of(h_a * SSD_HEADDIM, LANES), LANES)

                def pair_row(t, h_a=h_a):
                    return jnp.where(lane1_lo, tt_ref[c, t, pl.ds(h_a, 1), :], tt_ref[c, t, pl.ds(h_a + 1, 1), :])

                e2 = e_refs[c][:, cols]
                att = cb2 * jnp.where(causal2, jnp.exp(e2 - pair_row(0)), 0.0)
                lhs = jnp.concatenate([att, bt2 * pair_row(1)], axis=0).astype(BF16)
                x2 = xc_ref[r0:r0 + CHUNK, cols]
                rhs = jnp.concatenate([jnp.where(lane_lo, x2, 0.0), jnp.where(lane_lo, 0.0, x2)],
                                      axis=0).astype(BF16)
                res = jnp.dot(lhs, rhs, preferred_element_type=F32)
                s_p = st_in[:, cols]
                e_out = jnp.exp(e2)
                y_off = e_out * jnp.dot(c_bf, s_p.astype(BF16), preferred_element_type=F32)
                y_ref[r0:r0 + CHUNK, cols] = res[0:CHUNK] + y_off + x2 * dsk_ref[:, cols]
                st_out[:, cols] = s_p * e_out[CHUNK - 1:CHUNK] + res[CHUNK:3 * CHUNK]
            if c + 1 < n_chunks:
                expand_group(c + 1, g)
            return carry

        lax.fori_loop(0, SSD_GROUPS, group_body, 0)

    def norm_body(r, carry):
        rows = pl.ds(pl.multiple_of(r * ROW_SLAB, ROW_SLAB), ROW_SLAB)
        v = y_ref[rows, :] * sz_ref[rows, :]
        ms = jnp.mean(v * v, axis=-1, keepdims=True)
        o_ref[rows, :] = (v * lax.rsqrt(ms + EPS) * nw_ref[...]).astype(o_ref.dtype)
        return carry

    lax.fori_loop(0, tb // ROW_SLAB, norm_body, 0)


def _ssd(sz, xc, dt, dt_bias, a_log, d_skip, norm_w, bsz, seq, tb=128):
    assert (tb // CHUNK) % 2 == 0, "the state ping-pong needs an even number of chunks per block"
    nb = seq // tb
    dsk = jnp.repeat(d_skip, SSD_HEADDIM).reshape(1, SSD_INNER)
    pair_idx = (2 * jnp.arange(SSD_HEADS // 2, dtype=jnp.int32)[:, None]
                + (jnp.arange(LANES, dtype=jnp.int32) // SSD_HEADDIM)[None, :])

    def rows(width):
        return pl.BlockSpec((tb, width), lambda b, s: (b * nb + s, 0))

    def full(shape):
        return pl.BlockSpec(shape, lambda b, s: (0,) * len(shape))

    return pl.pallas_call(
        functools.partial(_ssd_kernel, tb=tb),
        grid=(bsz, nb),
        in_specs=[rows(SSD_INNER), rows(SSD_CONV_DIM), rows(SSD_HEADS),
                  full((1, SSD_HEADS)), full((1, SSD_HEADS)), full((1, SSD_INNER)), full((1, SSD_INNER)),
                  full((SSD_HEADS // 2, LANES))],
        out_specs=rows(SSD_INNER),
        out_shape=jax.ShapeDtypeStruct((bsz * seq, SSD_INNER), BF16),
        scratch_shapes=[pltpu.VMEM((SSD_STATE, SSD_INNER), F32),
                        pltpu.VMEM((SSD_STATE, SSD_INNER), F32),
                        pltpu.VMEM((tb, SSD_INNER), F32),
                        pltpu.VMEM((tb // CHUNK, 2, SSD_HEADS, LANES), F32),
                        pltpu.VMEM((tb // CHUNK, CHUNK, SSD_HEADS), F32),
                        pltpu.VMEM((tb // CHUNK, SSD_GROUPS, CHUNK, LANES), F32),
                        pltpu.VMEM((tb // CHUNK, SSD_GROUPS, SSD_STATE, LANES), F32)]
                       + [pltpu.VMEM((CHUNK, SSD_INNER), F32)] * (tb // CHUNK),
        compiler_params=_cparams(("parallel", "arbitrary"), 48),
        name="ssd",
    )(sz, xc, dt, dt_bias.reshape(1, SSD_HEADS), a_log.reshape(1, SSD_HEADS), dsk,
      norm_w.reshape(1, SSD_INNER), pair_idx)


def _even_layer(h, norm_w, w_in, lb_logits, a_norm_w, conv_w, conv_b, wa, ba, wx, bx, lam, w_out, bsz, seq):
    hn = _rmsnorm(h, norm_w, BF16)
    hg = 4 * HG_HEADS * HG_DK
    lru = LRU_BLOCKS * LRU_BLOCK
    proj = _matmul([(hn, 0)], w_in[:, :hg].astype(BF16), tm=1024, tn=1024, vmem_mib=48, name="even_in_proj")
    o_a = _hgrn(proj, lb_logits, a_norm_w, bsz, seq)
    o_b = _lru_proj(hn, w_in[:, hg:hg + lru], w_in[:, hg + lru:], conv_w, conv_b, wa, ba, wx, bx, lam, seq)
    return _matmul([(o_a, 0), (o_b, 0)], w_out.astype(BF16), residual=h, tm=512, tn=1024, vmem_mib=48,
                   name="even_out_proj")


def _odd_layer(h, norm_w, w_in, conv_w, conv_b, dt_bias, a_log, d_skip, ssd_norm_w, w_out, bsz, seq):
    hn = _rmsnorm(h, norm_w, BF16)
    w_z = w_in[:, :SSD_INNER].astype(BF16)
    w_x = w_in[:, SSD_INNER:SSD_INNER + SSD_CONV_DIM].astype(BF16)
    w_dt = w_in[:, SSD_INNER + SSD_CONV_DIM:].astype(BF16)
    sz = _matmul([(hn, 0)], w_z, tm=1024, tn=1024, vmem_mib=56, name="odd_in_proj_z", epilogue=_silu)
    xc = _conv_proj(hn, w_x, conv_w, conv_b, seq)
    dt = _matmul([(hn, 0)], w_dt, tm=1024, tn=SSD_HEADS, vmem_mib=32, name="odd_in_proj_dt")
    y = _ssd(sz, xc, dt, dt_bias, a_log, d_skip, ssd_norm_w, bsz, seq)
    return _matmul([(y, 0), (y, 1)], w_out.astype(BF16), residual=h, tm=512, tn=1024, vmem_mib=48,
                   name="odd_out_proj")


def kernel(x, norm_w, e_w_in, lb_logits, e_a_norm_w, e_conv_w, e_conv_b, e_wa, e_ba, e_wx, e_bx, e_lambda,
           e_w_out, o_w_in, o_conv_w, o_conv_b, o_dt_bias, o_a_log, o_d, o_norm_w, o_w_out, final_norm_w):
    bsz, seq, d = x.shape
    h = x.reshape(bsz * seq, d)
    h = _even_layer(h, norm_w[0], e_w_in[0], lb_logits, e_a_norm_w[0], e_conv_w[0], e_conv_b[0], e_wa[0],
                    e_ba[0], e_wx[0], e_bx[0], e_lambda[0], e_w_out[0], bsz, seq)
    h = _odd_layer(h, norm_w[1], o_w_in[0], o_conv_w[0], o_conv_b[0], o_dt_bias[0], o_a_log[0], o_d[0],
                   o_norm_w[0], o_w_out[0], bsz, seq)
    return _rmsnorm(h, final_norm_w, x.dtype).reshape(bsz, seq, d)
```

```python
import functools

import jax
import jax.numpy as jnp
from jax import lax
from jax.experimental import pallas as pl
from jax.experimental.pallas import tpu as pltpu

F32 = jnp.float32
BF16 = jnp.bfloat16

EPS = 1e-6
CHUNK = 64
LANES = 128
SUBLANES = 8
D_MODEL = 4096
HG_HEADS = 32
HG_DK = 128
LRU_BLOCKS = 32
LRU_BLOCK = 128
LRU_C = 8.0
CONV_W = 4
SSD_INNER = 8192
SSD_HEADS = 128
SSD_HEADDIM = 64
SSD_GROUPS = 8
SSD_STATE = 128
SSD_BC = SSD_GROUPS * SSD_STATE
SSD_CONV_DIM = SSD_INNER + 2 * SSD_BC
PAIRS_PER_GROUP = SSD_HEADS // SSD_GROUPS // 2
MIB = 1024 * 1024


def _cparams(semantics, vmem_mib):
    return pltpu.CompilerParams(dimension_semantics=semantics, vmem_limit_bytes=vmem_mib * MIB)


def _sigmoid(x):
    return 0.5 + 0.5 * jnp.tanh(0.5 * x)


def _silu(x):
    return x * _sigmoid(x)


def _softplus(x):
    return jnp.maximum(x, 0.0) + jnp.log1p(jnp.exp(-jnp.abs(x)))


def _chunk_cumsum(tri, x, terms):
    acc = None
    rem = x
    for t in range(terms):
        piece = rem.astype(BF16)
        part = jnp.dot(tri, piece, preferred_element_type=F32)
        acc = part if acc is None else acc + part
        if t + 1 < terms:
            rem = rem - piece.astype(F32)
    return acc


def _causal_conv(xe, w, b):
    out = b + xe[SUBLANES:] * w[CONV_W - 1:CONV_W]
    for k in range(CONV_W - 1):
        shifted = pltpu.roll(xe, CONV_W - 1 - k, 0)
        out = out + shifted[SUBLANES:] * w[k:k + 1]
    return out


def _causal_conv_rows(hist, x_ref, cols, n_rows, w, b):
    row = lax.broadcasted_iota(jnp.int32, hist.shape, 0)
    shifts = range(1, CONV_W)
    prev_rot = [pltpu.roll(hist, s, 0) for s in shifts]
    for r0 in range(0, n_rows, SUBLANES):
        cur = x_ref[r0:r0 + SUBLANES, cols]
        cur_rot = [pltpu.roll(cur, s, 0) for s in shifts]
        out = b + cur * w[CONV_W - 1:CONV_W]
        for s, c_rot, p_rot in zip(shifts, cur_rot, prev_rot):
            out = out + jnp.where(row >= s, c_rot, p_rot) * w[CONV_W - 1 - s:CONV_W - s]
        prev_rot = cur_rot
        yield r0, out


def _tri(n):
    r = lax.broadcasted_iota(jnp.int32, (n, n), 0)
    c = lax.broadcasted_iota(jnp.int32, (n, n), 1)
    return r >= c


def _norm_kernel(x_ref, w_ref, o_ref):
    x = x_ref[...]
    ms = jnp.mean(x * x, axis=-1, keepdims=True)
    o_ref[...] = (x * lax.rsqrt(ms + EPS) * w_ref[...]).astype(o_ref.dtype)


def _rmsnorm(x, w, out_dtype, tm=256):
    m, d = x.shape
    return pl.pallas_call(
        _norm_kernel,
        grid=(m // tm,),
        in_specs=[pl.BlockSpec((tm, d), lambda i: (i, 0)),
                  pl.BlockSpec((1, d), lambda i: (0, 0))],
        out_specs=pl.BlockSpec((tm, d), lambda i: (i, 0)),
        out_shape=jax.ShapeDtypeStruct((m, d), out_dtype),
        compiler_params=_cparams(("parallel",), 32),
        name="rmsnorm",
    )(x, w.reshape(1, d))


def _matmul_kernel(*refs, k_ranges, has_res, epilogue):
    n_a = len(k_ranges)
    a_refs = refs[:n_a]
    b_ref = refs[n_a]
    r_ref = refs[n_a + 1] if has_res else None
    o_ref = refs[-1]
    nk = k_ranges[-1][1]
    if nk == 1:
        p = jnp.dot(a_refs[0][...], b_ref[...], preferred_element_type=F32)
        if has_res:
            p = p + r_ref[...]
        o_ref[...] = p if epilogue is None else epilogue(p)
        return
    k = pl.program_id(2)

    @pl.when(k == 0)
    def _():
        o_ref[...] = r_ref[...] if has_res else jnp.zeros(o_ref.shape, F32)

    for a_ref, (k0, k1) in zip(a_refs, k_ranges):
        @pl.when((k >= k0) & (k < k1))
        def _(a_ref=a_ref):
            o_ref[...] += jnp.dot(a_ref[...], b_ref[...], preferred_element_type=F32)


def _matmul(a_list, w, residual=None, *, n, w_col0=0, tm, tn, tk, vmem_mib, name, epilogue=None):
    m = a_list[0].shape[0]
    k_ranges, k0 = [], 0
    for a in a_list:
        k_ranges.append((k0, k0 + a.shape[1] // tk))
        k0 = k_ranges[-1][1]
    nk = k0
    assert nk * tk == w.shape[0] and w_col0 % tn == 0 and (epilogue is None or nk == 1)
    in_specs = [pl.BlockSpec((tm, tk), lambda i, j, k, k0=k0, k1=k1: (i, jnp.clip(k - k0, 0, k1 - k0 - 1)))
                for k0, k1 in k_ranges]
    in_specs.append(pl.BlockSpec((tk, tn), lambda i, j, k: (k, w_col0 // tn + j)))
    args = list(a_list) + [w]
    if residual is not None:
        in_specs.append(pl.BlockSpec((tm, tn), lambda i, j, k: (i, j)))
        args.append(residual)
    return pl.pallas_call(
        functools.partial(_matmul_kernel, k_ranges=tuple(k_ranges), has_res=residual is not None,
                          epilogue=epilogue),
        grid=(m // tm, n // tn, nk),
        in_specs=in_specs,
        out_specs=pl.BlockSpec((tm, tn), lambda i, j, k: (i, j)),
        out_shape=jax.ShapeDtypeStruct((m, n), F32),
        compiler_params=_cparams(("parallel", "parallel", "arbitrary"), vmem_mib),
        name=name,
    )(*args)


HG_SLAB = 512
ROW_SLAB = 16


def _hgrn_kernel(q_ref, f_ref, i_ref, g_ref, lbl_ref, nw_ref, o_ref,
                 st_ref, acc_ref, qin_ref, kin_ref, v_ref, qout_ref, kst_ref, gl_ref, *, tb):
    @pl.when(pl.program_id(1) == 0)
    def _():
        st_ref[...] = jnp.zeros_like(st_ref)

    n_chunks = tb // CHUNK
    n_slabs = (HG_HEADS * HG_DK) // HG_SLAB
    heads_per_slab = HG_SLAB // HG_DK
    tri = _tri(CHUNK)
    tri_bf = tri.astype(BF16)
    nt = (((1,), (1,)), ((), ()))
    tn = (((0,), (0,)), ((), ()))

    def phase_a(j, p):
        cols = pl.ds(pl.multiple_of(j * HG_SLAB, HG_SLAB), HG_SLAB)
        lg = lbl_ref[:, cols]
        e = jnp.exp(lg - jnp.max(lg, axis=0, keepdims=True))
        lb = e[0:1] / jnp.sum(e, axis=0, keepdims=True)
        half_k = 0.5 * (1.0 - lb)
        for c in range(n_chunks):
            rows = pl.ds(c * CHUNK, CHUNK)
            qp = q_ref[rows, cols]
            k = half_k - half_k * jnp.tanh(0.5 * f_ref[rows, cols])
            logf = jnp.log(1.0 - k)
            b = _chunk_cumsum(tri_bf, logf, terms=2)
            ref = b[CHUNK // 2 - 1:CHUNK // 2]
            blast = b[CHUNK - 1:CHUNK]
            d = b - ref
            q_in = _silu(qp) * jnp.exp(d)
            k_in = k * jnp.exp(-d)
            qin_ref[p, rows, :] = q_in.astype(BF16)
            kin_ref[p, rows, :] = k_in.astype(BF16)
            qout_ref[p, rows, :] = (q_in * jnp.exp(ref)).astype(BF16)
            kst_ref[p, rows, :] = (k_in * jnp.exp(blast - ref)).astype(BF16)
            v_ref[p, rows, :] = i_ref[rows, cols].astype(BF16)
            gl_ref[p, c:c + 1, :] = jnp.exp(blast)

    def phase_b(j, p):
        states = [st_ref[j * heads_per_slab + hh] for hh in range(heads_per_slab)]
        for hh in range(heads_per_slab):
            lanes = slice(hh * HG_DK, (hh + 1) * HG_DK)
            out_cols = pl.ds(pl.multiple_of((j * heads_per_slab + hh) * HG_DK, HG_DK), HG_DK)
            s_t = states[hh]
            for c in range(n_chunks):
                rows = pl.ds(c * CHUNK, CHUNK)
                v = v_ref[p, rows, lanes]
                scores = lax.dot_general(qin_ref[p, rows, lanes], kin_ref[p, rows, lanes], nt,
                                         preferred_element_type=F32)
                scores = jnp.where(tri, scores, 0.0).astype(BF16)
                lhs = jnp.concatenate([qout_ref[p, rows, lanes], scores], axis=1)
                rhs = jnp.concatenate([s_t.astype(BF16).T, v], axis=0)
                acc_ref[rows, out_cols] = jnp.dot(lhs, rhs, preferred_element_type=F32)
                s_t = (s_t * gl_ref[p, c:c + 1, lanes]
                       + lax.dot_general(v, kst_ref[p, rows, lanes], tn, preferred_element_type=F32))
            states[hh] = s_t
        for hh in range(heads_per_slab):
            st_ref[j * heads_per_slab + hh] = states[hh]

    phase_a(0, 0)

    def skew_body(jj, carry):
        j = 2 * jj
        phase_a(j + 1, 1)
        phase_b(j, 0)
        phase_a(j + 2, 0)
        phase_b(j + 1, 1)
        return carry

    lax.fori_loop(0, n_slabs // 2 - 1, skew_body, 0)
    phase_a(n_slabs - 1, 1)
    phase_b(n_slabs - 2, 0)
    phase_b(n_slabs - 1, 1)

    def norm_body(r, carry):
        rows = pl.ds(pl.multiple_of(r * ROW_SLAB, ROW_SLAB), ROW_SLAB)
        o = acc_ref[rows, :]
        ms = jnp.mean(o * o, axis=-1, keepdims=True)
        o_ref[rows, :] = (o * lax.rsqrt(ms + EPS) * nw_ref[...] * _silu(g_ref[rows, :])).astype(o_ref.dtype)
        return carry

    lax.fori_loop(0, tb // ROW_SLAB, norm_body, 0)


def _hgrn(proj, lb_logits, a_norm_w, bsz, seq, tb=128):
    width = HG_HEADS * HG_DK
    nb = seq // tb

    def col(cb):
        return pl.BlockSpec((tb, width), lambda b, s, cb=cb: (b * nb + s, cb))

    return pl.pallas_call(
        functools.partial(_hgrn_kernel, tb=tb),
        grid=(bsz, nb),
        in_specs=[col(0), col(1), col(2), col(3),
                  pl.BlockSpec(lb_logits.shape, lambda b, s: (0, 0)),
                  pl.BlockSpec((1, width), lambda b, s: (0, 0))],
        out_specs=pl.BlockSpec((tb, width), lambda b, s: (b * nb + s, 0)),
        out_shape=jax.ShapeDtypeStruct((bsz * seq, width), BF16),
        scratch_shapes=[pltpu.VMEM((HG_HEADS, HG_DK, HG_DK), F32),
                        pltpu.VMEM((tb, width), F32)]
                       + [pltpu.VMEM((2, tb, HG_SLAB), BF16)] * 5
                       + [pltpu.VMEM((2, tb // CHUNK, HG_SLAB), F32)],
        compiler_params=_cparams(("parallel", "arbitrary"), 40),
        name="hgrn2",
    )(proj, proj, proj, proj, lb_logits, a_norm_w.reshape(1, width))


LRU_TILE = 512


def _lru_block(xc, gate_in, wg, ba, bx, lam, h, row):
    gates = jnp.dot(xc.astype(BF16), wg, preferred_element_type=F32)
    quarter = (0.25 * LRU_C) * _softplus(-lam)
    t_r = jnp.tanh(0.5 * (gates[:, :LRU_BLOCK] + ba))
    tau = jnp.tanh(-quarter - quarter * t_r)
    rho = 1.0 / (1.0 - tau)
    a = (1.0 + tau) * rho
    neg_tau = -tau
    root = jnp.where(neg_tau > 0.0, neg_tau * lax.rsqrt(neg_tau), 0.0)
    ig = _sigmoid(gates[:, LRU_BLOCK:] + bx)
    u = (2.0 * rho * root) * (ig * xc)
    outs = []
    for r8 in range(xc.shape[0] // SUBLANES):
        a8 = a[r8 * SUBLANES:(r8 + 1) * SUBLANES]
        u8 = u[r8 * SUBLANES:(r8 + 1) * SUBLANES]
        for sh in (1, 2, 4):
            keep = row >= sh
            a_s = jnp.where(keep, pltpu.roll(a8, sh, 0), 1.0)
            u_s = jnp.where(keep, pltpu.roll(u8, sh, 0), 0.0)
            u8 = a8 * u_s + u8
            a8 = a8 * a_s
        h8 = a8 * h + u8
        h = h8[SUBLANES - 1:SUBLANES]
        outs.append(h8)
    return jnp.concatenate(outs, axis=0) * _silu(gate_in), h


def _lru_proj_kernel(a_ref, wxb_ref, wgb_ref, cw_ref, cb_ref, wg_ref, bg_ref, lam_ref, o_ref,
                     p_ref, hist_ref, h_ref, *, tm, tiles_per_seq):
    i = pl.program_id(0)
    j = pl.program_id(1)

    @pl.when(lax.rem(i, tiles_per_seq) == 0)
    def _():
        hist_ref[j] = jnp.zeros(hist_ref.shape[1:], F32)
        h_ref[j] = jnp.zeros(h_ref.shape[1:], F32)

    a = a_ref[...]
    p_ref[:, 0:LRU_TILE] = jnp.dot(a, wxb_ref[...], preferred_element_type=F32)
    p_ref[:, LRU_TILE:2 * LRU_TILE] = jnp.dot(a, wgb_ref[...], preferred_element_type=F32)
    row = lax.broadcasted_iota(jnp.int32, (SUBLANES, LRU_BLOCK), 0)
    hist = hist_ref[j]
    h_in = h_ref[j]
    h_out = []
    for blk in range(LRU_TILE // LRU_BLOCK):
        lanes = slice(blk * LRU_BLOCK, (blk + 1) * LRU_BLOCK)
        n = j * (LRU_TILE // LRU_BLOCK) + blk
        cols = pl.ds(pl.multiple_of(n * LRU_BLOCK, LRU_BLOCK), LRU_BLOCK)
        xc = jnp.concatenate(
            [acc for _, acc in _causal_conv_rows(hist[:, lanes], p_ref, lanes, tm, cw_ref[:, cols], cb_ref[:, cols])],
            axis=0)
        out, h_last = _lru_block(xc, p_ref[:, LRU_TILE + blk * LRU_BLOCK:LRU_TILE + (blk + 1) * LRU_BLOCK],
                                 wg_ref[n], bg_ref[0:1, cols], bg_ref[1:2, cols], lam_ref[:, cols],
                                 h_in[:, lanes], row)
        o_ref[:, lanes] = out.astype(o_ref.dtype)
        h_out.append(h_last)
    hist_ref[j] = p_ref[tm - SUBLANES:tm, 0:LRU_TILE]
    h_ref[j] = jnp.concatenate(h_out, axis=1)


def _lru_proj(hn, w, xb_col0, gb_col0, conv_w, conv_b, wa, ba, wx, bx, lam, seq, tm=1024):
    m, k = hn.shape
    width = LRU_BLOCKS * LRU_BLOCK
    n_tiles = width // LRU_TILE
    assert xb_col0 % LRU_TILE == 0 and gb_col0 % LRU_TILE == 0
    wg = jnp.concatenate([wa, wx], axis=-1).astype(BF16)
    bg = jnp.stack([ba, bx], axis=0)

    def full(shape):
        return pl.BlockSpec(shape, lambda i, j: (0,) * len(shape))

    return pl.pallas_call(
        functools.partial(_lru_proj_kernel, tm=tm, tiles_per_seq=seq // tm),
        grid=(m // tm, n_tiles),
        in_specs=[pl.BlockSpec((tm, k), lambda i, j: (i, 0)),
                  pl.BlockSpec((k, LRU_TILE), lambda i, j: (0, xb_col0 // LRU_TILE + j)),
                  pl.BlockSpec((k, LRU_TILE), lambda i, j: (0, gb_col0 // LRU_TILE + j)),
                  full(conv_w.shape), full((1, width)), full(wg.shape), full(bg.shape), full((1, width))],
        out_specs=pl.BlockSpec((tm, LRU_TILE), lambda i, j: (i, j)),
        out_shape=jax.ShapeDtypeStruct((m, width), BF16),
        scratch_shapes=[pltpu.VMEM((tm, 2 * LRU_TILE), F32),
                        pltpu.VMEM((n_tiles, SUBLANES, LRU_TILE), F32),
                        pltpu.VMEM((n_tiles, 1, LRU_TILE), F32)],
        compiler_params=_cparams(("arbitrary", "arbitrary"), 48),
        name="even_in_proj_rglru",
    )(hn, w, w, conv_w, conv_b.reshape(1, width), wg, bg, lam.reshape(1, width))


CONV_SLAB = 256


def _conv_proj_kernel(a_ref, w_ref, cw_ref, cb_ref, o_ref, p_ref, hist_ref, *, tm, tn, tiles_per_seq):
    i = pl.program_id(0)
    j = pl.program_id(1)

    @pl.when(lax.rem(i, tiles_per_seq) == 0)
    def _():
        hist_ref[j] = jnp.zeros(hist_ref.shape[1:], F32)

    p_ref[...] = jnp.dot(a_ref[...], w_ref[...], preferred_element_type=F32)
    hist = hist_ref[j]
    for blk in range(tn // CONV_SLAB):
        lanes = slice(blk * CONV_SLAB, (blk + 1) * CONV_SLAB)
        cols = pl.ds(pl.multiple_of(j * tn + blk * CONV_SLAB, CONV_SLAB), CONV_SLAB)
        for r0, acc in _causal_conv_rows(hist[:, lanes], p_ref, lanes, tm, cw_ref[:, cols], cb_ref[:, cols]):
            o_ref[r0:r0 + SUBLANES, lanes] = _silu(acc)
    hist_ref[j] = p_ref[tm - SUBLANES:tm, :]


def _conv_proj(hn, w, w_col0, conv_w, conv_b, seq, tm=1024, tn=1024):
    m, k = hn.shape
    n = conv_w.shape[1]
    assert w_col0 % tn == 0
    return pl.pallas_call(
        functools.partial(_conv_proj_kernel, tm=tm, tn=tn, tiles_per_seq=seq // tm),
        grid=(m // tm, n // tn),
        in_specs=[pl.BlockSpec((tm, k), lambda i, j: (i, 0)),
                  pl.BlockSpec((k, tn), lambda i, j: (0, w_col0 // tn + j)),
                  pl.BlockSpec(conv_w.shape, lambda i, j: (0, 0)),
                  pl.BlockSpec((1, n), lambda i, j: (0, 0))],
        out_specs=pl.BlockSpec((tm, tn), lambda i, j: (i, j)),
        out_shape=jax.ShapeDtypeStruct((m, n), F32),
        scratch_shapes=[pltpu.VMEM((tm, tn), F32),
                        pltpu.VMEM((n // tn, SUBLANES, tn), F32)],
        compiler_params=_cparams(("arbitrary", "arbitrary"), 56),
        name="odd_in_proj_xbc_conv",
    )(hn, w, conv_w, conv_b.reshape(1, n))


def _ssd_kernel(sz_ref, xc_ref, dt_ref, dtb_ref, alog_ref, dsk_ref, nw_ref, pidx_ref, o_ref,
                st_ref, st2_ref, y_ref, tt_ref, cs_ref, cbd_ref, bt_ref, *e_refs, tb):
    n_chunks = tb // CHUNK

    @pl.when(pl.program_id(1) == 0)
    def _():
        st_ref[...] = jnp.zeros_like(st_ref)

    tri_bf = _tri(CHUNK).astype(BF16)
    lane = lax.broadcasted_iota(jnp.int32, (CHUNK, LANES), 1)
    rowi = lax.broadcasted_iota(jnp.int32, (CHUNK, LANES), 0)
    lane_hi = (lane >= SSD_HEADDIM).astype(jnp.int32)
    causal2 = rowi >= (lane - SSD_HEADDIM * lane_hi)
    lane1_lo = lax.broadcasted_iota(jnp.int32, (1, LANES), 1) < SSD_HEADDIM
    lane_lo = lane < SSD_HEADDIM
    nt = (((1,), (1,)), ((), ()))
    a_neg = -jnp.exp(alog_ref[...])

    for c in range(n_chunks):
        dt = _softplus(dt_ref[c * CHUNK:(c + 1) * CHUNK, :] + dtb_ref[...])
        cs = _chunk_cumsum(tri_bf, dt * a_neg, terms=3)
        w = dt * jnp.exp(cs[CHUNK - 1:CHUNK] - cs)
        cs_ref[c] = cs
        for t, arr in enumerate((cs - jnp.log(dt), w)):
            tt_ref[c, t] = jnp.concatenate([arr, arr], axis=0).T

    def expand_group(c, g):
        cs = cs_ref[c]
        for i in range(PAIRS_PER_GROUP):
            pair = g * PAIRS_PER_GROUP + i
            idx = jnp.broadcast_to(pidx_ref[pl.ds(pair, 1), :], (CHUNK, LANES))
            e_refs[c][:, pl.ds(pl.multiple_of(pair * LANES, LANES), LANES)] = jnp.take_along_axis(
                cs, idx, axis=1, mode="promise_in_bounds")

    def expand_body(g, carry):
        expand_group(0, g)
        return carry

    lax.fori_loop(0, SSD_GROUPS, expand_body, 0)

    for c in range(n_chunks):
        for g in range(SSD_GROUPS):
            b_g = xc_ref[c * CHUNK:(c + 1) * CHUNK, SSD_INNER + g * SSD_STATE:SSD_INNER + (g + 1) * SSD_STATE]
            c_g = xc_ref[c * CHUNK:(c + 1) * CHUNK,
                         SSD_INNER + SSD_BC + g * SSD_STATE:SSD_INNER + SSD_BC + (g + 1) * SSD_STATE]
            b2 = jnp.concatenate([b_g, b_g], axis=0)
            cbd_ref[c, g] = lax.dot_general(c_g.astype(BF16), b2.astype(BF16), nt,
                                            preferred_element_type=F32)
            bt_ref[c, g] = b2.T

    for c in range(n_chunks):
        r0 = c * CHUNK
        st_in, st_out = (st_ref, st2_ref) if c % 2 == 0 else (st2_ref, st_ref)

        def group_body(g, carry, c=c, r0=r0, st_in=st_in, st_out=st_out):
            c_g = xc_ref[r0:r0 + CHUNK,
                         pl.ds(pl.multiple_of(SSD_INNER + SSD_BC + g * SSD_STATE, SSD_STATE), SSD_STATE)]
            c_bf = c_g.astype(BF16)
            cb2 = cbd_ref[c, g]
            bt2 = bt_ref[c, g]
            for i in range(PAIRS_PER_GROUP):
                h_a = g * (2 * PAIRS_PER_GROUP) + 2 * i
                cols = pl.ds(pl.multiple_of(h_a * SSD_HEADDIM, LANES), LANES)

                def pair_row(t, h_a=h_a):
                    return jnp.where(lane1_lo, tt_ref[c, t, pl.ds(h_a, 1), :], tt_ref[c, t, pl.ds(h_a + 1, 1), :])

                e2 = e_refs[c][:, cols]
                att = cb2 * jnp.where(causal2, jnp.exp(e2 - pair_row(0)), 0.0)
                lhs = jnp.concatenate([att, bt2 * pair_row(1)], axis=0).astype(BF16)
                x2 = xc_ref[r0:r0 + CHUNK, cols]
                rhs = jnp.concatenate([jnp.where(lane_lo, x2, 0.0), jnp.where(lane_lo, 0.0, x2)],
                                      axis=0).astype(BF16)
                res = jnp.dot(lhs, rhs, preferred_element_type=F32)
                s_p = st_in[:, cols]
                e_out = jnp.exp(e2)
                y_off = e_out * jnp.dot(c_bf, s_p.astype(BF16), preferred_element_type=F32)
                y_ref[r0:r0 + CHUNK, cols] = res[0:CHUNK] + y_off + x2 * dsk_ref[:, cols]
                st_out[:, cols] = s_p * e_out[CHUNK - 1:CHUNK] + res[CHUNK:3 * CHUNK]
            if c + 1 < n_chunks:
                expand_group(c + 1, g)
            return carry

        lax.fori_loop(0, SSD_GROUPS, group_body, 0)

    def norm_body(r, carry):
        rows = pl.ds(pl.multiple_of(r * ROW_SLAB, ROW_SLAB), ROW_SLAB)
        v = y_ref[rows, :] * sz_ref[rows, :]
        ms = jnp.mean(v * v, axis=-1, keepdims=True)
        o_ref[rows, :] = (v * lax.rsqrt(ms + EPS) * nw_ref[...]).astype(o_ref.dtype)
        return carry

    lax.fori_loop(0, tb // ROW_SLAB, norm_body, 0)


def _ssd(sz, xc, dt, dt_bias, a_log, d_skip, norm_w, bsz, seq, tb=128):
    assert (tb // CHUNK) % 2 == 0, "the state ping-pong needs an even number of chunks per block"
    nb = seq // tb
    dsk = jnp.repeat(d_skip, SSD_HEADDIM).reshape(1, SSD_INNER)
    pair_idx = (2 * jnp.arange(SSD_HEADS // 2, dtype=jnp.int32)[:, None]
                + (jnp.arange(LANES, dtype=jnp.int32) // SSD_HEADDIM)[None, :])

    def rows(width):
        return pl.BlockSpec((tb, width), lambda b, s: (b * nb + s, 0))

    def full(shape):
        return pl.BlockSpec(shape, lambda b, s: (0,) * len(shape))

    return pl.pallas_call(
        functools.partial(_ssd_kernel, tb=tb),
        grid=(bsz, nb),
        in_specs=[rows(SSD_INNER), rows(SSD_CONV_DIM), rows(SSD_HEADS),
                  full((1, SSD_HEADS)), full((1, SSD_HEADS)), full((1, SSD_INNER)), full((1, SSD_INNER)),
                  full((SSD_HEADS // 2, LANES))],
        out_specs=rows(SSD_INNER),
        out_shape=jax.ShapeDtypeStruct((bsz * seq, SSD_INNER), BF16),
        scratch_shapes=[pltpu.VMEM((SSD_STATE, SSD_INNER), F32),
                        pltpu.VMEM((SSD_STATE, SSD_INNER), F32),
                        pltpu.VMEM((tb, SSD_INNER), F32),
                        pltpu.VMEM((tb // CHUNK, 2, SSD_HEADS, LANES), F32),
                        pltpu.VMEM((tb // CHUNK, CHUNK, SSD_HEADS), F32),
                        pltpu.VMEM((tb // CHUNK, SSD_GROUPS, CHUNK, LANES), F32),
                        pltpu.VMEM((tb // CHUNK, SSD_GROUPS, SSD_STATE, LANES), F32)]
                       + [pltpu.VMEM((CHUNK, SSD_INNER), F32)] * (tb // CHUNK),
        compiler_params=_cparams(("parallel", "arbitrary"), 48),
        name="ssd",
    )(sz, xc, dt, dt_bias.reshape(1, SSD_HEADS), a_log.reshape(1, SSD_HEADS), dsk,
      norm_w.reshape(1, SSD_INNER), pair_idx)


def _even_layer(h, norm_w, w_in, lb_logits, a_norm_w, conv_w, conv_b, wa, ba, wx, bx, lam, w_out, bsz, seq):
    hn = _rmsnorm(h, norm_w, BF16)
    hg = 4 * HG_HEADS * HG_DK
    lru = LRU_BLOCKS * LRU_BLOCK
    w_in = w_in.astype(BF16)
    proj = _matmul([hn], w_in, n=hg, tm=1024, tn=1024, tk=D_MODEL, vmem_mib=48, name="even_in_proj")
    o_a = _hgrn(proj, lb_logits, a_norm_w, bsz, seq)
    o_b = _lru_proj(hn, w_in, hg, hg + lru, conv_w, conv_b, wa, ba, wx, bx, lam, seq)
    return _matmul([o_a, o_b], w_out.astype(BF16), residual=h, n=D_MODEL, tm=1024, tn=1024, tk=2048,
                   vmem_mib=48, name="even_out_proj")


def _odd_layer(h, norm_w, w_in, conv_w, conv_b, dt_bias, a_log, d_skip, ssd_norm_w, w_out, bsz, seq):
    hn = _rmsnorm(h, norm_w, BF16)
    w_in = w_in.astype(BF16)
    sz = _matmul([hn], w_in, n=SSD_INNER, tm=1024, tn=1024, tk=D_MODEL, vmem_mib=56, name="odd_in_proj_z",
                 epilogue=_silu)
    xc = _conv_proj(hn, w_in, SSD_INNER, conv_w, conv_b, seq)
    dt = _matmul([hn], w_in, n=SSD_HEADS, w_col0=SSD_INNER + SSD_CONV_DIM, tm=1024, tn=SSD_HEADS, tk=D_MODEL,
                 vmem_mib=32, name="odd_in_proj_dt")
    y = _ssd(sz, xc, dt, dt_bias, a_log, d_skip, ssd_norm_w, bsz, seq)
    return _matmul([y], w_out.astype(BF16), residual=h, n=D_MODEL, tm=1024, tn=1024, tk=2048, vmem_mib=48,
                   name="odd_out_proj")


def kernel(x, norm_w, e_w_in, lb_logits, e_a_norm_w, e_conv_w, e_conv_b, e_wa, e_ba, e_wx, e_bx, e_lambda,
           e_w_out, o_w_in, o_conv_w, o_conv_b, o_dt_bias, o_a_log, o_d, o_norm_w, o_w_out, final_norm_w):
    bsz, seq, d = x.shape
    h = x.reshape(bsz * seq, d)
    h = _even_layer(h, norm_w[0], e_w_in[0], lb_logits, e_a_norm_w[0], e_conv_w[0], e_conv_b[0], e_wa[0],
                    e_ba[0], e_wx[0], e_bx[0], e_lambda[0], e_w_out[0], bsz, seq)
    h = _odd_layer(h, norm_w[1], o_w_in[0], o_conv_w[0], o_conv_b[0], o_dt_bias[0], o_a_log[0], o_d[0],
                   o_norm_w[0], o_w_out[0], bsz, seq)
    return _rmsnorm(h, final_norm_w, x.dtype).reshape(bsz, seq, d)
```

```python
import functools

import jax
import jax.numpy as jnp
from jax import lax
from jax.experimental import pallas as pl
from jax.experimental.pallas import tpu as pltpu

F32 = jnp.float32
BF16 = jnp.bfloat16

EPS = 1e-6
CHUNK = 64
LANES = 128
SUBLANES = 8
D_MODEL = 4096
HG_HEADS = 32
HG_DK = 128
LRU_BLOCKS = 32
LRU_BLOCK = 128
LRU_C = 8.0
CONV_W = 4
SSD_INNER = 8192
SSD_HEADS = 128
SSD_HEADDIM = 64
SSD_GROUPS = 8
SSD_STATE = 128
SSD_BC = SSD_GROUPS * SSD_STATE
SSD_CONV_DIM = SSD_INNER + 2 * SSD_BC
PAIRS_PER_GROUP = SSD_HEADS // SSD_GROUPS // 2
MIB = 1024 * 1024


def _cparams(semantics, vmem_mib):
    return pltpu.CompilerParams(dimension_semantics=semantics, vmem_limit_bytes=vmem_mib * MIB)


def _sigmoid(x):
    return 0.5 + 0.5 * jnp.tanh(0.5 * x)


def _silu(x):
    return x * _sigmoid(x)


def _softplus(x):
    return jnp.maximum(x, 0.0) + jnp.log1p(jnp.exp(-jnp.abs(x)))


def _chunk_cumsum(tri, x, terms):
    acc = None
    rem = x
    for t in range(terms):
        piece = rem.astype(BF16)
        part = jnp.dot(tri, piece, preferred_element_type=F32)
        acc = part if acc is None else acc + part
        if t + 1 < terms:
            rem = rem - piece.astype(F32)
    return acc


def _causal_conv_rows(hist, x_ref, cols, n_rows, w, b):
    row = lax.broadcasted_iota(jnp.int32, hist.shape, 0)
    shifts = range(1, CONV_W)
    prev_rot = [pltpu.roll(hist, s, 0) for s in shifts]
    for r0 in range(0, n_rows, SUBLANES):
        cur = x_ref[r0:r0 + SUBLANES, cols]
        cur_rot = [pltpu.roll(cur, s, 0) for s in shifts]
        out = b + cur * w[CONV_W - 1:CONV_W]
        for s, c_rot, p_rot in zip(shifts, cur_rot, prev_rot):
            out = out + jnp.where(row >= s, c_rot, p_rot) * w[CONV_W - 1 - s:CONV_W - s]
        prev_rot = cur_rot
        yield r0, out


def _tri(n):
    r = lax.broadcasted_iota(jnp.int32, (n, n), 0)
    c = lax.broadcasted_iota(jnp.int32, (n, n), 1)
    return r >= c


def _norm_kernel(x_ref, w_ref, o_ref):
    x = x_ref[...]
    ms = jnp.mean(x * x, axis=-1, keepdims=True)
    o_ref[...] = (x * lax.rsqrt(ms + EPS) * w_ref[...]).astype(o_ref.dtype)


def _rmsnorm(x, w, out_dtype, tm=256):
    m, d = x.shape
    return pl.pallas_call(
        _norm_kernel,
        grid=(m // tm,),
        in_specs=[pl.BlockSpec((tm, d), lambda i: (i, 0)),
                  pl.BlockSpec((1, d), lambda i: (0, 0))],
        out_specs=pl.BlockSpec((tm, d), lambda i: (i, 0)),
        out_shape=jax.ShapeDtypeStruct((m, d), out_dtype),
        compiler_params=_cparams(("parallel",), 32),
        name="rmsnorm",
    )(x, w.reshape(1, d))


def _matmul_kernel(*refs, k_ranges, has_res, epilogue):
    n_a = len(k_ranges)
    a_refs = refs[:n_a]
    b_ref = refs[n_a]
    r_ref = refs[n_a + 1] if has_res else None
    o_ref = refs[-1]
    nk = k_ranges[-1][1]
    if nk == 1:
        p = jnp.dot(a_refs[0][...], b_ref[...], preferred_element_type=F32)
        if has_res:
            p = p + r_ref[...]
        o_ref[...] = p if epilogue is None else epilogue(p)
        return
    k = pl.program_id(2)

    @pl.when(k == 0)
    def _():
        o_ref[...] = r_ref[...] if has_res else jnp.zeros(o_ref.shape, F32)

    for a_ref, (k0, k1) in zip(a_refs, k_ranges):
        @pl.when((k >= k0) & (k < k1))
        def _(a_ref=a_ref):
            o_ref[...] += jnp.dot(a_ref[...], b_ref[...], preferred_element_type=F32)


def _matmul(a_list, w, residual=None, *, n, w_col0=0, tm, tn, tk, vmem_mib, name, epilogue=None):
    m = a_list[0].shape[0]
    k_ranges, k0 = [], 0
    for a in a_list:
        k_ranges.append((k0, k0 + a.shape[1] // tk))
        k0 = k_ranges[-1][1]
    nk = k0
    assert nk * tk == w.shape[0] and w_col0 % tn == 0 and (epilogue is None or nk == 1)
    in_specs = [pl.BlockSpec((tm, tk), lambda i, j, k, k0=k0, k1=k1: (i, jnp.clip(k - k0, 0, k1 - k0 - 1)))
                for k0, k1 in k_ranges]
    in_specs.append(pl.BlockSpec((tk, tn), lambda i, j, k: (k, w_col0 // tn + j)))
    args = list(a_list) + [w]
    if residual is not None:
        in_specs.append(pl.BlockSpec((tm, tn), lambda i, j, k: (i, j)))
        args.append(residual)
    return pl.pallas_call(
        functools.partial(_matmul_kernel, k_ranges=tuple(k_ranges), has_res=residual is not None,
                          epilogue=epilogue),
        grid=(m // tm, n // tn, nk),
        in_specs=in_specs,
        out_specs=pl.BlockSpec((tm, tn), lambda i, j, k: (i, j)),
        out_shape=jax.ShapeDtypeStruct((m, n), F32),
        compiler_params=_cparams(("parallel", "parallel", "arbitrary"), vmem_mib),
        name=name,
    )(*args)


HG_SLAB = 512
ROW_SLAB = 16


def _hgrn_kernel(q_ref, f_ref, i_ref, g_ref, lbl_ref, nw_ref, o_ref,
                 st_ref, acc_ref, qin_ref, kin_ref, v_ref, qout_ref, kst_ref, gl_ref, *, tb):
    @pl.when(pl.program_id(1) == 0)
    def _():
        st_ref[...] = jnp.zeros_like(st_ref)

    n_chunks = tb // CHUNK
    n_slabs = (HG_HEADS * HG_DK) // HG_SLAB
    heads_per_slab = HG_SLAB // HG_DK
    tri = _tri(CHUNK)
    tri_bf = tri.astype(BF16)
    nt = (((1,), (1,)), ((), ()))
    tn = (((0,), (0,)), ((), ()))

    def phase_a(j, p):
        cols = pl.ds(pl.multiple_of(j * HG_SLAB, HG_SLAB), HG_SLAB)
        lg = lbl_ref[:, cols]
        e = jnp.exp(lg - jnp.max(lg, axis=0, keepdims=True))
        lb = e[0:1] / jnp.sum(e, axis=0, keepdims=True)
        half_k = 0.5 * (1.0 - lb)
        for c in range(n_chunks):
            rows = pl.ds(c * CHUNK, CHUNK)
            qp = q_ref[rows, cols]
            k = half_k - half_k * jnp.tanh(0.5 * f_ref[rows, cols])
            logf = jnp.log(1.0 - k)
            b = _chunk_cumsum(tri_bf, logf, terms=2)
            ref = b[CHUNK // 2 - 1:CHUNK // 2]
            blast = b[CHUNK - 1:CHUNK]
            d = b - ref
            q_in = _silu(qp) * jnp.exp(d)
            k_in = k * jnp.exp(-d)
            qin_ref[p, rows, :] = q_in.astype(BF16)
            kin_ref[p, rows, :] = k_in.astype(BF16)
            qout_ref[p, rows, :] = (q_in * jnp.exp(ref)).astype(BF16)
            kst_ref[p, rows, :] = (k_in * jnp.exp(blast - ref)).astype(BF16)
            v_ref[p, rows, :] = i_ref[rows, cols].astype(BF16)
            gl_ref[p, c:c + 1, :] = jnp.exp(blast)

    def phase_b(j, p):
        states = [st_ref[j * heads_per_slab + hh] for hh in range(heads_per_slab)]
        for hh in range(heads_per_slab):
            lanes = slice(hh * HG_DK, (hh + 1) * HG_DK)
            out_cols = pl.ds(pl.multiple_of((j * heads_per_slab + hh) * HG_DK, HG_DK), HG_DK)
            s_t = states[hh]
            for c in range(n_chunks):
                rows = pl.ds(c * CHUNK, CHUNK)
                v = v_ref[p, rows, lanes]
                scores = lax.dot_general(qin_ref[p, rows, lanes], kin_ref[p, rows, lanes], nt,
                                         preferred_element_type=F32)
                scores = jnp.where(tri, scores, 0.0).astype(BF16)
                lhs = jnp.concatenate([qout_ref[p, rows, lanes], scores], axis=1)
                rhs = jnp.concatenate([s_t.astype(BF16).T, v], axis=0)
                acc_ref[rows, out_cols] = jnp.dot(lhs, rhs, preferred_element_type=F32)
                s_t = (s_t * gl_ref[p, c:c + 1, lanes]
                       + lax.dot_general(v, kst_ref[p, rows, lanes], tn, preferred_element_type=F32))
            states[hh] = s_t
        for hh in range(heads_per_slab):
            st_ref[j * heads_per_slab + hh] = states[hh]

    phase_a(0, 0)

    def skew_body(jj, carry):
        j = 2 * jj
        phase_a(j + 1, 1)
        phase_b(j, 0)
        phase_a(j + 2, 0)
        phase_b(j + 1, 1)
        return carry

    lax.fori_loop(0, n_slabs // 2 - 1, skew_body, 0)
    phase_a(n_slabs - 1, 1)
    phase_b(n_slabs - 2, 0)
    phase_b(n_slabs - 1, 1)

    def norm_body(r, carry):
        rows = pl.ds(pl.multiple_of(r * ROW_SLAB, ROW_SLAB), ROW_SLAB)
        o = acc_ref[rows, :]
        ms = jnp.mean(o * o, axis=-1, keepdims=True)
        o_ref[rows, :] = (o * lax.rsqrt(ms + EPS) * nw_ref[...] * _silu(g_ref[rows, :])).astype(o_ref.dtype)
        return carry

    lax.fori_loop(0, tb // ROW_SLAB, norm_body, 0)


def _hgrn(proj, lb_logits, a_norm_w, bsz, seq, tb=128):
    width = HG_HEADS * HG_DK
    nb = seq // tb

    def col(cb):
        return pl.BlockSpec((tb, width), lambda b, s, cb=cb: (b * nb + s, cb))

    return pl.pallas_call(
        functools.partial(_hgrn_kernel, tb=tb),
        grid=(bsz, nb),
        in_specs=[col(0), col(1), col(2), col(3),
                  pl.BlockSpec(lb_logits.shape, lambda b, s: (0, 0)),
                  pl.BlockSpec((1, width), lambda b, s: (0, 0))],
        out_specs=pl.BlockSpec((tb, width), lambda b, s: (b * nb + s, 0)),
        out_shape=jax.ShapeDtypeStruct((bsz * seq, width), BF16),
        scratch_shapes=[pltpu.VMEM((HG_HEADS, HG_DK, HG_DK), F32),
                        pltpu.VMEM((tb, width), F32)]
                       + [pltpu.VMEM((2, tb, HG_SLAB), BF16)] * 5
                       + [pltpu.VMEM((2, tb // CHUNK, HG_SLAB), F32)],
        compiler_params=_cparams(("parallel", "arbitrary"), 40),
        name="hgrn2",
    )(proj, proj, proj, proj, lb_logits, a_norm_w.reshape(1, width))


LRU_TILE = 512


def _lru_block(xc, gate_in, wg, ba, bx, lam, h, row):
    gates = jnp.dot(xc.astype(BF16), wg, preferred_element_type=F32)
    quarter = (0.25 * LRU_C) * _softplus(-lam)
    t_r = jnp.tanh(0.5 * (gates[:, :LRU_BLOCK] + ba))
    tau = jnp.tanh(-quarter - quarter * t_r)
    rho = 1.0 / (1.0 - tau)
    a = (1.0 + tau) * rho
    neg_tau = -tau
    root = jnp.where(neg_tau > 0.0, neg_tau * lax.rsqrt(neg_tau), 0.0)
    ig = _sigmoid(gates[:, LRU_BLOCK:] + bx)
    u = (2.0 * rho * root) * (ig * xc)
    outs = []
    for r8 in range(xc.shape[0] // SUBLANES):
        a8 = a[r8 * SUBLANES:(r8 + 1) * SUBLANES]
        u8 = u[r8 * SUBLANES:(r8 + 1) * SUBLANES]
        for sh in (1, 2, 4):
            keep = row >= sh
            a_s = jnp.where(keep, pltpu.roll(a8, sh, 0), 1.0)
            u_s = jnp.where(keep, pltpu.roll(u8, sh, 0), 0.0)
            u8 = a8 * u_s + u8
            a8 = a8 * a_s
        h8 = a8 * h + u8
        h = h8[SUBLANES - 1:SUBLANES]
        outs.append(h8)
    return jnp.concatenate(outs, axis=0) * _silu(gate_in), h


def _lru_proj_kernel(a_ref, wxb_ref, wgb_ref, cw_ref, cb_ref, wg_ref, bg_ref, lam_ref, o_ref,
                     p_ref, hist_ref, h_ref, *, tm, tiles_per_seq):
    i = pl.program_id(0)
    j = pl.program_id(1)

    @pl.when(lax.rem(i, tiles_per_seq) == 0)
    def _():
        hist_ref[j] = jnp.zeros(hist_ref.shape[1:], F32)
        h_ref[j] = jnp.zeros(h_ref.shape[1:], F32)

    a = a_ref[...]
    p_ref[:, 0:LRU_TILE] = jnp.dot(a, wxb_ref[...], preferred_element_type=F32)
    p_ref[:, LRU_TILE:2 * LRU_TILE] = jnp.dot(a, wgb_ref[...], preferred_element_type=F32)
    row = lax.broadcasted_iota(jnp.int32, (SUBLANES, LRU_BLOCK), 0)
    hist = hist_ref[j]
    h_in = h_ref[j]
    h_out = []
    for blk in range(LRU_TILE // LRU_BLOCK):
        lanes = slice(blk * LRU_BLOCK, (blk + 1) * LRU_BLOCK)
        n = j * (LRU_TILE // LRU_BLOCK) + blk
        cols = pl.ds(pl.multiple_of(n * LRU_BLOCK, LRU_BLOCK), LRU_BLOCK)
        xc = jnp.concatenate(
            [acc for _, acc in _causal_conv_rows(hist[:, lanes], p_ref, lanes, tm, cw_ref[:, cols], cb_ref[:, cols])],
            axis=0)
        out, h_last = _lru_block(xc, p_ref[:, LRU_TILE + blk * LRU_BLOCK:LRU_TILE + (blk + 1) * LRU_BLOCK],
                                 wg_ref[n], bg_ref[0:1, cols], bg_ref[1:2, cols], lam_ref[:, cols],
                                 h_in[:, lanes], row)
        o_ref[:, lanes] = out.astype(o_ref.dtype)
        h_out.append(h_last)
    hist_ref[j] = p_ref[tm - SUBLANES:tm, 0:LRU_TILE]
    h_ref[j] = jnp.concatenate(h_out, axis=1)


def _lru_proj(hn, w, xb_col0, gb_col0, conv_w, conv_b, wa, ba, wx, bx, lam, seq, tm=1024):
    m, k = hn.shape
    width = LRU_BLOCKS * LRU_BLOCK
    n_tiles = width // LRU_TILE
    assert xb_col0 % LRU_TILE == 0 and gb_col0 % LRU_TILE == 0
    wg = jnp.concatenate([wa, wx], axis=-1).astype(BF16)
    bg = jnp.stack([ba, bx], axis=0)

    def full(shape):
        return pl.BlockSpec(shape, lambda i, j: (0,) * len(shape))

    return pl.pallas_call(
        functools.partial(_lru_proj_kernel, tm=tm, tiles_per_seq=seq // tm),
        grid=(m // tm, n_tiles),
        in_specs=[pl.BlockSpec((tm, k), lambda i, j: (i, 0)),
                  pl.BlockSpec((k, LRU_TILE), lambda i, j: (0, xb_col0 // LRU_TILE + j)),
                  pl.BlockSpec((k, LRU_TILE), lambda i, j: (0, gb_col0 // LRU_TILE + j)),
                  full(conv_w.shape), full((1, width)), full(wg.shape), full(bg.shape), full((1, width))],
        out_specs=pl.BlockSpec((tm, LRU_TILE), lambda i, j: (i, j)),
        out_shape=jax.ShapeDtypeStruct((m, width), BF16),
        scratch_shapes=[pltpu.VMEM((tm, 2 * LRU_TILE), F32),
                        pltpu.VMEM((n_tiles, SUBLANES, LRU_TILE), F32),
                        pltpu.VMEM((n_tiles, 1, LRU_TILE), F32)],
        compiler_params=_cparams(("arbitrary", "arbitrary"), 48),
        name="even_in_proj_rglru",
    )(hn, w, w, conv_w, conv_b.reshape(1, width), wg, bg, lam.reshape(1, width))


def _ssd_kernel(sz_ref, xbc_ref, dt_ref, cw_ref, cb_ref, dtb_ref, alog_ref, dsk_ref, nw_ref, pidx_ref, o_ref,
                hist_ref, xc_ref, st_ref, st2_ref, y_ref, tt_ref, cs_ref, cbd_ref, bt_ref, e_ref, *, tb):
    n_chunks = tb // CHUNK

    @pl.when(pl.program_id(1) == 0)
    def _():
        hist_ref[...] = jnp.zeros_like(hist_ref)
        st_ref[...] = jnp.zeros_like(st_ref)

    tri_bf = _tri(CHUNK).astype(BF16)
    lane = lax.broadcasted_iota(jnp.int32, (CHUNK, LANES), 1)
    rowi = lax.broadcasted_iota(jnp.int32, (CHUNK, LANES), 0)
    lane_hi = (lane >= SSD_HEADDIM).astype(jnp.int32)
    causal2 = rowi >= (lane - SSD_HEADDIM * lane_hi)
    lane1_lo = lax.broadcasted_iota(jnp.int32, (1, LANES), 1) < SSD_HEADDIM
    lane_lo = lane < SSD_HEADDIM
    nt = (((1,), (1,)), ((), ()))
    a_neg = -jnp.exp(alog_ref[...])

    for c in range(n_chunks):
        dt = _softplus(dt_ref[c * CHUNK:(c + 1) * CHUNK, :] + dtb_ref[...])
        cs = _chunk_cumsum(tri_bf, dt * a_neg, terms=3)
        w = dt * jnp.exp(cs[CHUNK - 1:CHUNK] - cs)
        cs_ref[c] = cs
        for t, arr in enumerate((cs - jnp.log(dt), w)):
            tt_ref[c, t] = jnp.concatenate([arr, arr], axis=0).T

    def expand_group(c, g):
        cs = cs_ref[c]
        for i in range(PAIRS_PER_GROUP):
            pair = g * PAIRS_PER_GROUP + i
            idx = jnp.broadcast_to(pidx_ref[pl.ds(pair, 1), :], (CHUNK, LANES))
            e_ref[c, :, pl.ds(pl.multiple_of(pair * LANES, LANES), LANES)] = jnp.take_along_axis(
                cs, idx, axis=1, mode="promise_in_bounds")

    slab = 512
    n_expand = n_chunks * SSD_GROUPS
    assert SSD_CONV_DIM // slab >= n_expand

    def conv_body(j, carry):
        job = lax.rem(j, n_expand)
        expand_group(job // SSD_GROUPS, lax.rem(job, SSD_GROUPS))
        cols = pl.ds(pl.multiple_of(j * slab, slab), slab)
        for r0, acc in _causal_conv_rows(hist_ref[:, cols], xbc_ref, cols, tb, cw_ref[:, cols], cb_ref[:, cols]):
            xc_ref[r0:r0 + SUBLANES, cols] = _silu(acc)
        return carry

    lax.fori_loop(0, SSD_CONV_DIM // slab, conv_body, 0)

    for c in range(n_chunks):
        for g in range(SSD_GROUPS):
            b_g = xc_ref[c * CHUNK:(c + 1) * CHUNK, SSD_INNER + g * SSD_STATE:SSD_INNER + (g + 1) * SSD_STATE]
            c_g = xc_ref[c * CHUNK:(c + 1) * CHUNK,
                         SSD_INNER + SSD_BC + g * SSD_STATE:SSD_INNER + SSD_BC + (g + 1) * SSD_STATE]
            b2 = jnp.concatenate([b_g, b_g], axis=0)
            cbd_ref[c, g] = lax.dot_general(c_g.astype(BF16), b2.astype(BF16), nt,
                                            preferred_element_type=F32)
            bt_ref[c, g] = b2.T

    for c in range(n_chunks):
        r0 = c * CHUNK
        st_in, st_out = (st_ref, st2_ref) if c % 2 == 0 else (st2_ref, st_ref)

        def group_body(g, carry, c=c, r0=r0, st_in=st_in, st_out=st_out):
            c_g = xc_ref[r0:r0 + CHUNK,
                         pl.ds(pl.multiple_of(SSD_INNER + SSD_BC + g * SSD_STATE, SSD_STATE), SSD_STATE)]
            c_bf = c_g.astype(BF16)
            cb2 = cbd_ref[c, g]
            bt2 = bt_ref[c, g]
            for i in range(PAIRS_PER_GROUP):
                h_a = g * (2 * PAIRS_PER_GROUP) + 2 * i
                cols = pl.ds(pl.multiple_of(h_a * SSD_HEADDIM, LANES), LANES)

                def pair_row(t, h_a=h_a):
                    return jnp.where(lane1_lo, tt_ref[c, t, pl.ds(h_a, 1), :], tt_ref[c, t, pl.ds(h_a + 1, 1), :])

                e2 = e_ref[c, :, cols]
                att = cb2 * jnp.where(causal2, jnp.exp(e2 - pair_row(0)), 0.0)
                lhs = jnp.concatenate([att, bt2 * pair_row(1)], axis=0).astype(BF16)
                x2 = xc_ref[r0:r0 + CHUNK, cols]
                rhs = jnp.concatenate([jnp.where(lane_lo, x2, 0.0), jnp.where(lane_lo, 0.0, x2)],
                                      axis=0).astype(BF16)
                res = jnp.dot(lhs, rhs, preferred_element_type=F32)
                s_p = st_in[:, cols]
                e_out = jnp.exp(e2)
                y_off = e_out * jnp.dot(c_bf, s_p.astype(BF16), preferred_element_type=F32)
                y_ref[r0:r0 + CHUNK, cols] = res[0:CHUNK] + y_off + x2 * dsk_ref[:, cols]
                st_out[:, cols] = s_p * e_out[CHUNK - 1:CHUNK] + res[CHUNK:3 * CHUNK]
            return carry

        lax.fori_loop(0, SSD_GROUPS, group_body, 0)

    hist_ref[...] = xbc_ref[tb - SUBLANES:tb, :]

    def norm_body(r, carry):
        rows = pl.ds(pl.multiple_of(r * ROW_SLAB, ROW_SLAB), ROW_SLAB)
        v = y_ref[rows, :] * sz_ref[rows, :]
        ms = jnp.mean(v * v, axis=-1, keepdims=True)
        o_ref[rows, :] = (v * lax.rsqrt(ms + EPS) * nw_ref[...]).astype(o_ref.dtype)
        return carry

    lax.fori_loop(0, tb // ROW_SLAB, norm_body, 0)


def _ssd(sz, xbc, dt, conv_w, conv_b, dt_bias, a_log, d_skip, norm_w, bsz, seq, tb=128):
    assert (tb // CHUNK) % 2 == 0, "the state ping-pong needs an even number of chunks per block"
    nb = seq // tb
    dsk = jnp.repeat(d_skip, SSD_HEADDIM).reshape(1, SSD_INNER)
    pair_idx = (2 * jnp.arange(SSD_HEADS // 2, dtype=jnp.int32)[:, None]
                + (jnp.arange(LANES, dtype=jnp.int32) // SSD_HEADDIM)[None, :])

    def rows(width):
        return pl.BlockSpec((tb, width), lambda b, s: (b * nb + s, 0))

    def full(shape):
        return pl.BlockSpec(shape, lambda b, s: (0,) * len(shape))

    return pl.pallas_call(
        functools.partial(_ssd_kernel, tb=tb),
        grid=(bsz, nb),
        in_specs=[rows(SSD_INNER), rows(SSD_CONV_DIM), rows(SSD_HEADS),
                  full(conv_w.shape), full((1, SSD_CONV_DIM)),
                  full((1, SSD_HEADS)), full((1, SSD_HEADS)), full((1, SSD_INNER)), full((1, SSD_INNER)),
                  full((SSD_HEADS // 2, LANES))],
        out_specs=rows(SSD_INNER),
        out_shape=jax.ShapeDtypeStruct((bsz * seq, SSD_INNER), BF16),
        scratch_shapes=[pltpu.VMEM((SUBLANES, SSD_CONV_DIM), F32),
                        pltpu.VMEM((tb, SSD_CONV_DIM), F32),
                        pltpu.VMEM((SSD_STATE, SSD_INNER), F32),
                        pltpu.VMEM((SSD_STATE, SSD_INNER), F32),
                        pltpu.VMEM((tb, SSD_INNER), F32),
                        pltpu.VMEM((tb // CHUNK, 2, SSD_HEADS, LANES), F32),
                        pltpu.VMEM((tb // CHUNK, CHUNK, SSD_HEADS), F32),
                        pltpu.VMEM((tb // CHUNK, SSD_GROUPS, CHUNK, LANES), F32),
                        pltpu.VMEM((tb // CHUNK, SSD_GROUPS, SSD_STATE, LANES), F32),
                        pltpu.VMEM((tb // CHUNK, CHUNK, SSD_INNER), F32)],
        compiler_params=_cparams(("parallel", "arbitrary"), 56),
        name="ssd",
    )(sz, xbc, dt, conv_w, conv_b.reshape(1, SSD_CONV_DIM), dt_bias.reshape(1, SSD_HEADS),
      a_log.reshape(1, SSD_HEADS), dsk, norm_w.reshape(1, SSD_INNER), pair_idx)


def _even_layer(h, norm_w, w_in, lb_logits, a_norm_w, conv_w, conv_b, wa, ba, wx, bx, lam, w_out, bsz, seq):
    hn = _rmsnorm(h, norm_w, BF16)
    hg = 4 * HG_HEADS * HG_DK
    lru = LRU_BLOCKS * LRU_BLOCK
    w_in = w_in.astype(BF16)
    proj = _matmul([hn], w_in, n=hg, tm=1024, tn=1024, tk=D_MODEL, vmem_mib=48, name="even_in_proj")
    o_a = _hgrn(proj, lb_logits, a_norm_w, bsz, seq)
    o_b = _lru_proj(hn, w_in, hg, hg + lru, conv_w, conv_b, wa, ba, wx, bx, lam, seq)
    return _matmul([o_a, o_b], w_out.astype(BF16), residual=h, n=D_MODEL, tm=512, tn=1024, tk=D_MODEL,
                   vmem_mib=48, name="even_out_proj")


def _odd_layer(h, norm_w, w_in, conv_w, conv_b, dt_bias, a_log, d_skip, ssd_norm_w, w_out, bsz, seq):
    hn = _rmsnorm(h, norm_w, BF16)
    w_in = w_in.astype(BF16)
    sz = _matmul([hn], w_in, n=SSD_INNER, tm=1024, tn=1024, tk=D_MODEL, vmem_mib=56, name="odd_in_proj_z",
                 epilogue=_silu)
    xbc = _matmul([hn], w_in, n=SSD_CONV_DIM, w_col0=SSD_INNER, tm=1024, tn=1024, tk=D_MODEL, vmem_mib=48,
                  name="odd_in_proj_xbc")
    dt = _matmul([hn], w_in, n=SSD_HEADS, w_col0=SSD_INNER + SSD_CONV_DIM, tm=1024, tn=SSD_HEADS, tk=D_MODEL,
                 vmem_mib=32, name="odd_in_proj_dt")
    y = _ssd(sz, xbc, dt, conv_w, conv_b, dt_bias, a_log, d_skip, ssd_norm_w, bsz, seq)
    return _matmul([y], w_out.astype(BF16), residual=h, n=D_MODEL, tm=1024, tn=1024, tk=D_MODEL, vmem_mib=56,
                   name="odd_out_proj")


def kernel(x, norm_w, e_w_in, lb_logits, e_a_norm_w, e_conv_w, e_conv_b, e_wa, e_ba, e_wx, e_bx, e_lambda,
           e_w_out, o_w_in, o_conv_w, o_conv_b, o_dt_bias, o_a_log, o_d, o_norm_w, o_w_out, final_norm_w):
    bsz, seq, d = x.shape
    h = x.reshape(bsz * seq, d)
    h = _even_layer(h, norm_w[0], e_w_in[0], lb_logits, e_a_norm_w[0], e_conv_w[0], e_conv_b[0], e_wa[0],
                    e_ba[0], e_wx[0], e_bx[0], e_lambda[0], e_w_out[0], bsz, seq)
    h = _odd_layer(h, norm_w[1], o_w_in[0], o_conv_w[0], o_conv_b[0], o_dt_bias[0], o_a_log[0], o_d[0],
                   o_norm_w[0], o_w_out[0], bsz, seq)
    return _rmsnorm(h, final_norm_w, x.dtype).reshape(bsz, seq, d)
```

```python
import functools

import jax
import jax.numpy as jnp
from jax import lax
from jax.experimental import pallas as pl
from jax.experimental.pallas import tpu as pltpu

F32 = jnp.float32
BF16 = jnp.bfloat16

EPS = 1e-6
CHUNK = 64
LANES = 128
SUBLANES = 8
D_MODEL = 4096
HG_HEADS = 32
HG_DK = 128
LRU_BLOCKS = 32
LRU_BLOCK = 128
LRU_C = 8.0
CONV_W = 4
SSD_INNER = 8192
SSD_HEADS = 128
SSD_HEADDIM = 64
SSD_GROUPS = 8
SSD_STATE = 128
SSD_BC = SSD_GROUPS * SSD_STATE
SSD_CONV_DIM = SSD_INNER + 2 * SSD_BC
PAIRS_PER_GROUP = SSD_HEADS // SSD_GROUPS // 2
MIB = 1024 * 1024


def _cparams(semantics, vmem_mib):
    return pltpu.CompilerParams(dimension_semantics=semantics, vmem_limit_bytes=vmem_mib * MIB)


def _sigmoid(x):
    return 0.5 + 0.5 * jnp.tanh(0.5 * x)


def _silu(x):
    return x * _sigmoid(x)


def _softplus(x):
    return jnp.maximum(x, 0.0) + jnp.log1p(jnp.exp(-jnp.abs(x)))


def _chunk_cumsum(tri, x, terms):
    acc = None
    rem = x
    for t in range(terms):
        piece = rem.astype(BF16)
        part = jnp.dot(tri, piece, preferred_element_type=F32)
        acc = part if acc is None else acc + part
        if t + 1 < terms:
            rem = rem - piece.astype(F32)
    return acc


def _causal_conv_rows(hist, x_ref, cols, n_rows, w, b):
    row = lax.broadcasted_iota(jnp.int32, hist.shape, 0)
    shifts = range(1, CONV_W)
    prev_rot = [pltpu.roll(hist, s, 0) for s in shifts]
    for r0 in range(0, n_rows, SUBLANES):
        cur = x_ref[r0:r0 + SUBLANES, cols]
        cur_rot = [pltpu.roll(cur, s, 0) for s in shifts]
        out = b + cur * w[CONV_W - 1:CONV_W]
        for s, c_rot, p_rot in zip(shifts, cur_rot, prev_rot):
            out = out + jnp.where(row >= s, c_rot, p_rot) * w[CONV_W - 1 - s:CONV_W - s]
        prev_rot = cur_rot
        yield r0, out


def _tri(n):
    r = lax.broadcasted_iota(jnp.int32, (n, n), 0)
    c = lax.broadcasted_iota(jnp.int32, (n, n), 1)
    return r >= c


def _norm_kernel(x_ref, w_ref, o_ref):
    x = x_ref[...]
    ms = jnp.mean(x * x, axis=-1, keepdims=True)
    o_ref[...] = (x * lax.rsqrt(ms + EPS) * w_ref[...]).astype(o_ref.dtype)


def _rmsnorm(x, w, out_dtype, tm=256):
    m, d = x.shape
    return pl.pallas_call(
        _norm_kernel,
        grid=(m // tm,),
        in_specs=[pl.BlockSpec((tm, d), lambda i: (i, 0)),
                  pl.BlockSpec((1, d), lambda i: (0, 0))],
        out_specs=pl.BlockSpec((tm, d), lambda i: (i, 0)),
        out_shape=jax.ShapeDtypeStruct((m, d), out_dtype),
        compiler_params=_cparams(("parallel",), 32),
        name="rmsnorm",
    )(x, w.reshape(1, d))


def _matmul_kernel(*refs, k_ranges, has_res, n_aux, epilogue):
    n_a = len(k_ranges)
    a_refs = refs[:n_a]
    b_ref = refs[n_a]
    r_ref = refs[n_a + 1] if has_res else None
    aux_refs = refs[len(refs) - 1 - n_aux:len(refs) - 1]
    o_ref = refs[-1]
    nk = k_ranges[-1][1]
    if nk == 1:
        p = jnp.dot(a_refs[0][...], b_ref[...], preferred_element_type=F32)
        if has_res:
            p = p + r_ref[...]
        if epilogue is not None:
            p = epilogue(p, *[r[...] for r in aux_refs])
        o_ref[...] = p.astype(o_ref.dtype)
        return
    k = pl.program_id(2)

    def first_step(a_ref):
        p = jnp.dot(a_ref[...], b_ref[...], preferred_element_type=F32)
        o_ref[...] = p + r_ref[...] if has_res else p

    def later_step(a_ref):
        o_ref[...] += jnp.dot(a_ref[...], b_ref[...], preferred_element_type=F32)

    for a_ref, (k0, k1) in zip(a_refs, k_ranges):
        if k0 == 0:
            pl.when(k == 0)(functools.partial(first_step, a_ref))
            k0 = 1
        if k1 > k0:
            pl.when((k >= k0) & (k < k1))(functools.partial(later_step, a_ref))


def _matmul(a_list, w, residual=None, *, n, w_col0=0, tm, tn, tk, vmem_mib, name, epilogue=None, aux=(),
            out_dtype=F32):
    m = a_list[0].shape[0]
    k_ranges, k0 = [], 0
    for a in a_list:
        k_ranges.append((k0, k0 + a.shape[1] // tk))
        k0 = k_ranges[-1][1]
    nk = k0
    assert nk * tk == w.shape[0] and w_col0 % tn == 0
    assert nk == 1 or (epilogue is None and out_dtype == F32)
    in_specs = [pl.BlockSpec((tm, tk), lambda i, j, k, k0=k0, k1=k1: (i, jnp.clip(k - k0, 0, k1 - k0 - 1)))
                for k0, k1 in k_ranges]
    in_specs.append(pl.BlockSpec((tk, tn), lambda i, j, k: (k, w_col0 // tn + j)))
    args = list(a_list) + [w]
    if residual is not None:
        in_specs.append(pl.BlockSpec((tm, tn), lambda i, j, k: (i, j)))
        args.append(residual)
    for x in aux:
        in_specs.append(pl.BlockSpec((x.shape[0], tn), lambda i, j, k: (0, j)))
        args.append(x)
    return pl.pallas_call(
        functools.partial(_matmul_kernel, k_ranges=tuple(k_ranges), has_res=residual is not None,
                          n_aux=len(aux), epilogue=epilogue),
        grid=(m // tm, n // tn, nk),
        in_specs=in_specs,
        out_specs=pl.BlockSpec((tm, tn), lambda i, j, k: (i, j)),
        out_shape=jax.ShapeDtypeStruct((m, n), out_dtype),
        compiler_params=_cparams(("parallel", "parallel", "arbitrary"), vmem_mib),
        name=name,
    )(*args)


HG_SLAB = 512
ROW_SLAB = 16


def _hgrn_logf(f_pre, lb_logits):
    e = jnp.exp(lb_logits - jnp.max(lb_logits, axis=0, keepdims=True))
    lb = e[0:1] / jnp.sum(e, axis=0, keepdims=True)
    half_k = 0.5 * (1.0 - lb)
    return jnp.log(1.0 - (half_k - half_k * jnp.tanh(0.5 * f_pre)))


def _hgrn_kernel(sq_ref, lf_ref, v_ref, sg_ref, nw_ref, o_ref,
                 st_ref, acc_ref, qin_ref, kin_ref, qout_ref, kst_ref, gl_ref, *, tb):
    @pl.when(pl.program_id(1) == 0)
    def _():
        st_ref[...] = jnp.zeros_like(st_ref)

    n_chunks = tb // CHUNK
    n_slabs = (HG_HEADS * HG_DK) // HG_SLAB
    heads_per_slab = HG_SLAB // HG_DK
    tri = _tri(CHUNK)
    tri_bf = tri.astype(BF16)
    nt = (((1,), (1,)), ((), ()))
    tn = (((0,), (0,)), ((), ()))

    def phase_a(j, p):
        cols = pl.ds(pl.multiple_of(j * HG_SLAB, HG_SLAB), HG_SLAB)
        for c in range(n_chunks):
            rows = pl.ds(c * CHUNK, CHUNK)
            logf = lf_ref[rows, cols]
            k = 1.0 - jnp.exp(logf)
            b = _chunk_cumsum(tri_bf, logf, terms=2)
            ref = b[CHUNK // 2 - 1:CHUNK // 2]
            blast = b[CHUNK - 1:CHUNK]
            d = b - ref
            q_in = sq_ref[rows, cols] * jnp.exp(d)
            k_in = k * jnp.exp(-d)
            qin_ref[p, rows, :] = q_in.astype(BF16)
            kin_ref[p, rows, :] = k_in.astype(BF16)
            qout_ref[p, rows, :] = (q_in * jnp.exp(ref)).astype(BF16)
            kst_ref[p, rows, :] = (k_in * jnp.exp(blast - ref)).astype(BF16)
            gl_ref[p, c:c + 1, :] = jnp.exp(blast)

    def phase_b(j, p):
        states = [st_ref[j * heads_per_slab + hh] for hh in range(heads_per_slab)]
        for hh in range(heads_per_slab):
            lanes = slice(hh * HG_DK, (hh + 1) * HG_DK)
            out_cols = pl.ds(pl.multiple_of((j * heads_per_slab + hh) * HG_DK, HG_DK), HG_DK)
            s_t = states[hh]
            for c in range(n_chunks):
                rows = pl.ds(c * CHUNK, CHUNK)
                v = v_ref[rows, out_cols]
                scores = lax.dot_general(qin_ref[p, rows, lanes], kin_ref[p, rows, lanes], nt,
                                         preferred_element_type=F32)
                scores = jnp.where(tri, scores, 0.0).astype(BF16)
                lhs = jnp.concatenate([qout_ref[p, rows, lanes], scores], axis=1)
                rhs = jnp.concatenate([s_t.astype(BF16).T, v], axis=0)
                acc_ref[rows, out_cols] = jnp.dot(lhs, rhs, preferred_element_type=F32)
                s_t = (s_t * gl_ref[p, c:c + 1, lanes]
                       + lax.dot_general(v, kst_ref[p, rows, lanes], tn, preferred_element_type=F32))
            states[hh] = s_t
        for hh in range(heads_per_slab):
            st_ref[j * heads_per_slab + hh] = states[hh]

    phase_a(0, 0)

    def skew_body(jj, carry):
        j = 2 * jj
        phase_a(j + 1, 1)
        phase_b(j, 0)
        phase_a(j + 2, 0)
        phase_b(j + 1, 1)
        return carry

    lax.fori_loop(0, n_slabs // 2 - 1, skew_body, 0)
    phase_a(n_slabs - 1, 1)
    phase_b(n_slabs - 2, 0)
    phase_b(n_slabs - 1, 1)

    def norm_body(r, carry):
        rows = pl.ds(pl.multiple_of(r * ROW_SLAB, ROW_SLAB), ROW_SLAB)
        o = acc_ref[rows, :]
        ms = jnp.mean(o * o, axis=-1, keepdims=True)
        o_ref[rows, :] = (o * lax.rsqrt(ms + EPS) * nw_ref[...] * sg_ref[rows, :]).astype(o_ref.dtype)
        return carry

    lax.fori_loop(0, tb // ROW_SLAB, norm_body, 0)


def _hgrn(sq, lf, v, sg, a_norm_w, bsz, seq, tb=128):
    width = HG_HEADS * HG_DK
    nb = seq // tb
    rows = pl.BlockSpec((tb, width), lambda b, s: (b * nb + s, 0))
    return pl.pallas_call(
        functools.partial(_hgrn_kernel, tb=tb),
        grid=(bsz, nb),
        in_specs=[rows, rows, rows, rows, pl.BlockSpec((1, width), lambda b, s: (0, 0))],
        out_specs=rows,
        out_shape=jax.ShapeDtypeStruct((bsz * seq, width), BF16),
        scratch_shapes=[pltpu.VMEM((HG_HEADS, HG_DK, HG_DK), F32),
                        pltpu.VMEM((tb, width), F32)]
                       + [pltpu.VMEM((2, tb, HG_SLAB), BF16)] * 4
                       + [pltpu.VMEM((2, tb // CHUNK, HG_SLAB), F32)],
        compiler_params=_cparams(("parallel", "arbitrary"), 40),
        name="hgrn2",
    )(sq, lf, v, sg, a_norm_w.reshape(1, width))


LRU_TILE = 512


def _lru_block(xc, gate_in, wg, ba, bx, lam, h, row):
    gates = jnp.dot(xc.astype(BF16), wg, preferred_element_type=F32)
    quarter = (0.25 * LRU_C) * _softplus(-lam)
    t_r = jnp.tanh(0.5 * (gates[:, :LRU_BLOCK] + ba))
    tau = jnp.tanh(-quarter - quarter * t_r)
    rho = 1.0 / (1.0 - tau)
    a = (1.0 + tau) * rho
    neg_tau = -tau
    root = jnp.where(neg_tau > 0.0, neg_tau * lax.rsqrt(neg_tau), 0.0)
    ig = _sigmoid(gates[:, LRU_BLOCK:] + bx)
    u = (2.0 * rho * root) * (ig * xc)
    outs = []
    for r8 in range(xc.shape[0] // SUBLANES):
        a8 = a[r8 * SUBLANES:(r8 + 1) * SUBLANES]
        u8 = u[r8 * SUBLANES:(r8 + 1) * SUBLANES]
        for sh in (1, 2, 4):
            keep = row >= sh
            a_s = jnp.where(keep, pltpu.roll(a8, sh, 0), 1.0)
            u_s = jnp.where(keep, pltpu.roll(u8, sh, 0), 0.0)
            u8 = a8 * u_s + u8
            a8 = a8 * a_s
        h8 = a8 * h + u8
        h = h8[SUBLANES - 1:SUBLANES]
        outs.append(h8)
    return jnp.concatenate(outs, axis=0) * _silu(gate_in), h


def _lru_proj_kernel(a_ref, wxb_ref, wgb_ref, cw_ref, cb_ref, wg_ref, bg_ref, lam_ref, o_ref,
                     p_ref, hist_ref, h_ref, *, tm, tiles_per_seq):
    i = pl.program_id(0)
    j = pl.program_id(1)

    @pl.when(lax.rem(i, tiles_per_seq) == 0)
    def _():
        hist_ref[j] = jnp.zeros(hist_ref.shape[1:], F32)
        h_ref[j] = jnp.zeros(h_ref.shape[1:], F32)

    a = a_ref[...]
    p_ref[:, 0:LRU_TILE] = jnp.dot(a, wxb_ref[...], preferred_element_type=F32)
    p_ref[:, LRU_TILE:2 * LRU_TILE] = jnp.dot(a, wgb_ref[...], preferred_element_type=F32)
    row = lax.broadcasted_iota(jnp.int32, (SUBLANES, LRU_BLOCK), 0)
    hist = hist_ref[j]
    h_in = h_ref[j]
    h_out = []
    for blk in range(LRU_TILE // LRU_BLOCK):
        lanes = slice(blk * LRU_BLOCK, (blk + 1) * LRU_BLOCK)
        n = j * (LRU_TILE // LRU_BLOCK) + blk
        cols = pl.ds(pl.multiple_of(n * LRU_BLOCK, LRU_BLOCK), LRU_BLOCK)
        xc = jnp.concatenate(
            [acc for _, acc in _causal_conv_rows(hist[:, lanes], p_ref, lanes, tm, cw_ref[:, cols], cb_ref[:, cols])],
            axis=0)
        out, h_last = _lru_block(xc, p_ref[:, LRU_TILE + blk * LRU_BLOCK:LRU_TILE + (blk + 1) * LRU_BLOCK],
                                 wg_ref[n], bg_ref[0:1, cols], bg_ref[1:2, cols], lam_ref[:, cols],
                                 h_in[:, lanes], row)
        o_ref[:, lanes] = out.astype(o_ref.dtype)
        h_out.append(h_last)
    hist_ref[j] = p_ref[tm - SUBLANES:tm, 0:LRU_TILE]
    h_ref[j] = jnp.concatenate(h_out, axis=1)


def _lru_proj(hn, w, xb_col0, gb_col0, conv_w, conv_b, wa, ba, wx, bx, lam, seq, tm=1024):
    m, k = hn.shape
    width = LRU_BLOCKS * LRU_BLOCK
    n_tiles = width // LRU_TILE
    assert xb_col0 % LRU_TILE == 0 and gb_col0 % LRU_TILE == 0
    wg = jnp.concatenate([wa, wx], axis=-1).astype(BF16)
    bg = jnp.stack([ba, bx], axis=0)

    def full(shape):
        return pl.BlockSpec(shape, lambda i, j: (0,) * len(shape))

    return pl.pallas_call(
        functools.partial(_lru_proj_kernel, tm=tm, tiles_per_seq=seq // tm),
        grid=(m // tm, n_tiles),
        in_specs=[pl.BlockSpec((tm, k), lambda i, j: (i, 0)),
                  pl.BlockSpec((k, LRU_TILE), lambda i, j: (0, xb_col0 // LRU_TILE + j)),
                  pl.BlockSpec((k, LRU_TILE), lambda i, j: (0, gb_col0 // LRU_TILE + j)),
                  full(conv_w.shape), full((1, width)), full(wg.shape), full(bg.shape), full((1, width))],
        out_specs=pl.BlockSpec((tm, LRU_TILE), lambda i, j: (i, j)),
        out_shape=jax.ShapeDtypeStruct((m, width), BF16),
        scratch_shapes=[pltpu.VMEM((tm, 2 * LRU_TILE), F32),
                        pltpu.VMEM((n_tiles, SUBLANES, LRU_TILE), F32),
                        pltpu.VMEM((n_tiles, 1, LRU_TILE), F32)],
        compiler_params=_cparams(("arbitrary", "arbitrary"), 48),
        name="even_in_proj_rglru",
    )(hn, w, w, conv_w, conv_b.reshape(1, width), wg, bg, lam.reshape(1, width))


def _ssd_kernel(sz_ref, xbc_ref, dt_ref, cw_ref, cb_ref, dtb_ref, alog_ref, dsk_ref, nw_ref, pidx_ref, o_ref,
                hist_ref, xc_ref, st_ref, st2_ref, y_ref, tt_ref, cs_ref, cbd_ref, bt_ref, e_ref, *, tb):
    n_chunks = tb // CHUNK

    @pl.when(pl.program_id(1) == 0)
    def _():
        hist_ref[...] = jnp.zeros_like(hist_ref)
        st_ref[...] = jnp.zeros_like(st_ref)

    tri_bf = _tri(CHUNK).astype(BF16)
    lane = lax.broadcasted_iota(jnp.int32, (CHUNK, LANES), 1)
    rowi = lax.broadcasted_iota(jnp.int32, (CHUNK, LANES), 0)
    lane_hi = (lane >= SSD_HEADDIM).astype(jnp.int32)
    causal2 = rowi >= (lane - SSD_HEADDIM * lane_hi)
    lane1_lo = lax.broadcasted_iota(jnp.int32, (1, LANES), 1) < SSD_HEADDIM
    lane_lo = lane < SSD_HEADDIM
    nt = (((1,), (1,)), ((), ()))
    a_neg = -jnp.exp(alog_ref[...])

    for c in range(n_chunks):
        dt = _softplus(dt_ref[c * CHUNK:(c + 1) * CHUNK, :] + dtb_ref[...])
        cs = _chunk_cumsum(tri_bf, dt * a_neg, terms=3)
        w = dt * jnp.exp(cs[CHUNK - 1:CHUNK] - cs)
        cs_ref[c] = cs
        for t, arr in enumerate((cs - jnp.log(dt), w)):
            tt_ref[c, t] = jnp.concatenate([arr, arr], axis=0).T

    def expand_group(c, g):
        cs = cs_ref[c]
        for i in range(PAIRS_PER_GROUP):
            pair = g * PAIRS_PER_GROUP + i
            idx = jnp.broadcast_to(pidx_ref[pl.ds(pair, 1), :], (CHUNK, LANES))
            e_ref[c, :, pl.ds(pl.multiple_of(pair * LANES, LANES), LANES)] = jnp.take_along_axis(
                cs, idx, axis=1, mode="promise_in_bounds")

    slab = 512
    n_expand = n_chunks * SSD_GROUPS
    assert SSD_CONV_DIM // slab >= n_expand

    def conv_body(j, carry):
        job = lax.rem(j, n_expand)
        expand_group(job // SSD_GROUPS, lax.rem(job, SSD_GROUPS))
        cols = pl.ds(pl.multiple_of(j * slab, slab), slab)
        for r0, acc in _causal_conv_rows(hist_ref[:, cols], xbc_ref, cols, tb, cw_ref[:, cols], cb_ref[:, cols]):
            xc_ref[r0:r0 + SUBLANES, cols] = _silu(acc)
        return carry

    lax.fori_loop(0, SSD_CONV_DIM // slab, conv_body, 0)

    for c in range(n_chunks):
        for g in range(SSD_GROUPS):
            b_g = xc_ref[c * CHUNK:(c + 1) * CHUNK, SSD_INNER + g * SSD_STATE:SSD_INNER + (g + 1) * SSD_STATE]
            c_g = xc_ref[c * CHUNK:(c + 1) * CHUNK,
                         SSD_INNER + SSD_BC + g * SSD_STATE:SSD_INNER + SSD_BC + (g + 1) * SSD_STATE]
            b2 = jnp.concatenate([b_g, b_g], axis=0)
            cbd_ref[c, g] = lax.dot_general(c_g.astype(BF16), b2.astype(BF16), nt,
                                            preferred_element_type=F32)
            bt_ref[c, g] = b2.T

    for c in range(n_chunks):
        r0 = c * CHUNK
        st_in, st_out = (st_ref, st2_ref) if c % 2 == 0 else (st2_ref, st_ref)

        def group_body(g, carry, c=c, r0=r0, st_in=st_in, st_out=st_out):
            c_g = xc_ref[r0:r0 + CHUNK,
                         pl.ds(pl.multiple_of(SSD_INNER + SSD_BC + g * SSD_STATE, SSD_STATE), SSD_STATE)]
            c_bf = c_g.astype(BF16)
            cb2 = cbd_ref[c, g]
            bt2 = bt_ref[c, g]
            for i in range(PAIRS_PER_GROUP):
                h_a = g * (2 * PAIRS_PER_GROUP) + 2 * i
                cols = pl.ds(pl.multiple_of(h_a * SSD_HEADDIM, LANES), LANES)

                def pair_row(t, h_a=h_a):
                    return jnp.where(lane1_lo, tt_ref[c, t, pl.ds(h_a, 1), :], tt_ref[c, t, pl.ds(h_a + 1, 1), :])

                e2 = e_ref[c, :, cols]
                att = cb2 * jnp.where(causal2, jnp.exp(e2 - pair_row(0)), 0.0)
                lhs = jnp.concatenate([att, bt2 * pair_row(1)], axis=0).astype(BF16)
                x2 = xc_ref[r0:r0 + CHUNK, cols]
                rhs = jnp.concatenate([jnp.where(lane_lo, x2, 0.0), jnp.where(lane_lo, 0.0, x2)],
                                      axis=0).astype(BF16)
                res = jnp.dot(lhs, rhs, preferred_element_type=F32)
                s_p = st_in[:, cols]
                e_out = jnp.exp(e2)
                y_off = e_out * jnp.dot(c_bf, s_p.astype(BF16), preferred_element_type=F32)
                y_ref[r0:r0 + CHUNK, cols] = res[0:CHUNK] + y_off + x2 * dsk_ref[:, cols]
                st_out[:, cols] = s_p * e_out[CHUNK - 1:CHUNK] + res[CHUNK:3 * CHUNK]
            return carry

        lax.fori_loop(0, SSD_GROUPS, group_body, 0)

    hist_ref[...] = xbc_ref[tb - SUBLANES:tb, :]

    def norm_body(r, carry):
        rows = pl.ds(pl.multiple_of(r * ROW_SLAB, ROW_SLAB), ROW_SLAB)
        v = y_ref[rows, :] * sz_ref[rows, :]
        ms = jnp.mean(v * v, axis=-1, keepdims=True)
        o_ref[rows, :] = (v * lax.rsqrt(ms + EPS) * nw_ref[...]).astype(o_ref.dtype)
        return carry

    lax.fori_loop(0, tb // ROW_SLAB, norm_body, 0)


def _ssd(sz, xbc, dt, conv_w, conv_b, dt_bias, a_log, d_skip, norm_w, bsz, seq, tb=128):
    assert (tb // CHUNK) % 2 == 0, "the state ping-pong needs an even number of chunks per block"
    nb = seq // tb
    dsk = jnp.repeat(d_skip, SSD_HEADDIM).reshape(1, SSD_INNER)
    pair_idx = (2 * jnp.arange(SSD_HEADS // 2, dtype=jnp.int32)[:, None]
                + (jnp.arange(LANES, dtype=jnp.int32) // SSD_HEADDIM)[None, :])

    def rows(width):
        return pl.BlockSpec((tb, width), lambda b, s: (b * nb + s, 0))

    def full(shape):
        return pl.BlockSpec(shape, lambda b, s: (0,) * len(shape))

    return pl.pallas_call(
        functools.partial(_ssd_kernel, tb=tb),
        grid=(bsz, nb),
        in_specs=[rows(SSD_INNER), rows(SSD_CONV_DIM), rows(SSD_HEADS),
                  full(conv_w.shape), full((1, SSD_CONV_DIM)),
                  full((1, SSD_HEADS)), full((1, SSD_HEADS)), full((1, SSD_INNER)), full((1, SSD_INNER)),
                  full((SSD_HEADS // 2, LANES))],
        out_specs=rows(SSD_INNER),
        out_shape=jax.ShapeDtypeStruct((bsz * seq, SSD_INNER), BF16),
        scratch_shapes=[pltpu.VMEM((SUBLANES, SSD_CONV_DIM), F32),
                        pltpu.VMEM((tb, SSD_CONV_DIM), F32),
                        pltpu.VMEM((SSD_STATE, SSD_INNER), F32),
                        pltpu.VMEM((SSD_STATE, SSD_INNER), F32),
                        pltpu.VMEM((tb, SSD_INNER), F32),
                        pltpu.VMEM((tb // CHUNK, 2, SSD_HEADS, LANES), F32),
                        pltpu.VMEM((tb // CHUNK, CHUNK, SSD_HEADS), F32),
                        pltpu.VMEM((tb // CHUNK, SSD_GROUPS, CHUNK, LANES), F32),
                        pltpu.VMEM((tb // CHUNK, SSD_GROUPS, SSD_STATE, LANES), F32),
                        pltpu.VMEM((tb // CHUNK, CHUNK, SSD_INNER), F32)],
        compiler_params=_cparams(("parallel", "arbitrary"), 56),
        name="ssd",
    )(sz, xbc, dt, conv_w, conv_b.reshape(1, SSD_CONV_DIM), dt_bias.reshape(1, SSD_HEADS),
      a_log.reshape(1, SSD_HEADS), dsk, norm_w.reshape(1, SSD_INNER), pair_idx)


def _even_layer(h, norm_w, w_in, lb_logits, a_norm_w, conv_w, conv_b, wa, ba, wx, bx, lam, w_out, bsz, seq):
    hn = _rmsnorm(h, norm_w, BF16)
    hg = 4 * HG_HEADS * HG_DK
    lru = LRU_BLOCKS * LRU_BLOCK
    w_in = w_in.astype(BF16)
    hw = HG_HEADS * HG_DK
    proj = functools.partial(_matmul, [hn], w_in, n=hw, tm=1024, tn=1024, tk=D_MODEL, vmem_mib=56)
    sq = proj(w_col0=0, epilogue=_silu, name="even_in_proj_q")
    lf = proj(w_col0=hw, epilogue=_hgrn_logf, aux=(lb_logits,), name="even_in_proj_f")
    v = proj(w_col0=2 * hw, out_dtype=BF16, name="even_in_proj_i")
    sg = proj(w_col0=3 * hw, epilogue=_silu, name="even_in_proj_ga")
    o_a = _hgrn(sq, lf, v, sg, a_norm_w, bsz, seq)
    o_b = _lru_proj(hn, w_in, hg, hg + lru, conv_w, conv_b, wa, ba, wx, bx, lam, seq)
    return _matmul([o_a, o_b], w_out.astype(BF16), residual=h, n=D_MODEL, tm=512, tn=1024, tk=D_MODEL,
                   vmem_mib=48, name="even_out_proj")


def _odd_layer(h, norm_w, w_in, conv_w, conv_b, dt_bias, a_log, d_skip, ssd_norm_w, w_out, bsz, seq):
    hn = _rmsnorm(h, norm_w, BF16)
    w_in = w_in.astype(BF16)
    sz = _matmul([hn], w_in, n=SSD_INNER, tm=1024, tn=1024, tk=D_MODEL, vmem_mib=56, name="odd_in_proj_z",
                 epilogue=_silu)
    xbc = _matmul([hn], w_in, n=SSD_CONV_DIM, w_col0=SSD_INNER, tm=1024, tn=1024, tk=D_MODEL, vmem_mib=48,
                  name="odd_in_proj_xbc")
    dt = _matmul([hn], w_in, n=SSD_HEADS, w_col0=SSD_INNER + SSD_CONV_DIM, tm=1024, tn=SSD_HEADS, tk=D_MODEL,
                 vmem_mib=32, name="odd_in_proj_dt")
    y = _ssd(sz, xbc, dt, conv_w, conv_b, dt_bias, a_log, d_skip, ssd_norm_w, bsz, seq)
    return _matmul([y], w_out.astype(BF16), residual=h, n=D_MODEL, tm=1024, tn=1024, tk=D_MODEL, vmem_mib=56,
                   name="odd_out_proj")


def kernel(x, norm_w, e_w_in, lb_logits, e_a_norm_w, e_conv_w, e_conv_b, e_wa, e_ba, e_wx, e_bx, e_lambda,
           e_w_out, o_w_in, o_conv_w, o_conv_b, o_dt_bias, o_a_log, o_d, o_norm_w, o_w_out, final_norm_w):
    bsz, seq, d = x.shape
    h = x.reshape(bsz * seq, d)
    h = _even_layer(h, norm_w[0], e_w_in[0], lb_logits, e_a_norm_w[0], e_conv_w[0], e_conv_b[0], e_wa[0],
                    e_ba[0], e_wx[0], e_bx[0], e_lambda[0], e_w_out[0], bsz, seq)
    h = _odd_layer(h, norm_w[1], o_w_in[0], o_conv_w[0], o_conv_b[0], o_dt_bias[0], o_a_log[0], o_d[0],
                   o_norm_w[0], o_w_out[0], bsz, seq)
    return _rmsnorm(h, final_norm_w, x.dtype).reshape(bsz, seq, d)
```

```python
import functools

import jax
import jax.numpy as jnp
from jax import lax
from jax.experimental import pallas as pl
from jax.experimental.pallas import tpu as pltpu

F32 = jnp.float32
BF16 = jnp.bfloat16

EPS = 1e-6
CHUNK = 64
LANES = 128
SUBLANES = 8
D_MODEL = 4096
HG_HEADS = 32
HG_DK = 128
LRU_BLOCKS = 32
LRU_BLOCK = 128
LRU_C = 8.0
CONV_W = 4
SSD_INNER = 8192
SSD_HEADS = 128
SSD_HEADDIM = 64
SSD_GROUPS = 8
SSD_STATE = 128
SSD_BC = SSD_GROUPS * SSD_STATE
SSD_CONV_DIM = SSD_INNER + 2 * SSD_BC
PAIRS_PER_GROUP = SSD_HEADS // SSD_GROUPS // 2
MIB = 1024 * 1024


def _cparams(semantics, vmem_mib):
    return pltpu.CompilerParams(dimension_semantics=semantics, vmem_limit_bytes=vmem_mib * MIB)


def _sigmoid(x):
    return 0.5 + 0.5 * jnp.tanh(0.5 * x)


def _silu(x):
    return x * _sigmoid(x)


def _softplus(x):
    return jnp.maximum(x, 0.0) + jnp.log1p(jnp.exp(-jnp.abs(x)))


def _chunk_cumsum(tri, x, terms):
    acc = None
    rem = x
    for t in range(terms):
        piece = rem.astype(BF16)
        part = jnp.dot(tri, piece, preferred_element_type=F32)
        acc = part if acc is None else acc + part
        if t + 1 < terms:
            rem = rem - piece.astype(F32)
    return acc


def _causal_conv_rows(hist, x_ref, cols, n_rows, w, b):
    row = lax.broadcasted_iota(jnp.int32, hist.shape, 0)
    shifts = range(1, CONV_W)
    prev_rot = [pltpu.roll(hist, s, 0) for s in shifts]
    for r0 in range(0, n_rows, SUBLANES):
        cur = x_ref[r0:r0 + SUBLANES, cols]
        cur_rot = [pltpu.roll(cur, s, 0) for s in shifts]
        out = b + cur * w[CONV_W - 1:CONV_W]
        for s, c_rot, p_rot in zip(shifts, cur_rot, prev_rot):
            out = out + jnp.where(row >= s, c_rot, p_rot) * w[CONV_W - 1 - s:CONV_W - s]
        prev_rot = cur_rot
        yield r0, out


NORM_LANES = 512


def _gated_rmsnorm_rows(y_ref, g_ref, w_ref, o_ref, rows, gate_first):
    width = y_ref.shape[1]
    chunks = [slice(c0, c0 + NORM_LANES) for c0 in range(0, width, NORM_LANES)]
    ssq = None
    for cs in chunks:
        v = y_ref[rows, cs] * g_ref[rows, cs] if gate_first else y_ref[rows, cs]
        ssq = v * v if ssq is None else ssq + v * v
    scale = lax.rsqrt(jnp.sum(ssq, axis=-1, keepdims=True) * (1.0 / width) + EPS)
    for cs in chunks:
        v = y_ref[rows, cs] * g_ref[rows, cs]
        o_ref[rows, cs] = (v * scale * w_ref[:, cs]).astype(o_ref.dtype)


def _tri(n):
    r = lax.broadcasted_iota(jnp.int32, (n, n), 0)
    c = lax.broadcasted_iota(jnp.int32, (n, n), 1)
    return r >= c


def _norm_kernel(x_ref, w_ref, o_ref):
    x = x_ref[...]
    ms = jnp.mean(x * x, axis=-1, keepdims=True)
    o_ref[...] = (x * lax.rsqrt(ms + EPS) * w_ref[...]).astype(o_ref.dtype)


def _rmsnorm(x, w, out_dtype, tm=256):
    m, d = x.shape
    return pl.pallas_call(
        _norm_kernel,
        grid=(m // tm,),
        in_specs=[pl.BlockSpec((tm, d), lambda i: (i, 0)),
                  pl.BlockSpec((1, d), lambda i: (0, 0))],
        out_specs=pl.BlockSpec((tm, d), lambda i: (i, 0)),
        out_shape=jax.ShapeDtypeStruct((m, d), out_dtype),
        compiler_params=_cparams(("parallel",), 32),
        name="rmsnorm",
    )(x, w.reshape(1, d))


def _matmul_kernel(*refs, k_ranges, has_res, n_aux, epilogue):
    n_a = len(k_ranges)
    a_refs = refs[:n_a]
    b_ref = refs[n_a]
    r_ref = refs[n_a + 1] if has_res else None
    aux_refs = refs[len(refs) - 1 - n_aux:len(refs) - 1]
    o_ref = refs[-1]
    nk = k_ranges[-1][1]
    if nk == 1:
        p = jnp.dot(a_refs[0][...], b_ref[...], preferred_element_type=F32)
        if has_res:
            p = p + r_ref[...]
        if epilogue is not None:
            p = epilogue(p, *[r[...] for r in aux_refs])
        o_ref[...] = p.astype(o_ref.dtype)
        return
    k = pl.program_id(2)

    def first_step(a_ref):
        p = jnp.dot(a_ref[...], b_ref[...], preferred_element_type=F32)
        o_ref[...] = p + r_ref[...] if has_res else p

    def later_step(a_ref):
        o_ref[...] += jnp.dot(a_ref[...], b_ref[...], preferred_element_type=F32)

    for a_ref, (k0, k1) in zip(a_refs, k_ranges):
        if k0 == 0:
            pl.when(k == 0)(functools.partial(first_step, a_ref))
            k0 = 1
        if k1 > k0:
            pl.when((k >= k0) & (k < k1))(functools.partial(later_step, a_ref))


def _matmul(a_list, w, residual=None, *, n, w_col0=0, tm, tn, tk, vmem_mib, name, epilogue=None, aux=(),
            out_dtype=F32):
    m = a_list[0].shape[0]
    k_ranges, k0 = [], 0
    for a in a_list:
        k_ranges.append((k0, k0 + a.shape[1] // tk))
        k0 = k_ranges[-1][1]
    nk = k0
    assert nk * tk == w.shape[0] and w_col0 % tn == 0
    assert nk == 1 or (epilogue is None and out_dtype == F32)
    in_specs = [pl.BlockSpec((tm, tk), lambda i, j, k, k0=k0, k1=k1: (i, jnp.clip(k - k0, 0, k1 - k0 - 1)))
                for k0, k1 in k_ranges]
    in_specs.append(pl.BlockSpec((tk, tn), lambda i, j, k: (k, w_col0 // tn + j)))
    args = list(a_list) + [w]
    if residual is not None:
        in_specs.append(pl.BlockSpec((tm, tn), lambda i, j, k: (i, j)))
        args.append(residual)
    for x in aux:
        in_specs.append(pl.BlockSpec((x.shape[0], tn), lambda i, j, k: (0, j)))
        args.append(x)
    return pl.pallas_call(
        functools.partial(_matmul_kernel, k_ranges=tuple(k_ranges), has_res=residual is not None,
                          n_aux=len(aux), epilogue=epilogue),
        grid=(m // tm, n // tn, nk),
        in_specs=in_specs,
        out_specs=pl.BlockSpec((tm, tn), lambda i, j, k: (i, j)),
        out_shape=jax.ShapeDtypeStruct((m, n), out_dtype),
        compiler_params=_cparams(("parallel", "parallel", "arbitrary"), vmem_mib),
        name=name,
    )(*args)


HG_SLAB = 512
ROW_SLAB = 16


def _hgrn_logf(f_pre, lb_logits):
    e = jnp.exp(lb_logits - jnp.max(lb_logits, axis=0, keepdims=True))
    lb = e[0:1] / jnp.sum(e, axis=0, keepdims=True)
    half_k = 0.5 * (1.0 - lb)
    return jnp.log(1.0 - (half_k - half_k * jnp.tanh(0.5 * f_pre)))


def _hgrn_kernel(sq_ref, lf_ref, v_ref, sg_ref, nw_ref, o_ref,
                 st_ref, acc_ref, qin_ref, kin_ref, qout_ref, kst_ref, gl_ref, *, tb):
    @pl.when(pl.program_id(1) == 0)
    def _():
        st_ref[...] = jnp.zeros_like(st_ref)

    n_chunks = tb // CHUNK
    n_slabs = (HG_HEADS * HG_DK) // HG_SLAB
    heads_per_slab = HG_SLAB // HG_DK
    tri = _tri(CHUNK)
    tri_bf = tri.astype(BF16)
    nt = (((1,), (1,)), ((), ()))
    tn = (((0,), (0,)), ((), ()))

    def phase_a(j, p):
        cols = pl.ds(pl.multiple_of(j * HG_SLAB, HG_SLAB), HG_SLAB)
        for c in range(n_chunks):
            rows = pl.ds(c * CHUNK, CHUNK)
            logf = lf_ref[rows, cols]
            k = 1.0 - jnp.exp(logf)
            b = _chunk_cumsum(tri_bf, logf, terms=2)
            ref = b[CHUNK // 2 - 1:CHUNK // 2]
            blast = b[CHUNK - 1:CHUNK]
            d = b - ref
            q_in = sq_ref[rows, cols] * jnp.exp(d)
            k_in = k * jnp.exp(-d)
            qin_ref[p, rows, :] = q_in.astype(BF16)
            kin_ref[p, rows, :] = k_in.astype(BF16)
            qout_ref[p, rows, :] = (q_in * jnp.exp(ref)).astype(BF16)
            kst_ref[p, rows, :] = (k_in * jnp.exp(blast - ref)).astype(BF16)
            gl_ref[p, c:c + 1, :] = jnp.exp(blast)

    def phase_b(j, p):
        states = [st_ref[j * heads_per_slab + hh] for hh in range(heads_per_slab)]
        for hh in range(heads_per_slab):
            lanes = slice(hh * HG_DK, (hh + 1) * HG_DK)
            out_cols = pl.ds(pl.multiple_of((j * heads_per_slab + hh) * HG_DK, HG_DK), HG_DK)
            s_t = states[hh]
            for c in range(n_chunks):
                rows = pl.ds(c * CHUNK, CHUNK)
                v = v_ref[rows, out_cols]
                scores = lax.dot_general(qin_ref[p, rows, lanes], kin_ref[p, rows, lanes], nt,
                                         preferred_element_type=F32)
                scores = jnp.where(tri, scores, 0.0).astype(BF16)
                lhs = jnp.concatenate([qout_ref[p, rows, lanes], scores], axis=1)
                rhs = jnp.concatenate([s_t.astype(BF16).T, v], axis=0)
                acc_ref[rows, out_cols] = jnp.dot(lhs, rhs, preferred_element_type=F32)
                s_t = (s_t * gl_ref[p, c:c + 1, lanes]
                       + lax.dot_general(v, kst_ref[p, rows, lanes], tn, preferred_element_type=F32))
            states[hh] = s_t
        for hh in range(heads_per_slab):
            st_ref[j * heads_per_slab + hh] = states[hh]

    phase_a(0, 0)

    def skew_body(jj, carry):
        j = 2 * jj
        phase_a(j + 1, 1)
        phase_b(j, 0)
        phase_a(j + 2, 0)
        phase_b(j + 1, 1)
        return carry

    lax.fori_loop(0, n_slabs // 2 - 1, skew_body, 0)
    phase_a(n_slabs - 1, 1)
    phase_b(n_slabs - 2, 0)
    phase_b(n_slabs - 1, 1)

    def norm_body(r, carry):
        rows = pl.ds(pl.multiple_of(r * ROW_SLAB, ROW_SLAB), ROW_SLAB)
        _gated_rmsnorm_rows(acc_ref, sg_ref, nw_ref, o_ref, rows, gate_first=False)
        return carry

    lax.fori_loop(0, tb // ROW_SLAB, norm_body, 0)


def _hgrn(sq, lf, v, sg, a_norm_w, bsz, seq, tb=128):
    width = HG_HEADS * HG_DK
    nb = seq // tb
    rows = pl.BlockSpec((tb, width), lambda b, s: (b * nb + s, 0))
    return pl.pallas_call(
        functools.partial(_hgrn_kernel, tb=tb),
        grid=(bsz, nb),
        in_specs=[rows, rows, rows, rows, pl.BlockSpec((1, width), lambda b, s: (0, 0))],
        out_specs=rows,
        out_shape=jax.ShapeDtypeStruct((bsz * seq, width), BF16),
        scratch_shapes=[pltpu.VMEM((HG_HEADS, HG_DK, HG_DK), F32),
                        pltpu.VMEM((tb, width), F32)]
                       + [pltpu.VMEM((2, tb, HG_SLAB), BF16)] * 4
                       + [pltpu.VMEM((2, tb // CHUNK, HG_SLAB), F32)],
        compiler_params=_cparams(("parallel", "arbitrary"), 40),
        name="hgrn2",
    )(sq, lf, v, sg, a_norm_w.reshape(1, width))


LRU_TILE = 512


def _lru_block(xc, gate_in, wg, ba, bx, lam, h, row):
    gates = jnp.dot(xc.astype(BF16), wg, preferred_element_type=F32)
    quarter = (0.25 * LRU_C) * _softplus(-lam)
    t_r = jnp.tanh(0.5 * (gates[:, :LRU_BLOCK] + ba))
    tau = jnp.tanh(-quarter - quarter * t_r)
    rho = 1.0 / (1.0 - tau)
    a = (1.0 + tau) * rho
    neg_tau = -tau
    root = jnp.where(neg_tau > 0.0, neg_tau * lax.rsqrt(neg_tau), 0.0)
    ig = _sigmoid(gates[:, LRU_BLOCK:] + bx)
    u = (2.0 * rho * root) * (ig * xc)
    outs = []
    for r8 in range(xc.shape[0] // SUBLANES):
        a8 = a[r8 * SUBLANES:(r8 + 1) * SUBLANES]
        u8 = u[r8 * SUBLANES:(r8 + 1) * SUBLANES]
        for sh in (1, 2, 4):
            keep = row >= sh
            a_s = jnp.where(keep, pltpu.roll(a8, sh, 0), 1.0)
            u_s = jnp.where(keep, pltpu.roll(u8, sh, 0), 0.0)
            u8 = a8 * u_s + u8
            a8 = a8 * a_s
        h8 = a8 * h + u8
        h = h8[SUBLANES - 1:SUBLANES]
        outs.append(h8)
    return jnp.concatenate(outs, axis=0) * _silu(gate_in), h


def _lru_proj_kernel(a_ref, wxb_ref, wgb_ref, cw_ref, cb_ref, wg_ref, bg_ref, lam_ref, o_ref,
                     p_ref, hist_ref, h_ref, *, tm, tiles_per_seq):
    i = pl.program_id(0)
    j = pl.program_id(1)

    @pl.when(lax.rem(i, tiles_per_seq) == 0)
    def _():
        hist_ref[j] = jnp.zeros(hist_ref.shape[1:], F32)
        h_ref[j] = jnp.zeros(h_ref.shape[1:], F32)

    a = a_ref[...]
    p_ref[:, 0:LRU_TILE] = jnp.dot(a, wxb_ref[...], preferred_element_type=F32)
    p_ref[:, LRU_TILE:2 * LRU_TILE] = jnp.dot(a, wgb_ref[...], preferred_element_type=F32)
    row = lax.broadcasted_iota(jnp.int32, (SUBLANES, LRU_BLOCK), 0)
    hist = hist_ref[j]
    h_in = h_ref[j]
    h_out = []
    for blk in range(LRU_TILE // LRU_BLOCK):
        lanes = slice(blk * LRU_BLOCK, (blk + 1) * LRU_BLOCK)
        n = j * (LRU_TILE // LRU_BLOCK) + blk
        cols = pl.ds(pl.multiple_of(n * LRU_BLOCK, LRU_BLOCK), LRU_BLOCK)
        xc = jnp.concatenate(
            [acc for _, acc in _causal_conv_rows(hist[:, lanes], p_ref, lanes, tm, cw_ref[:, cols], cb_ref[:, cols])],
            axis=0)
        out, h_last = _lru_block(xc, p_ref[:, LRU_TILE + blk * LRU_BLOCK:LRU_TILE + (blk + 1) * LRU_BLOCK],
                                 wg_ref[n], bg_ref[0:1, cols], bg_ref[1:2, cols], lam_ref[:, cols],
                                 h_in[:, lanes], row)
        o_ref[:, lanes] = out.astype(o_ref.dtype)
        h_out.append(h_last)
    hist_ref[j] = p_ref[tm - SUBLANES:tm, 0:LRU_TILE]
    h_ref[j] = jnp.concatenate(h_out, axis=1)


def _lru_proj(hn, w, xb_col0, gb_col0, conv_w, conv_b, wa, ba, wx, bx, lam, seq, tm=1024):
    m, k = hn.shape
    width = LRU_BLOCKS * LRU_BLOCK
    n_tiles = width // LRU_TILE
    assert xb_col0 % LRU_TILE == 0 and gb_col0 % LRU_TILE == 0
    assert seq % tm == 0, "row tiles must not straddle sequences"
    wg = jnp.concatenate([wa, wx], axis=-1).astype(BF16)
    bg = jnp.stack([ba, bx], axis=0)

    def full(shape):
        return pl.BlockSpec(shape, lambda i, j: (0,) * len(shape))

    return pl.pallas_call(
        functools.partial(_lru_proj_kernel, tm=tm, tiles_per_seq=seq // tm),
        grid=(m // tm, n_tiles),
        in_specs=[pl.BlockSpec((tm, k), lambda i, j: (i, 0)),
                  pl.BlockSpec((k, LRU_TILE), lambda i, j: (0, xb_col0 // LRU_TILE + j)),
                  pl.BlockSpec((k, LRU_TILE), lambda i, j: (0, gb_col0 // LRU_TILE + j)),
                  full(conv_w.shape), full((1, width)), full(wg.shape), full(bg.shape), full((1, width))],
        out_specs=pl.BlockSpec((tm, LRU_TILE), lambda i, j: (i, j)),
        out_shape=jax.ShapeDtypeStruct((m, width), BF16),
        scratch_shapes=[pltpu.VMEM((tm, 2 * LRU_TILE), F32),
                        pltpu.VMEM((n_tiles, SUBLANES, LRU_TILE), F32),
                        pltpu.VMEM((n_tiles, 1, LRU_TILE), F32)],
        compiler_params=_cparams(("arbitrary", "arbitrary"), 48),
        name="even_in_proj_rglru",
    )(hn, w, w, conv_w, conv_b.reshape(1, width), wg, bg, lam.reshape(1, width))


def _ssd_kernel(sz_ref, xbc_ref, dt_ref, cw_ref, cb_ref, dtb_ref, alog_ref, dsk_ref, nw_ref, pidx_ref, o_ref,
                hist_ref, xc_ref, st_ref, st2_ref, y_ref, tt_ref, cs_ref, cbd_ref, bt_ref, e_ref, *, tb):
    n_chunks = tb // CHUNK

    @pl.when(pl.program_id(1) == 0)
    def _():
        hist_ref[...] = jnp.zeros_like(hist_ref)
        st_ref[...] = jnp.zeros_like(st_ref)

    tri_bf = _tri(CHUNK).astype(BF16)
    lane = lax.broadcasted_iota(jnp.int32, (CHUNK, LANES), 1)
    rowi = lax.broadcasted_iota(jnp.int32, (CHUNK, LANES), 0)
    lane_hi = (lane >= SSD_HEADDIM).astype(jnp.int32)
    causal2 = rowi >= (lane - SSD_HEADDIM * lane_hi)
    lane1_lo = lax.broadcasted_iota(jnp.int32, (1, LANES), 1) < SSD_HEADDIM
    lane_lo = lane < SSD_HEADDIM
    nt = (((1,), (1,)), ((), ()))
    a_neg = -jnp.exp(alog_ref[...])

    for c in range(n_chunks):
        dt = _softplus(dt_ref[c * CHUNK:(c + 1) * CHUNK, :] + dtb_ref[...])
        cs = _chunk_cumsum(tri_bf, dt * a_neg, terms=3)
        w = dt * jnp.exp(cs[CHUNK - 1:CHUNK] - cs)
        cs_ref[c] = cs
        for t, arr in enumerate((cs - jnp.log(dt), w)):
            tt_ref[c, t] = jnp.concatenate([arr, arr], axis=0).T

    def expand_group(c, g):
        cs = cs_ref[c]
        for i in range(PAIRS_PER_GROUP):
            pair = g * PAIRS_PER_GROUP + i
            idx = jnp.broadcast_to(pidx_ref[pl.ds(pair, 1), :], (CHUNK, LANES))
            e_ref[c, :, pl.ds(pl.multiple_of(pair * LANES, LANES), LANES)] = jnp.take_along_axis(
                cs, idx, axis=1, mode="promise_in_bounds")

    slab = 512
    n_expand = n_chunks * SSD_GROUPS
    assert SSD_CONV_DIM // slab >= n_expand

    def conv_body(j, carry):
        job = lax.rem(j, n_expand)
        expand_group(job // SSD_GROUPS, lax.rem(job, SSD_GROUPS))
        cols = pl.ds(pl.multiple_of(j * slab, slab), slab)
        for r0, acc in _causal_conv_rows(hist_ref[:, cols], xbc_ref, cols, tb, cw_ref[:, cols], cb_ref[:, cols]):
            xc_ref[r0:r0 + SUBLANES, cols] = _silu(acc)
        return carry

    lax.fori_loop(0, SSD_CONV_DIM // slab, conv_body, 0)

    for c in range(n_chunks):
        for g in range(SSD_GROUPS):
            b_g = xc_ref[c * CHUNK:(c + 1) * CHUNK, SSD_INNER + g * SSD_STATE:SSD_INNER + (g + 1) * SSD_STATE]
            c_g = xc_ref[c * CHUNK:(c + 1) * CHUNK,
                         SSD_INNER + SSD_BC + g * SSD_STATE:SSD_INNER + SSD_BC + (g + 1) * SSD_STATE]
            b2 = jnp.concatenate([b_g, b_g], axis=0)
            cbd_ref[c, g] = lax.dot_general(c_g.astype(BF16), b2.astype(BF16), nt,
                                            preferred_element_type=F32)
            bt_ref[c, g] = b2.T

    for c in range(n_chunks):
        r0 = c * CHUNK
        st_in, st_out = (st_ref, st2_ref) if c % 2 == 0 else (st2_ref, st_ref)

        def group_body(g, carry, c=c, r0=r0, st_in=st_in, st_out=st_out):
            c_g = xc_ref[r0:r0 + CHUNK,
                         pl.ds(pl.multiple_of(SSD_INNER + SSD_BC + g * SSD_STATE, SSD_STATE), SSD_STATE)]
            c_bf = c_g.astype(BF16)
            cb2 = cbd_ref[c, g]
            bt2 = bt_ref[c, g]
            for i in range(PAIRS_PER_GROUP):
                h_a = g * (2 * PAIRS_PER_GROUP) + 2 * i
                cols = pl.ds(pl.multiple_of(h_a * SSD_HEADDIM, LANES), LANES)

                def pair_row(t, h_a=h_a):
                    return jnp.where(lane1_lo, tt_ref[c, t, pl.ds(h_a, 1), :], tt_ref[c, t, pl.ds(h_a + 1, 1), :])

                e2 = e_ref[c, :, cols]
                att = cb2 * jnp.where(causal2, jnp.exp(e2 - pair_row(0)), 0.0)
                lhs = jnp.concatenate([att, bt2 * pair_row(1)], axis=0).astype(BF16)
                x2 = xc_ref[r0:r0 + CHUNK, cols]
                rhs = jnp.concatenate([jnp.where(lane_lo, x2, 0.0), jnp.where(lane_lo, 0.0, x2)],
                                      axis=0).astype(BF16)
                res = jnp.dot(lhs, rhs, preferred_element_type=F32)
                s_p = st_in[:, cols]
                e_out = jnp.exp(e2)
                y_off = e_out * jnp.dot(c_bf, s_p.astype(BF16), preferred_element_type=F32)
                y_ref[r0:r0 + CHUNK, cols] = res[0:CHUNK] + y_off + x2 * dsk_ref[:, cols]
                st_out[:, cols] = s_p * e_out[CHUNK - 1:CHUNK] + res[CHUNK:3 * CHUNK]
            return carry

        lax.fori_loop(0, SSD_GROUPS, group_body, 0)

    hist_ref[...] = xbc_ref[tb - SUBLANES:tb, :]

    def norm_body(r, carry):
        rows = pl.ds(pl.multiple_of(r * ROW_SLAB, ROW_SLAB), ROW_SLAB)
        _gated_rmsnorm_rows(y_ref, sz_ref, nw_ref, o_ref, rows, gate_first=True)
        return carry

    lax.fori_loop(0, tb // ROW_SLAB, norm_body, 0)


def _ssd(sz, xbc, dt, conv_w, conv_b, dt_bias, a_log, d_skip, norm_w, bsz, seq, tb=128):
    assert (tb // CHUNK) % 2 == 0, "the state ping-pong needs an even number of chunks per block"
    nb = seq // tb
    dsk = jnp.repeat(d_skip, SSD_HEADDIM).reshape(1, SSD_INNER)
    pair_idx = (2 * jnp.arange(SSD_HEADS // 2, dtype=jnp.int32)[:, None]
                + (jnp.arange(LANES, dtype=jnp.int32) // SSD_HEADDIM)[None, :])

    def rows(width):
        return pl.BlockSpec((tb, width), lambda b, s: (b * nb + s, 0))

    def full(shape):
        return pl.BlockSpec(shape, lambda b, s: (0,) * len(shape))

    return pl.pallas_call(
        functools.partial(_ssd_kernel, tb=tb),
        grid=(bsz, nb),
        in_specs=[rows(SSD_INNER), rows(SSD_CONV_DIM), rows(SSD_HEADS),
                  full(conv_w.shape), full((1, SSD_CONV_DIM)),
                  full((1, SSD_HEADS)), full((1, SSD_HEADS)), full((1, SSD_INNER)), full((1, SSD_INNER)),
                  full((SSD_HEADS // 2, LANES))],
        out_specs=rows(SSD_INNER),
        out_shape=jax.ShapeDtypeStruct((bsz * seq, SSD_INNER), BF16),
        scratch_shapes=[pltpu.VMEM((SUBLANES, SSD_CONV_DIM), F32),
                        pltpu.VMEM((tb, SSD_CONV_DIM), F32),
                        pltpu.VMEM((SSD_STATE, SSD_INNER), F32),
                        pltpu.VMEM((SSD_STATE, SSD_INNER), F32),
                        pltpu.VMEM((tb, SSD_INNER), F32),
                        pltpu.VMEM((tb // CHUNK, 2, SSD_HEADS, LANES), F32),
                        pltpu.VMEM((tb // CHUNK, CHUNK, SSD_HEADS), F32),
                        pltpu.VMEM((tb // CHUNK, SSD_GROUPS, CHUNK, LANES), F32),
                        pltpu.VMEM((tb // CHUNK, SSD_GROUPS, SSD_STATE, LANES), F32),
                        pltpu.VMEM((tb // CHUNK, CHUNK, SSD_INNER), F32)],
        compiler_params=_cparams(("parallel", "arbitrary"), 56),
        name="ssd",
    )(sz, xbc, dt, conv_w, conv_b.reshape(1, SSD_CONV_DIM), dt_bias.reshape(1, SSD_HEADS),
      a_log.reshape(1, SSD_HEADS), dsk, norm_w.reshape(1, SSD_INNER), pair_idx)


def _even_layer(h, norm_w, w_in, lb_logits, a_norm_w, conv_w, conv_b, wa, ba, wx, bx, lam, w_out, bsz, seq):
    hn = _rmsnorm(h, norm_w, BF16)
    hg = 4 * HG_HEADS * HG_DK
    lru = LRU_BLOCKS * LRU_BLOCK
    w_in = w_in.astype(BF16)
    hw = HG_HEADS * HG_DK
    proj = functools.partial(_matmul, [hn], w_in, n=hw, tm=1024, tn=1024, tk=D_MODEL, vmem_mib=56)
    sq = proj(w_col0=0, epilogue=_silu, name="even_in_proj_q")
    lf = proj(w_col0=hw, epilogue=_hgrn_logf, aux=(lb_logits,), name="even_in_proj_f")
    v = proj(w_col0=2 * hw, out_dtype=BF16, name="even_in_proj_i")
    sg = proj(w_col0=3 * hw, epilogue=_silu, name="even_in_proj_ga")
    o_a = _hgrn(sq, lf, v, sg, a_norm_w, bsz, seq)
    o_b = _lru_proj(hn, w_in, hg, hg + lru, conv_w, conv_b, wa, ba, wx, bx, lam, seq)
    return _matmul([o_a, o_b], w_out.astype(BF16), residual=h, n=D_MODEL, tm=512, tn=1024, tk=D_MODEL,
                   vmem_mib=48, name="even_out_proj")


def _odd_layer(h, norm_w, w_in, conv_w, conv_b, dt_bias, a_log, d_skip, ssd_norm_w, w_out, bsz, seq):
    hn = _rmsnorm(h, norm_w, BF16)
    w_in = w_in.astype(BF16)
    sz = _matmul([hn], w_in, n=SSD_INNER, tm=1024, tn=1024, tk=D_MODEL, vmem_mib=56, name="odd_in_proj_z",
                 epilogue=_silu)
    xbc = _matmul([hn], w_in, n=SSD_CONV_DIM, w_col0=SSD_INNER, tm=1024, tn=1024, tk=D_MODEL, vmem_mib=48,
                  name="odd_in_proj_xbc")
    dt = _matmul([hn], w_in, n=SSD_HEADS, w_col0=SSD_INNER + SSD_CONV_DIM, tm=1024, tn=SSD_HEADS, tk=D_MODEL,
                 vmem_mib=32, name="odd_in_proj_dt")
    y = _ssd(sz, xbc, dt, conv_w, conv_b, dt_bias, a_log, d_skip, ssd_norm_w, bsz, seq)
    return _matmul([y], w_out.astype(BF16), residual=h, n=D_MODEL, tm=1024, tn=1024, tk=D_MODEL, vmem_mib=56,
                   name="odd_out_proj")


def kernel(x, norm_w, e_w_in, lb_logits, e_a_norm_w, e_conv_w, e_conv_b, e_wa, e_ba, e_wx, e_bx, e_lambda,
           e_w_out, o_w_in, o_conv_w, o_conv_b, o_dt_bias, o_a_log, o_d, o_norm_w, o_w_out, final_norm_w):
    bsz, seq, d = x.shape
    h = x.reshape(bsz * seq, d)
    h = _even_layer(h, norm_w[0], e_w_in[0], lb_logits, e_a_norm_w[0], e_conv_w[0], e_conv_b[0], e_wa[0],
                    e_ba[0], e_wx[0], e_bx[0], e_lambda[0], e_w_out[0], bsz, seq)
    h = _odd_layer(h, norm_w[1], o_w_in[0], o_conv_w[0], o_conv_b[0], o_dt_bias[0], o_a_log[0], o_d[0],
                   o_norm_w[0], o_w_out[0], bsz, seq)
    return _rmsnorm(h, final_norm_w, x.dtype).reshape(bsz, seq, d)
```

```python
import functools

import jax
import jax.numpy as jnp
from jax import lax
from jax.experimental import pallas as pl
from jax.experimental.pallas import tpu as pltpu

F32 = jnp.float32
BF16 = jnp.bfloat16

EPS = 1e-6
CHUNK = 64
LANES = 128
SUBLANES = 8
D_MODEL = 4096
HG_HEADS = 32
HG_DK = 128
LRU_BLOCKS = 32
LRU_BLOCK = 128
LRU_C = 8.0
CONV_W = 4
SSD_INNER = 8192
SSD_HEADS = 128
SSD_HEADDIM = 64
SSD_GROUPS = 8
SSD_STATE = 128
SSD_BC = SSD_GROUPS * SSD_STATE
SSD_CONV_DIM = SSD_INNER + 2 * SSD_BC
PAIRS_PER_GROUP = SSD_HEADS // SSD_GROUPS // 2
MIB = 1024 * 1024


def _cparams(semantics, vmem_mib):
    return pltpu.CompilerParams(dimension_semantics=semantics, vmem_limit_bytes=vmem_mib * MIB)


def _sigmoid(x):
    return 0.5 + 0.5 * jnp.tanh(0.5 * x)


def _silu(x):
    return x * _sigmoid(x)


def _softplus(x):
    return jnp.maximum(x, 0.0) + jnp.log1p(jnp.exp(-jnp.abs(x)))


def _chunk_cumsum(tri, x, terms):
    acc = None
    rem = x
    for t in range(terms):
        piece = rem.astype(BF16)
        part = jnp.dot(tri, piece, preferred_element_type=F32)
        acc = part if acc is None else acc + part
        if t + 1 < terms:
            rem = rem - piece.astype(F32)
    return acc


def _causal_conv_rows(hist, x_ref, cols, n_rows, w, b):
    row = lax.broadcasted_iota(jnp.int32, hist.shape, 0)
    shifts = range(1, CONV_W)
    prev_rot = [pltpu.roll(hist, s, 0) for s in shifts]
    for r0 in range(0, n_rows, SUBLANES):
        cur = x_ref[r0:r0 + SUBLANES, cols]
        cur_rot = [pltpu.roll(cur, s, 0) for s in shifts]
        out = b + cur * w[CONV_W - 1:CONV_W]
        for s, c_rot, p_rot in zip(shifts, cur_rot, prev_rot):
            out = out + jnp.where(row >= s, c_rot, p_rot) * w[CONV_W - 1 - s:CONV_W - s]
        prev_rot = cur_rot
        yield r0, out


NORM_LANES = 512


def _gated_rmsnorm_rows(y_ref, g_ref, w_ref, o_ref, rows, gate_first):
    width = y_ref.shape[1]
    chunks = [slice(c0, c0 + NORM_LANES) for c0 in range(0, width, NORM_LANES)]
    ssq = None
    for cs in chunks:
        v = y_ref[rows, cs] * g_ref[rows, cs] if gate_first else y_ref[rows, cs]
        ssq = v * v if ssq is None else ssq + v * v
    scale = lax.rsqrt(jnp.sum(ssq, axis=-1, keepdims=True) * (1.0 / width) + EPS)
    for cs in chunks:
        v = y_ref[rows, cs] * g_ref[rows, cs]
        o_ref[rows, cs] = (v * scale * w_ref[:, cs]).astype(o_ref.dtype)


def _tri(n):
    r = lax.broadcasted_iota(jnp.int32, (n, n), 0)
    c = lax.broadcasted_iota(jnp.int32, (n, n), 1)
    return r >= c


def _norm_kernel(x_ref, w_ref, o_ref):
    x = x_ref[...]
    ms = jnp.mean(x * x, axis=-1, keepdims=True)
    o_ref[...] = (x * lax.rsqrt(ms + EPS) * w_ref[...]).astype(o_ref.dtype)


def _rmsnorm(x, w, out_dtype, tm=256):
    m, d = x.shape
    return pl.pallas_call(
        _norm_kernel,
        grid=(m // tm,),
        in_specs=[pl.BlockSpec((tm, d), lambda i: (i, 0)),
                  pl.BlockSpec((1, d), lambda i: (0, 0))],
        out_specs=pl.BlockSpec((tm, d), lambda i: (i, 0)),
        out_shape=jax.ShapeDtypeStruct((m, d), out_dtype),
        compiler_params=_cparams(("parallel",), 32),
        name="rmsnorm",
    )(x, w.reshape(1, d))


def _matmul_kernel(*refs, k_ranges, has_res, n_aux, epilogue):
    n_a = len(k_ranges)
    a_refs = refs[:n_a]
    b_ref = refs[n_a]
    r_ref = refs[n_a + 1] if has_res else None
    aux_refs = refs[len(refs) - 1 - n_aux:len(refs) - 1]
    o_ref = refs[-1]
    nk = k_ranges[-1][1]
    if nk == 1:
        p = jnp.dot(a_refs[0][...], b_ref[...], preferred_element_type=F32)
        if has_res:
            p = p + r_ref[...]
        if epilogue is not None:
            p = epilogue(p, *[r[...] for r in aux_refs])
        o_ref[...] = p.astype(o_ref.dtype)
        return
    k = pl.program_id(2)

    def first_step(a_ref):
        p = jnp.dot(a_ref[...], b_ref[...], preferred_element_type=F32)
        o_ref[...] = p + r_ref[...] if has_res else p

    def later_step(a_ref):
        o_ref[...] += jnp.dot(a_ref[...], b_ref[...], preferred_element_type=F32)

    for a_ref, (k0, k1) in zip(a_refs, k_ranges):
        if k0 == 0:
            pl.when(k == 0)(functools.partial(first_step, a_ref))
            k0 = 1
        if k1 > k0:
            pl.when((k >= k0) & (k < k1))(functools.partial(later_step, a_ref))


def _matmul(a_list, w, residual=None, *, n, w_col0=0, tm, tn, tk, vmem_mib, name, epilogue=None, aux=(),
            out_dtype=F32):
    m = a_list[0].shape[0]
    k_ranges, k0 = [], 0
    for a in a_list:
        k_ranges.append((k0, k0 + a.shape[1] // tk))
        k0 = k_ranges[-1][1]
    nk = k0
    assert nk * tk == w.shape[0] and w_col0 % tn == 0
    assert nk == 1 or (epilogue is None and out_dtype == F32)
    in_specs = [pl.BlockSpec((tm, tk), lambda i, j, k, k0=k0, k1=k1: (i, jnp.clip(k - k0, 0, k1 - k0 - 1)))
                for k0, k1 in k_ranges]
    in_specs.append(pl.BlockSpec((tk, tn), lambda i, j, k: (k, w_col0 // tn + j)))
    args = list(a_list) + [w]
    if residual is not None:
        in_specs.append(pl.BlockSpec((tm, tn), lambda i, j, k: (i, j)))
        args.append(residual)
    for x in aux:
        in_specs.append(pl.BlockSpec((x.shape[0], tn), lambda i, j, k: (0, j)))
        args.append(x)
    return pl.pallas_call(
        functools.partial(_matmul_kernel, k_ranges=tuple(k_ranges), has_res=residual is not None,
                          n_aux=len(aux), epilogue=epilogue),
        grid=(m // tm, n // tn, nk),
        in_specs=in_specs,
        out_specs=pl.BlockSpec((tm, tn), lambda i, j, k: (i, j)),
        out_shape=jax.ShapeDtypeStruct((m, n), out_dtype),
        compiler_params=_cparams(("parallel", "parallel", "arbitrary"), vmem_mib),
        name=name,
    )(*args)


HG_SLAB = 512
ROW_SLAB = 16


def _hgrn_logf(f_pre, lb_logits):
    e = jnp.exp(lb_logits - jnp.max(lb_logits, axis=0, keepdims=True))
    lb = e[0:1] / jnp.sum(e, axis=0, keepdims=True)
    half_k = 0.5 * (1.0 - lb)
    return jnp.log(1.0 - (half_k - half_k * jnp.tanh(0.5 * f_pre)))


def _hgrn_kernel(sq_ref, lf_ref, v_ref, sg_ref, nw_ref, o_ref,
                 st_ref, acc_ref, qin_ref, kin_ref, qout_ref, kst_ref, gl_ref, *, tb):
    @pl.when(pl.program_id(1) == 0)
    def _():
        st_ref[...] = jnp.zeros_like(st_ref)

    n_chunks = tb // CHUNK
    n_slabs = (HG_HEADS * HG_DK) // HG_SLAB
    heads_per_slab = HG_SLAB // HG_DK
    tri = _tri(CHUNK)
    tri_bf = tri.astype(BF16)
    nt = (((1,), (1,)), ((), ()))
    tn = (((0,), (0,)), ((), ()))

    def phase_a(j, p):
        cols = pl.ds(pl.multiple_of(j * HG_SLAB, HG_SLAB), HG_SLAB)
        for c in range(n_chunks):
            rows = pl.ds(c * CHUNK, CHUNK)
            logf = lf_ref[rows, cols]
            k = 1.0 - jnp.exp(logf)
            b = _chunk_cumsum(tri_bf, logf, terms=2)
            ref = b[CHUNK // 2 - 1:CHUNK // 2]
            blast = b[CHUNK - 1:CHUNK]
            d = b - ref
            q_in = sq_ref[rows, cols] * jnp.exp(d)
            k_in = k * jnp.exp(-d)
            qin_ref[p, rows, :] = q_in.astype(BF16)
            kin_ref[p, rows, :] = k_in.astype(BF16)
            qout_ref[p, rows, :] = (q_in * jnp.exp(ref)).astype(BF16)
            kst_ref[p, rows, :] = (k_in * jnp.exp(blast - ref)).astype(BF16)
            gl_ref[p, c:c + 1, :] = jnp.exp(blast)

    def phase_b(j, p):
        states = [st_ref[j * heads_per_slab + hh] for hh in range(heads_per_slab)]
        for hh in range(heads_per_slab):
            lanes = slice(hh * HG_DK, (hh + 1) * HG_DK)
            out_cols = pl.ds(pl.multiple_of((j * heads_per_slab + hh) * HG_DK, HG_DK), HG_DK)
            s_t = states[hh]
            for c in range(n_chunks):
                rows = pl.ds(c * CHUNK, CHUNK)
                v = v_ref[rows, out_cols]
                scores = lax.dot_general(qin_ref[p, rows, lanes], kin_ref[p, rows, lanes], nt,
                                         preferred_element_type=F32)
                scores = jnp.where(tri, scores, 0.0).astype(BF16)
                lhs = jnp.concatenate([qout_ref[p, rows, lanes], scores], axis=1)
                rhs = jnp.concatenate([s_t.astype(BF16).T, v], axis=0)
                acc_ref[rows, out_cols] = jnp.dot(lhs, rhs, preferred_element_type=F32)
                s_t = (s_t * gl_ref[p, c:c + 1, lanes]
                       + lax.dot_general(v, kst_ref[p, rows, lanes], tn, preferred_element_type=F32))
            states[hh] = s_t
        for hh in range(heads_per_slab):
            st_ref[j * heads_per_slab + hh] = states[hh]

    phase_a(0, 0)

    def skew_body(jj, carry):
        j = 2 * jj
        phase_a(j + 1, 1)
        phase_b(j, 0)
        phase_a(j + 2, 0)
        phase_b(j + 1, 1)
        return carry

    lax.fori_loop(0, n_slabs // 2 - 1, skew_body, 0)
    phase_a(n_slabs - 1, 1)
    phase_b(n_slabs - 2, 0)
    phase_b(n_slabs - 1, 1)

    def norm_body(r, carry):
        rows = pl.ds(pl.multiple_of(r * ROW_SLAB, ROW_SLAB), ROW_SLAB)
        _gated_rmsnorm_rows(acc_ref, sg_ref, nw_ref, o_ref, rows, gate_first=False)
        return carry

    lax.fori_loop(0, tb // ROW_SLAB, norm_body, 0)


def _hgrn(sq, lf, v, sg, a_norm_w, bsz, seq, tb=128):
    width = HG_HEADS * HG_DK
    nb = seq // tb
    rows = pl.BlockSpec((tb, width), lambda b, s: (b * nb + s, 0))
    return pl.pallas_call(
        functools.partial(_hgrn_kernel, tb=tb),
        grid=(bsz, nb),
        in_specs=[rows, rows, rows, rows, pl.BlockSpec((1, width), lambda b, s: (0, 0))],
        out_specs=rows,
        out_shape=jax.ShapeDtypeStruct((bsz * seq, width), BF16),
        scratch_shapes=[pltpu.VMEM((HG_HEADS, HG_DK, HG_DK), F32),
                        pltpu.VMEM((tb, width), F32)]
                       + [pltpu.VMEM((2, tb, HG_SLAB), BF16)] * 4
                       + [pltpu.VMEM((2, tb // CHUNK, HG_SLAB), F32)],
        compiler_params=_cparams(("parallel", "arbitrary"), 40),
        name="hgrn2",
    )(sq, lf, v, sg, a_norm_w.reshape(1, width))


LRU_TILE = 512


def _lru_block(xc, gate_in, wg, ba, bx, lam, h, row):
    gates = jnp.dot(xc.astype(BF16), wg, preferred_element_type=F32)
    quarter = (0.25 * LRU_C) * _softplus(-lam)
    t_r = jnp.tanh(0.5 * (gates[:, :LRU_BLOCK] + ba))
    tau = jnp.tanh(-quarter - quarter * t_r)
    rho = 1.0 / (1.0 - tau)
    a = (1.0 + tau) * rho
    neg_tau = -tau
    root = jnp.where(neg_tau > 0.0, neg_tau * lax.rsqrt(neg_tau), 0.0)
    t_i = jnp.tanh(0.5 * (gates[:, LRU_BLOCK:] + bx))
    u = (rho * root) * ((1.0 + t_i) * xc)
    outs = []
    for r8 in range(xc.shape[0] // SUBLANES):
        a8 = a[r8 * SUBLANES:(r8 + 1) * SUBLANES]
        u8 = u[r8 * SUBLANES:(r8 + 1) * SUBLANES]
        for sh in (1, 2, 4):
            keep = row >= sh
            a_s = jnp.where(keep, pltpu.roll(a8, sh, 0), 1.0)
            u_s = jnp.where(keep, pltpu.roll(u8, sh, 0), 0.0)
            u8 = a8 * u_s + u8
            a8 = a8 * a_s
        h8 = a8 * h + u8
        h = h8[SUBLANES - 1:SUBLANES]
        outs.append(h8)
    half_g = 0.5 * gate_in
    return (jnp.concatenate(outs, axis=0) * half_g) * (1.0 + jnp.tanh(half_g)), h


def _lru_proj_kernel(a_ref, wxb_ref, wgb_ref, cw_ref, cb_ref, wg_ref, bg_ref, lam_ref, o_ref,
                     p_ref, hist_ref, h_ref, *, tm, tiles_per_seq):
    i = pl.program_id(0)
    j = pl.program_id(1)

    @pl.when(lax.rem(i, tiles_per_seq) == 0)
    def _():
        hist_ref[j] = jnp.zeros(hist_ref.shape[1:], F32)
        h_ref[j] = jnp.zeros(h_ref.shape[1:], F32)

    a = a_ref[...]
    p_ref[:, 0:LRU_TILE] = jnp.dot(a, wxb_ref[...], preferred_element_type=F32)
    p_ref[:, LRU_TILE:2 * LRU_TILE] = jnp.dot(a, wgb_ref[...], preferred_element_type=F32)
    row = lax.broadcasted_iota(jnp.int32, (SUBLANES, LRU_BLOCK), 0)
    hist = hist_ref[j]
    h_in = h_ref[j]
    h_out = []
    for blk in range(LRU_TILE // LRU_BLOCK):
        lanes = slice(blk * LRU_BLOCK, (blk + 1) * LRU_BLOCK)
        n = j * (LRU_TILE // LRU_BLOCK) + blk
        cols = pl.ds(pl.multiple_of(n * LRU_BLOCK, LRU_BLOCK), LRU_BLOCK)
        xc = jnp.concatenate(
            [acc for _, acc in _causal_conv_rows(hist[:, lanes], p_ref, lanes, tm, cw_ref[:, cols], cb_ref[:, cols])],
            axis=0)
        out, h_last = _lru_block(xc, p_ref[:, LRU_TILE + blk * LRU_BLOCK:LRU_TILE + (blk + 1) * LRU_BLOCK],
                                 wg_ref[n], bg_ref[0:1, cols], bg_ref[1:2, cols], lam_ref[:, cols],
                                 h_in[:, lanes], row)
        o_ref[:, lanes] = out.astype(o_ref.dtype)
        h_out.append(h_last)
    hist_ref[j] = p_ref[tm - SUBLANES:tm, 0:LRU_TILE]
    h_ref[j] = jnp.concatenate(h_out, axis=1)


def _lru_proj(hn, w, xb_col0, gb_col0, conv_w, conv_b, wa, ba, wx, bx, lam, seq, tm=1024):
    m, k = hn.shape
    width = LRU_BLOCKS * LRU_BLOCK
    n_tiles = width // LRU_TILE
    assert xb_col0 % LRU_TILE == 0 and gb_col0 % LRU_TILE == 0
    assert seq % tm == 0, "row tiles must not straddle sequences"
    wg = jnp.concatenate([wa, wx], axis=-1).astype(BF16)
    bg = jnp.stack([ba, bx], axis=0)

    def full(shape):
        return pl.BlockSpec(shape, lambda i, j: (0,) * len(shape))

    return pl.pallas_call(
        functools.partial(_lru_proj_kernel, tm=tm, tiles_per_seq=seq // tm),
        grid=(m // tm, n_tiles),
        in_specs=[pl.BlockSpec((tm, k), lambda i, j: (i, 0)),
                  pl.BlockSpec((k, LRU_TILE), lambda i, j: (0, xb_col0 // LRU_TILE + j)),
                  pl.BlockSpec((k, LRU_TILE), lambda i, j: (0, gb_col0 // LRU_TILE + j)),
                  full(conv_w.shape), full((1, width)), full(wg.shape), full(bg.shape), full((1, width))],
        out_specs=pl.BlockSpec((tm, LRU_TILE), lambda i, j: (i, j)),
        out_shape=jax.ShapeDtypeStruct((m, width), BF16),
        scratch_shapes=[pltpu.VMEM((tm, 2 * LRU_TILE), F32),
                        pltpu.VMEM((n_tiles, SUBLANES, LRU_TILE), F32),
                        pltpu.VMEM((n_tiles, 1, LRU_TILE), F32)],
        compiler_params=_cparams(("arbitrary", "arbitrary"), 48),
        name="even_in_proj_rglru",
    )(hn, w, w, conv_w, conv_b.reshape(1, width), wg, bg, lam.reshape(1, width))


def _ssd_kernel(sz_ref, xbc_ref, dt_ref, cw_ref, cb_ref, dtb_ref, alog_ref, dsk_ref, nw_ref, pidx_ref, o_ref,
                hist_ref, xc_ref, st_ref, st2_ref, y_ref, tt_ref, cs_ref, cbd_ref, bt_ref, e_ref, *, tb):
    n_chunks = tb // CHUNK

    @pl.when(pl.program_id(1) == 0)
    def _():
        hist_ref[...] = jnp.zeros_like(hist_ref)
        st_ref[...] = jnp.zeros_like(st_ref)

    tri_bf = _tri(CHUNK).astype(BF16)
    lane = lax.broadcasted_iota(jnp.int32, (CHUNK, LANES), 1)
    rowi = lax.broadcasted_iota(jnp.int32, (CHUNK, LANES), 0)
    lane_hi = (lane >= SSD_HEADDIM).astype(jnp.int32)
    causal2 = rowi >= (lane - SSD_HEADDIM * lane_hi)
    lane1_lo = lax.broadcasted_iota(jnp.int32, (1, LANES), 1) < SSD_HEADDIM
    lane_lo = lane < SSD_HEADDIM
    nt = (((1,), (1,)), ((), ()))
    a_neg = -jnp.exp(alog_ref[...])

    for c in range(n_chunks):
        dt = _softplus(dt_ref[c * CHUNK:(c + 1) * CHUNK, :] + dtb_ref[...])
        cs = _chunk_cumsum(tri_bf, dt * a_neg, terms=3)
        w = dt * jnp.exp(cs[CHUNK - 1:CHUNK] - cs)
        cs_ref[c] = cs
        for t, arr in enumerate((cs - jnp.log(dt), w)):
            tt_ref[c, t] = jnp.concatenate([arr, arr], axis=0).T

    def expand_group(c, g):
        cs = cs_ref[c]
        for i in range(PAIRS_PER_GROUP):
            pair = g * PAIRS_PER_GROUP + i
            idx = jnp.broadcast_to(pidx_ref[pl.ds(pair, 1), :], (CHUNK, LANES))
            e_ref[c, :, pl.ds(pl.multiple_of(pair * LANES, LANES), LANES)] = jnp.take_along_axis(
                cs, idx, axis=1, mode="promise_in_bounds")

    slab = 512
    n_expand = n_chunks * SSD_GROUPS
    assert SSD_CONV_DIM // slab >= n_expand

    def conv_body(j, carry):
        job = lax.rem(j, n_expand)
        expand_group(job // SSD_GROUPS, lax.rem(job, SSD_GROUPS))
        cols = pl.ds(pl.multiple_of(j * slab, slab), slab)
        for r0, acc in _causal_conv_rows(hist_ref[:, cols], xbc_ref, cols, tb, cw_ref[:, cols], cb_ref[:, cols]):
            xc_ref[r0:r0 + SUBLANES, cols] = _silu(acc)
        return carry

    lax.fori_loop(0, SSD_CONV_DIM // slab, conv_body, 0)

    for c in range(n_chunks):
        for g in range(SSD_GROUPS):
            b_g = xc_ref[c * CHUNK:(c + 1) * CHUNK, SSD_INNER + g * SSD_STATE:SSD_INNER + (g + 1) * SSD_STATE]
            c_g = xc_ref[c * CHUNK:(c + 1) * CHUNK,
                         SSD_INNER + SSD_BC + g * SSD_STATE:SSD_INNER + SSD_BC + (g + 1) * SSD_STATE]
            b2 = jnp.concatenate([b_g, b_g], axis=0)
            cbd_ref[c, g] = lax.dot_general(c_g.astype(BF16), b2.astype(BF16), nt,
                                            preferred_element_type=F32)
            bt_ref[c, g] = b2.T

    for c in range(n_chunks):
        r0 = c * CHUNK
        st_in, st_out = (st_ref, st2_ref) if c % 2 == 0 else (st2_ref, st_ref)

        def group_body(g, carry, c=c, r0=r0, st_in=st_in, st_out=st_out):
            c_g = xc_ref[r0:r0 + CHUNK,
                         pl.ds(pl.multiple_of(SSD_INNER + SSD_BC + g * SSD_STATE, SSD_STATE), SSD_STATE)]
            c_bf = c_g.astype(BF16)
            cb2 = cbd_ref[c, g]
            bt2 = bt_ref[c, g]
            for i in range(PAIRS_PER_GROUP):
                h_a = g * (2 * PAIRS_PER_GROUP) + 2 * i
                cols = pl.ds(pl.multiple_of(h_a * SSD_HEADDIM, LANES), LANES)

                def pair_row(t, h_a=h_a):
                    return jnp.where(lane1_lo, tt_ref[c, t, pl.ds(h_a, 1), :], tt_ref[c, t, pl.ds(h_a + 1, 1), :])

                e2 = e_ref[c, :, cols]
                att = cb2 * jnp.where(causal2, jnp.exp(e2 - pair_row(0)), 0.0)
                lhs = jnp.concatenate([att, bt2 * pair_row(1)], axis=0).astype(BF16)
                x2 = xc_ref[r0:r0 + CHUNK, cols]
                rhs = jnp.concatenate([jnp.where(lane_lo, x2, 0.0), jnp.where(lane_lo, 0.0, x2)],
                                      axis=0).astype(BF16)
                res = jnp.dot(lhs, rhs, preferred_element_type=F32)
                s_p = st_in[:, cols]
                e_out = jnp.exp(e2)
                y_off = e_out * jnp.dot(c_bf, s_p.astype(BF16), preferred_element_type=F32)
                y_ref[r0:r0 + CHUNK, cols] = res[0:CHUNK] + y_off + x2 * dsk_ref[:, cols]
                st_out[:, cols] = s_p * e_out[CHUNK - 1:CHUNK] + res[CHUNK:3 * CHUNK]
            return carry

        lax.fori_loop(0, SSD_GROUPS, group_body, 0)

    hist_ref[...] = xbc_ref[tb - SUBLANES:tb, :]

    def norm_body(r, carry):
        rows = pl.ds(pl.multiple_of(r * ROW_SLAB, ROW_SLAB), ROW_SLAB)
        _gated_rmsnorm_rows(y_ref, sz_ref, nw_ref, o_ref, rows, gate_first=True)
        return carry

    lax.fori_loop(0, tb // ROW_SLAB, norm_body, 0)


def _ssd(sz, xbc, dt, conv_w, conv_b, dt_bias, a_log, d_skip, norm_w, bsz, seq, tb=128):
    assert (tb // CHUNK) % 2 == 0, "the state ping-pong needs an even number of chunks per block"
    nb = seq // tb
    dsk = jnp.repeat(d_skip, SSD_HEADDIM).reshape(1, SSD_INNER)
    pair_idx = (2 * jnp.arange(SSD_HEADS // 2, dtype=jnp.int32)[:, None]
                + (jnp.arange(LANES, dtype=jnp.int32) // SSD_HEADDIM)[None, :])

    def rows(width):
        return pl.BlockSpec((tb, width), lambda b, s: (b * nb + s, 0))

    def full(shape):
        return pl.BlockSpec(shape, lambda b, s: (0,) * len(shape))

    return pl.pallas_call(
        functools.partial(_ssd_kernel, tb=tb),
        grid=(bsz, nb),
        in_specs=[rows(SSD_INNER), rows(SSD_CONV_DIM), rows(SSD_HEADS),
                  full(conv_w.shape), full((1, SSD_CONV_DIM)),
                  full((1, SSD_HEADS)), full((1, SSD_HEADS)), full((1, SSD_INNER)), full((1, SSD_INNER)),
                  full((SSD_HEADS // 2, LANES))],
        out_specs=rows(SSD_INNER),
        out_shape=jax.ShapeDtypeStruct((bsz * seq, SSD_INNER), BF16),
        scratch_shapes=[pltpu.VMEM((SUBLANES, SSD_CONV_DIM), F32),
                        pltpu.VMEM((tb, SSD_CONV_DIM), F32),
                        pltpu.VMEM((SSD_STATE, SSD_INNER), F32),
                        pltpu.VMEM((SSD_STATE, SSD_INNER), F32),
                        pltpu.VMEM((tb, SSD_INNER), F32),
                        pltpu.VMEM((tb // CHUNK, 2, SSD_HEADS, LANES), F32),
                        pltpu.VMEM((tb // CHUNK, CHUNK, SSD_HEADS), F32),
                        pltpu.VMEM((tb // CHUNK, SSD_GROUPS, CHUNK, LANES), F32),
                        pltpu.VMEM((tb // CHUNK, SSD_GROUPS, SSD_STATE, LANES), F32),
                        pltpu.VMEM((tb // CHUNK, CHUNK, SSD_INNER), F32)],
        compiler_params=_cparams(("parallel", "arbitrary"), 56),
        name="ssd",
    )(sz, xbc, dt, conv_w, conv_b.reshape(1, SSD_CONV_DIM), dt_bias.reshape(1, SSD_HEADS),
      a_log.reshape(1, SSD_HEADS), dsk, norm_w.reshape(1, SSD_INNER), pair_idx)


def _even_layer(h, norm_w, w_in, lb_logits, a_norm_w, conv_w, conv_b, wa, ba, wx, bx, lam, w_out, bsz, seq):
    hn = _rmsnorm(h, norm_w, BF16)
    hg = 4 * HG_HEADS * HG_DK
    lru = LRU_BLOCKS * LRU_BLOCK
    w_in = w_in.astype(BF16)
    hw = HG_HEADS * HG_DK
    proj = functools.partial(_matmul, [hn], w_in, n=hw, tm=1024, tn=1024, tk=D_MODEL, vmem_mib=56)
    sq = proj(w_col0=0, epilogue=_silu, name="even_in_proj_q")
    lf = proj(w_col0=hw, epilogue=_hgrn_logf, aux=(lb_logits,), name="even_in_proj_f")
    v = proj(w_col0=2 * hw, out_dtype=BF16, name="even_in_proj_i")
    sg = proj(w_col0=3 * hw, epilogue=_silu, name="even_in_proj_ga")
    o_a = _hgrn(sq, lf, v, sg, a_norm_w, bsz, seq)
    o_b = _lru_proj(hn, w_in, hg, hg + lru, conv_w, conv_b, wa, ba, wx, bx, lam, seq)
    return _matmul([o_a, o_b], w_out.astype(BF16), residual=h, n=D_MODEL, tm=1024, tn=1024, tk=2048,
                   vmem_mib=48, name="even_out_proj")


def _odd_layer(h, norm_w, w_in, conv_w, conv_b, dt_bias, a_log, d_skip, ssd_norm_w, w_out, bsz, seq):
    hn = _rmsnorm(h, norm_w, BF16)
    w_in = w_in.astype(BF16)
    sz = _matmul([hn], w_in, n=SSD_INNER, tm=1024, tn=1024, tk=D_MODEL, vmem_mib=56, name="odd_in_proj_z",
                 epilogue=_silu)
    xbc = _matmul([hn], w_in, n=SSD_CONV_DIM, w_col0=SSD_INNER, tm=1024, tn=1024, tk=D_MODEL, vmem_mib=48,
                  name="odd_in_proj_xbc")
    dt = _matmul([hn], w_in, n=SSD_HEADS, w_col0=SSD_INNER + SSD_CONV_DIM, tm=1024, tn=SSD_HEADS, tk=D_MODEL,
                 vmem_mib=32, name="odd_in_proj_dt")
    y = _ssd(sz, xbc, dt, conv_w, conv_b, dt_bias, a_log, d_skip, ssd_norm_w, bsz, seq)
    return _matmul([y], w_out.astype(BF16), residual=h, n=D_MODEL, tm=1024, tn=1024, tk=D_MODEL, vmem_mib=56,
                   name="odd_out_proj")


def kernel(x, norm_w, e_w_in, lb_logits, e_a_norm_w, e_conv_w, e_conv_b, e_wa, e_ba, e_wx, e_bx, e_lambda,
           e_w_out, o_w_in, o_conv_w, o_conv_b, o_dt_bias, o_a_log, o_d, o_norm_w, o_w_out, final_norm_w):
    bsz, seq, d = x.shape
    h = x.reshape(bsz * seq, d)
    h = _even_layer(h, norm_w[0], e_w_in[0], lb_logits, e_a_norm_w[0], e_conv_w[0], e_conv_b[0], e_wa[0],
                    e_ba[0], e_wx[0], e_bx[0], e_lambda[0], e_w_out[0], bsz, seq)
    h = _odd_layer(h, norm_w[1], o_w_in[0], o_conv_w[0], o_conv_b[0], o_dt_bias[0], o_a_log[0], o_d[0],
                   o_norm_w[0], o_w_out[0], bsz, seq)
    return _rmsnorm(h, final_norm_w, x.dtype).reshape(bsz, seq, d)
```

```python
import functools

import jax
import jax.numpy as jnp
from jax import lax
from jax.experimental import pallas as pl
from jax.experimental.pallas import tpu as pltpu

F32 = jnp.float32
BF16 = jnp.bfloat16

EPS = 1e-6
CHUNK = 64
LANES = 128
SUBLANES = 8
D_MODEL = 4096
HG_HEADS = 32
HG_DK = 128
LRU_BLOCKS = 32
LRU_BLOCK = 128
LRU_C = 8.0
CONV_W = 4
SSD_INNER = 8192
SSD_HEADS = 128
SSD_HEADDIM = 64
SSD_GROUPS = 8
SSD_STATE = 128
SSD_BC = SSD_GROUPS * SSD_STATE
SSD_CONV_DIM = SSD_INNER + 2 * SSD_BC
PAIRS_PER_GROUP = SSD_HEADS // SSD_GROUPS // 2
MIB = 1024 * 1024


def _cparams(semantics, vmem_mib):
    return pltpu.CompilerParams(dimension_semantics=semantics, vmem_limit_bytes=vmem_mib * MIB)


def _sigmoid(x):
    return 0.5 + 0.5 * jnp.tanh(0.5 * x)


def _silu(x):
    return x * _sigmoid(x)


def _softplus(x):
    return jnp.maximum(x, 0.0) + jnp.log1p(jnp.exp(-jnp.abs(x)))


def _chunk_cumsum(tri, x, terms):
    acc = None
    rem = x
    for t in range(terms):
        piece = rem.astype(BF16)
        part = jnp.dot(tri, piece, preferred_element_type=F32)
        acc = part if acc is None else acc + part
        if t + 1 < terms:
            rem = rem - piece.astype(F32)
    return acc


def _causal_conv_rows(hist, x_ref, cols, n_rows, w, b):
    row = lax.broadcasted_iota(jnp.int32, hist.shape, 0)
    shifts = range(1, CONV_W)
    prev_rot = [pltpu.roll(hist, s, 0) for s in shifts]
    for r0 in range(0, n_rows, SUBLANES):
        cur = x_ref[r0:r0 + SUBLANES, cols]
        cur_rot = [pltpu.roll(cur, s, 0) for s in shifts]
        out = b + cur * w[CONV_W - 1:CONV_W]
        for s, c_rot, p_rot in zip(shifts, cur_rot, prev_rot):
            out = out + jnp.where(row >= s, c_rot, p_rot) * w[CONV_W - 1 - s:CONV_W - s]
        prev_rot = cur_rot
        yield r0, out


NORM_LANES = 512


def _gated_rmsnorm_rows(y_ref, g_ref, w_ref, o_ref, rows, gate_first):
    width = y_ref.shape[1]
    chunks = [slice(c0, c0 + NORM_LANES) for c0 in range(0, width, NORM_LANES)]
    ssq = None
    for cs in chunks:
        v = y_ref[rows, cs] * g_ref[rows, cs] if gate_first else y_ref[rows, cs]
        ssq = v * v if ssq is None else ssq + v * v
    scale = lax.rsqrt(jnp.sum(ssq, axis=-1, keepdims=True) * (1.0 / width) + EPS)
    for cs in chunks:
        v = y_ref[rows, cs] * g_ref[rows, cs]
        o_ref[rows, cs] = (v * scale * w_ref[:, cs]).astype(o_ref.dtype)


def _tri(n):
    r = lax.broadcasted_iota(jnp.int32, (n, n), 0)
    c = lax.broadcasted_iota(jnp.int32, (n, n), 1)
    return r >= c


def _norm_kernel(x_ref, w_ref, o_ref):
    x = x_ref[...]
    ms = jnp.mean(x * x, axis=-1, keepdims=True)
    o_ref[...] = (x * lax.rsqrt(ms + EPS) * w_ref[...]).astype(o_ref.dtype)


def _rmsnorm(x, w, out_dtype, tm=256):
    m, d = x.shape
    return pl.pallas_call(
        _norm_kernel,
        grid=(m // tm,),
        in_specs=[pl.BlockSpec((tm, d), lambda i: (i, 0)),
                  pl.BlockSpec((1, d), lambda i: (0, 0))],
        out_specs=pl.BlockSpec((tm, d), lambda i: (i, 0)),
        out_shape=jax.ShapeDtypeStruct((m, d), out_dtype),
        compiler_params=_cparams(("parallel",), 32),
        name="rmsnorm",
    )(x, w.reshape(1, d))


def _matmul_kernel(*refs, k_ranges, has_res, n_aux, epilogue):
    n_a = len(k_ranges)
    a_refs = refs[:n_a]
    b_ref = refs[n_a]
    r_ref = refs[n_a + 1] if has_res else None
    aux_refs = refs[len(refs) - 1 - n_aux:len(refs) - 1]
    o_ref = refs[-1]
    nk = k_ranges[-1][1]
    if nk == 1:
        p = jnp.dot(a_refs[0][...], b_ref[...], preferred_element_type=F32)
        if has_res:
            p = p + r_ref[...]
        if epilogue is not None:
            p = epilogue(p, *[r[...] for r in aux_refs])
        o_ref[...] = p.astype(o_ref.dtype)
        return
    k = pl.program_id(2)

    def first_step(a_ref):
        p = jnp.dot(a_ref[...], b_ref[...], preferred_element_type=F32)
        o_ref[...] = p + r_ref[...] if has_res else p

    def later_step(a_ref):
        o_ref[...] += jnp.dot(a_ref[...], b_ref[...], preferred_element_type=F32)

    for a_ref, (k0, k1) in zip(a_refs, k_ranges):
        if k0 == 0:
            pl.when(k == 0)(functools.partial(first_step, a_ref))
            k0 = 1
        if k1 > k0:
            pl.when((k >= k0) & (k < k1))(functools.partial(later_step, a_ref))


def _matmul(a_list, w, residual=None, *, n, w_col0=0, tm, tn, tk, vmem_mib, name, epilogue=None, aux=(),
            out_dtype=F32):
    m = a_list[0].shape[0]
    k_ranges, k0 = [], 0
    for a in a_list:
        k_ranges.append((k0, k0 + a.shape[1] // tk))
        k0 = k_ranges[-1][1]
    nk = k0
    assert nk * tk == w.shape[0] and w_col0 % tn == 0
    assert nk == 1 or (epilogue is None and out_dtype == F32)
    in_specs = [pl.BlockSpec((tm, tk), lambda i, j, k, k0=k0, k1=k1: (i, jnp.clip(k - k0, 0, k1 - k0 - 1)))
                for k0, k1 in k_ranges]
    in_specs.append(pl.BlockSpec((tk, tn), lambda i, j, k: (k, w_col0 // tn + j)))
    args = list(a_list) + [w]
    if residual is not None:
        in_specs.append(pl.BlockSpec((tm, tn), lambda i, j, k: (i, j)))
        args.append(residual)
    for x in aux:
        in_specs.append(pl.BlockSpec((x.shape[0], tn), lambda i, j, k: (0, j)))
        args.append(x)
    return pl.pallas_call(
        functools.partial(_matmul_kernel, k_ranges=tuple(k_ranges), has_res=residual is not None,
                          n_aux=len(aux), epilogue=epilogue),
        grid=(m // tm, n // tn, nk),
        in_specs=in_specs,
        out_specs=pl.BlockSpec((tm, tn), lambda i, j, k: (i, j)),
        out_shape=jax.ShapeDtypeStruct((m, n), out_dtype),
        compiler_params=_cparams(("parallel", "parallel", "arbitrary"), vmem_mib),
        name=name,
    )(*args)


HG_SLAB = 512
ROW_SLAB = 16


def _hgrn_logf(f_pre, lb_logits):
    e = jnp.exp(lb_logits - jnp.max(lb_logits, axis=0, keepdims=True))
    lb = e[0:1] / jnp.sum(e, axis=0, keepdims=True)
    half_k = 0.5 * (1.0 - lb)
    return jnp.log(1.0 - (half_k - half_k * jnp.tanh(0.5 * f_pre)))


def _hgrn_kernel(sq_ref, lf_ref, v_ref, sg_ref, ob_ref, nw_ref, o_ref,
                 st_ref, acc_ref, qin_ref, kin_ref, qout_ref, kst_ref, gl_ref, *, tb):
    @pl.when(pl.program_id(1) == 0)
    def _():
        st_ref[...] = jnp.zeros_like(st_ref)

    hg_width = HG_HEADS * HG_DK
    o_ref[:, hg_width:] = ob_ref[...]
    n_chunks = tb // CHUNK
    n_slabs = hg_width // HG_SLAB
    heads_per_slab = HG_SLAB // HG_DK
    tri = _tri(CHUNK)
    tri_bf = tri.astype(BF16)
    nt = (((1,), (1,)), ((), ()))
    tn = (((0,), (0,)), ((), ()))

    def phase_a(j, p):
        cols = pl.ds(pl.multiple_of(j * HG_SLAB, HG_SLAB), HG_SLAB)
        for c in range(n_chunks):
            rows = pl.ds(c * CHUNK, CHUNK)
            logf = lf_ref[rows, cols]
            k = 1.0 - jnp.exp(logf)
            b = _chunk_cumsum(tri_bf, logf, terms=2)
            ref = b[CHUNK // 2 - 1:CHUNK // 2]
            blast = b[CHUNK - 1:CHUNK]
            d = b - ref
            q_in = sq_ref[rows, cols] * jnp.exp(d)
            k_in = k * jnp.exp(-d)
            qin_ref[p, rows, :] = q_in.astype(BF16)
            kin_ref[p, rows, :] = k_in.astype(BF16)
            qout_ref[p, rows, :] = (q_in * jnp.exp(ref)).astype(BF16)
            kst_ref[p, rows, :] = (k_in * jnp.exp(blast - ref)).astype(BF16)
            gl_ref[p, c:c + 1, :] = jnp.exp(blast)

    def phase_b(j, p):
        states = [st_ref[j * heads_per_slab + hh] for hh in range(heads_per_slab)]
        for hh in range(heads_per_slab):
            lanes = slice(hh * HG_DK, (hh + 1) * HG_DK)
            out_cols = pl.ds(pl.multiple_of((j * heads_per_slab + hh) * HG_DK, HG_DK), HG_DK)
            s_t = states[hh]
            for c in range(n_chunks):
                rows = pl.ds(c * CHUNK, CHUNK)
                v = v_ref[rows, out_cols]
                scores = lax.dot_general(qin_ref[p, rows, lanes], kin_ref[p, rows, lanes], nt,
                                         preferred_element_type=F32)
                scores = jnp.where(tri, scores, 0.0).astype(BF16)
                lhs = jnp.concatenate([qout_ref[p, rows, lanes], scores], axis=1)
                rhs = jnp.concatenate([s_t.astype(BF16).T, v], axis=0)
                acc_ref[rows, out_cols] = jnp.dot(lhs, rhs, preferred_element_type=F32)
                s_t = (s_t * gl_ref[p, c:c + 1, lanes]
                       + lax.dot_general(v, kst_ref[p, rows, lanes], tn, preferred_element_type=F32))
            states[hh] = s_t
        for hh in range(heads_per_slab):
            st_ref[j * heads_per_slab + hh] = states[hh]

    phase_a(0, 0)

    def skew_body(jj, carry):
        j = 2 * jj
        phase_a(j + 1, 1)
        phase_b(j, 0)
        phase_a(j + 2, 0)
        phase_b(j + 1, 1)
        return carry

    lax.fori_loop(0, n_slabs // 2 - 1, skew_body, 0)
    phase_a(n_slabs - 1, 1)
    phase_b(n_slabs - 2, 0)
    phase_b(n_slabs - 1, 1)

    def norm_body(r, carry):
        rows = pl.ds(pl.multiple_of(r * ROW_SLAB, ROW_SLAB), ROW_SLAB)
        _gated_rmsnorm_rows(acc_ref, sg_ref, nw_ref, o_ref, rows, gate_first=False)
        return carry

    lax.fori_loop(0, tb // ROW_SLAB, norm_body, 0)


def _hgrn(sq, lf, v, sg, o_b, a_norm_w, bsz, seq, tb=128):
    width = HG_HEADS * HG_DK
    nb = seq // tb
    rows = pl.BlockSpec((tb, width), lambda b, s: (b * nb + s, 0))
    rows_b = pl.BlockSpec((tb, o_b.shape[1]), lambda b, s: (b * nb + s, 0))
    return pl.pallas_call(
        functools.partial(_hgrn_kernel, tb=tb),
        grid=(bsz, nb),
        in_specs=[rows, rows, rows, rows, rows_b, pl.BlockSpec((1, width), lambda b, s: (0, 0))],
        out_specs=pl.BlockSpec((tb, width + o_b.shape[1]), lambda b, s: (b * nb + s, 0)),
        out_shape=jax.ShapeDtypeStruct((bsz * seq, width + o_b.shape[1]), BF16),
        scratch_shapes=[pltpu.VMEM((HG_HEADS, HG_DK, HG_DK), F32),
                        pltpu.VMEM((tb, width), F32)]
                       + [pltpu.VMEM((2, tb, HG_SLAB), BF16)] * 4
                       + [pltpu.VMEM((2, tb // CHUNK, HG_SLAB), F32)],
        compiler_params=_cparams(("parallel", "arbitrary"), 40),
        name="hgrn2",
    )(sq, lf, v, sg, o_b, a_norm_w.reshape(1, width))


LRU_TILE = 512


def _lru_block(xc, gate_in, wg, ba, bx, lam, h, row):
    gates = jnp.dot(xc.astype(BF16), wg, preferred_element_type=F32)
    quarter = (0.25 * LRU_C) * _softplus(-lam)
    t_r = jnp.tanh(0.5 * (gates[:, :LRU_BLOCK] + ba))
    tau = jnp.tanh(-quarter - quarter * t_r)
    rho = 1.0 / (1.0 - tau)
    a = (1.0 + tau) * rho
    neg_tau = -tau
    root = jnp.where(neg_tau > 0.0, neg_tau * lax.rsqrt(neg_tau), 0.0)
    t_i = jnp.tanh(0.5 * (gates[:, LRU_BLOCK:] + bx))
    u = (rho * root) * ((1.0 + t_i) * xc)
    outs = []
    for r8 in range(xc.shape[0] // SUBLANES):
        a8 = a[r8 * SUBLANES:(r8 + 1) * SUBLANES]
        u8 = u[r8 * SUBLANES:(r8 + 1) * SUBLANES]
        for sh in (1, 2, 4):
            keep = row >= sh
            a_s = jnp.where(keep, pltpu.roll(a8, sh, 0), 1.0)
            u_s = jnp.where(keep, pltpu.roll(u8, sh, 0), 0.0)
            u8 = a8 * u_s + u8
            a8 = a8 * a_s
        h8 = a8 * h + u8
        h = h8[SUBLANES - 1:SUBLANES]
        outs.append(h8)
    half_g = 0.5 * gate_in
    return (jnp.concatenate(outs, axis=0) * half_g) * (1.0 + jnp.tanh(half_g)), h


def _lru_proj_kernel(a_ref, wxb_ref, wgb_ref, cw_ref, cb_ref, wg_ref, bg_ref, lam_ref, o_ref,
                     p_ref, hist_ref, h_ref, *, tm, tiles_per_seq):
    i = pl.program_id(0)
    j = pl.program_id(1)

    @pl.when(lax.rem(i, tiles_per_seq) == 0)
    def _():
        hist_ref[j] = jnp.zeros(hist_ref.shape[1:], F32)
        h_ref[j] = jnp.zeros(h_ref.shape[1:], F32)

    a = a_ref[...]
    p_ref[:, 0:LRU_TILE] = jnp.dot(a, wxb_ref[...], preferred_element_type=F32)
    p_ref[:, LRU_TILE:2 * LRU_TILE] = jnp.dot(a, wgb_ref[...], preferred_element_type=F32)
    row = lax.broadcasted_iota(jnp.int32, (SUBLANES, LRU_BLOCK), 0)
    hist = hist_ref[j]
    h_in = h_ref[j]
    h_out = []
    for blk in range(LRU_TILE // LRU_BLOCK):
        lanes = slice(blk * LRU_BLOCK, (blk + 1) * LRU_BLOCK)
        n = j * (LRU_TILE // LRU_BLOCK) + blk
        cols = pl.ds(pl.multiple_of(n * LRU_BLOCK, LRU_BLOCK), LRU_BLOCK)
        xc = jnp.concatenate(
            [acc for _, acc in _causal_conv_rows(hist[:, lanes], p_ref, lanes, tm, cw_ref[:, cols], cb_ref[:, cols])],
            axis=0)
        out, h_last = _lru_block(xc, p_ref[:, LRU_TILE + blk * LRU_BLOCK:LRU_TILE + (blk + 1) * LRU_BLOCK],
                                 wg_ref[n], bg_ref[0:1, cols], bg_ref[1:2, cols], lam_ref[:, cols],
                                 h_in[:, lanes], row)
        o_ref[:, lanes] = out.astype(o_ref.dtype)
        h_out.append(h_last)
    hist_ref[j] = p_ref[tm - SUBLANES:tm, 0:LRU_TILE]
    h_ref[j] = jnp.concatenate(h_out, axis=1)


def _lru_proj(hn, w, xb_col0, gb_col0, conv_w, conv_b, wa, ba, wx, bx, lam, seq, tm=1024):
    m, k = hn.shape
    width = LRU_BLOCKS * LRU_BLOCK
    n_tiles = width // LRU_TILE
    assert xb_col0 % LRU_TILE == 0 and gb_col0 % LRU_TILE == 0
    assert seq % tm == 0, "row tiles must not straddle sequences"
    wg = jnp.concatenate([wa, wx], axis=-1).astype(BF16)
    bg = jnp.stack([ba, bx], axis=0)

    def full(shape):
        return pl.BlockSpec(shape, lambda i, j: (0,) * len(shape))

    return pl.pallas_call(
        functools.partial(_lru_proj_kernel, tm=tm, tiles_per_seq=seq // tm),
        grid=(m // tm, n_tiles),
        in_specs=[pl.BlockSpec((tm, k), lambda i, j: (i, 0)),
                  pl.BlockSpec((k, LRU_TILE), lambda i, j: (0, xb_col0 // LRU_TILE + j)),
                  pl.BlockSpec((k, LRU_TILE), lambda i, j: (0, gb_col0 // LRU_TILE + j)),
                  full(conv_w.shape), full((1, width)), full(wg.shape), full(bg.shape), full((1, width))],
        out_specs=pl.BlockSpec((tm, LRU_TILE), lambda i, j: (i, j)),
        out_shape=jax.ShapeDtypeStruct((m, width), BF16),
        scratch_shapes=[pltpu.VMEM((tm, 2 * LRU_TILE), F32),
                        pltpu.VMEM((n_tiles, SUBLANES, LRU_TILE), F32),
                        pltpu.VMEM((n_tiles, 1, LRU_TILE), F32)],
        compiler_params=_cparams(("arbitrary", "arbitrary"), 48),
        name="even_in_proj_rglru",
    )(hn, w, w, conv_w, conv_b.reshape(1, width), wg, bg, lam.reshape(1, width))


def _ssd_kernel(sz_ref, xbc_ref, dt_ref, cw_ref, cb_ref, dtb_ref, alog_ref, dsk_ref, nw_ref, pidx_ref, o_ref,
                hist_ref, xc_ref, st_ref, st2_ref, y_ref, tt_ref, cs_ref, cbd_ref, bt_ref, e_ref, *, tb):
    n_chunks = tb // CHUNK

    @pl.when(pl.program_id(1) == 0)
    def _():
        hist_ref[...] = jnp.zeros_like(hist_ref)
        st_ref[...] = jnp.zeros_like(st_ref)

    tri_bf = _tri(CHUNK).astype(BF16)
    lane = lax.broadcasted_iota(jnp.int32, (CHUNK, LANES), 1)
    rowi = lax.broadcasted_iota(jnp.int32, (CHUNK, LANES), 0)
    lane_hi = (lane >= SSD_HEADDIM).astype(jnp.int32)
    causal2 = rowi >= (lane - SSD_HEADDIM * lane_hi)
    lane1_lo = lax.broadcasted_iota(jnp.int32, (1, LANES), 1) < SSD_HEADDIM
    lane_lo = lane < SSD_HEADDIM
    nt = (((1,), (1,)), ((), ()))
    a_neg = -jnp.exp(alog_ref[...])

    for c in range(n_chunks):
        dt = _softplus(dt_ref[c * CHUNK:(c + 1) * CHUNK, :] + dtb_ref[...])
        cs = _chunk_cumsum(tri_bf, dt * a_neg, terms=3)
        w = dt * jnp.exp(cs[CHUNK - 1:CHUNK] - cs)
        cs_ref[c] = cs
        for t, arr in enumerate((cs - jnp.log(dt), w)):
            tt_ref[c, t] = jnp.concatenate([arr, arr], axis=0).T

    def expand_group(c, g):
        cs = cs_ref[c]
        for i in range(PAIRS_PER_GROUP):
            pair = g * PAIRS_PER_GROUP + i
            idx = jnp.broadcast_to(pidx_ref[pl.ds(pair, 1), :], (CHUNK, LANES))
            e_ref[c, :, pl.ds(pl.multiple_of(pair * LANES, LANES), LANES)] = jnp.take_along_axis(
                cs, idx, axis=1, mode="promise_in_bounds")

    slab = 512
    n_expand = n_chunks * SSD_GROUPS
    assert SSD_CONV_DIM // slab >= n_expand

    def conv_body(j, carry):
        job = lax.rem(j, n_expand)
        expand_group(job // SSD_GROUPS, lax.rem(job, SSD_GROUPS))
        cols = pl.ds(pl.multiple_of(j * slab, slab), slab)
        for r0, acc in _causal_conv_rows(hist_ref[:, cols], xbc_ref, cols, tb, cw_ref[:, cols], cb_ref[:, cols]):
            xc_ref[r0:r0 + SUBLANES, cols] = _silu(acc)
        return carry

    lax.fori_loop(0, SSD_CONV_DIM // slab, conv_body, 0)

    for c in range(n_chunks):
        for g in range(SSD_GROUPS):
            b_g = xc_ref[c * CHUNK:(c + 1) * CHUNK, SSD_INNER + g * SSD_STATE:SSD_INNER + (g + 1) * SSD_STATE]
            c_g = xc_ref[c * CHUNK:(c + 1) * CHUNK,
                         SSD_INNER + SSD_BC + g * SSD_STATE:SSD_INNER + SSD_BC + (g + 1) * SSD_STATE]
            b2 = jnp.concatenate([b_g, b_g], axis=0)
            cbd_ref[c, g] = lax.dot_general(c_g.astype(BF16), b2.astype(BF16), nt,
                                            preferred_element_type=F32)
            bt_ref[c, g] = b2.T

    for c in range(n_chunks):
        r0 = c * CHUNK
        st_in, st_out = (st_ref, st2_ref) if c % 2 == 0 else (st2_ref, st_ref)

        def group_body(g, carry, c=c, r0=r0, st_in=st_in, st_out=st_out):
            c_g = xc_ref[r0:r0 + CHUNK,
                         pl.ds(pl.multiple_of(SSD_INNER + SSD_BC + g * SSD_STATE, SSD_STATE), SSD_STATE)]
            c_bf = c_g.astype(BF16)
            cb2 = cbd_ref[c, g]
            bt2 = bt_ref[c, g]
            for i in range(PAIRS_PER_GROUP):
                h_a = g * (2 * PAIRS_PER_GROUP) + 2 * i
                cols = pl.ds(pl.multiple_of(h_a * SSD_HEADDIM, LANES), LANES)

                def pair_row(t, h_a=h_a):
                    return jnp.where(lane1_lo, tt_ref[c, t, pl.ds(h_a, 1), :], tt_ref[c, t, pl.ds(h_a + 1, 1), :])

                e2 = e_ref[c, :, cols]
                att = cb2 * jnp.where(causal2, jnp.exp(e2 - pair_row(0)), 0.0)
                lhs = jnp.concatenate([att, bt2 * pair_row(1)], axis=0).astype(BF16)
                x2 = xc_ref[r0:r0 + CHUNK, cols]
                rhs = jnp.concatenate([jnp.where(lane_lo, x2, 0.0), jnp.where(lane_lo, 0.0, x2)],
                                      axis=0).astype(BF16)
                res = jnp.dot(lhs, rhs, preferred_element_type=F32)
                s_p = st_in[:, cols]
                e_out = jnp.exp(e2)
                y_off = e_out * jnp.dot(c_bf, s_p.astype(BF16), preferred_element_type=F32)
                y_ref[r0:r0 + CHUNK, cols] = res[0:CHUNK] + y_off + x2 * dsk_ref[:, cols]
                st_out[:, cols] = s_p * e_out[CHUNK - 1:CHUNK] + res[CHUNK:3 * CHUNK]
            return carry

        lax.fori_loop(0, SSD_GROUPS, group_body, 0)

    hist_ref[...] = xbc_ref[tb - SUBLANES:tb, :]

    def norm_body(r, carry):
        rows = pl.ds(pl.multiple_of(r * ROW_SLAB, ROW_SLAB), ROW_SLAB)
        _gated_rmsnorm_rows(y_ref, sz_ref, nw_ref, o_ref, rows, gate_first=True)
        return carry

    lax.fori_loop(0, tb // ROW_SLAB, norm_body, 0)


def _ssd(sz, xbc, dt, conv_w, conv_b, dt_bias, a_log, d_skip, norm_w, bsz, seq, tb=128):
    assert (tb // CHUNK) % 2 == 0, "the state ping-pong needs an even number of chunks per block"
    nb = seq // tb
    dsk = jnp.repeat(d_skip, SSD_HEADDIM).reshape(1, SSD_INNER)
    pair_idx = (2 * jnp.arange(SSD_HEADS // 2, dtype=jnp.int32)[:, None]
                + (jnp.arange(LANES, dtype=jnp.int32) // SSD_HEADDIM)[None, :])

    def rows(width):
        return pl.BlockSpec((tb, width), lambda b, s: (b * nb + s, 0))

    def full(shape):
        return pl.BlockSpec(shape, lambda b, s: (0,) * len(shape))

    return pl.pallas_call(
        functools.partial(_ssd_kernel, tb=tb),
        grid=(bsz, nb),
        in_specs=[rows(SSD_INNER), rows(SSD_CONV_DIM), rows(SSD_HEADS),
                  full(conv_w.shape), full((1, SSD_CONV_DIM)),
                  full((1, SSD_HEADS)), full((1, SSD_HEADS)), full((1, SSD_INNER)), full((1, SSD_INNER)),
                  full((SSD_HEADS // 2, LANES))],
        out_specs=rows(SSD_INNER),
        out_shape=jax.ShapeDtypeStruct((bsz * seq, SSD_INNER), BF16),
        scratch_shapes=[pltpu.VMEM((SUBLANES, SSD_CONV_DIM), F32),
                        pltpu.VMEM((tb, SSD_CONV_DIM), F32),
                        pltpu.VMEM((SSD_STATE, SSD_INNER), F32),
                        pltpu.VMEM((SSD_STATE, SSD_INNER), F32),
                        pltpu.VMEM((tb, SSD_INNER), F32),
                        pltpu.VMEM((tb // CHUNK, 2, SSD_HEADS, LANES), F32),
                        pltpu.VMEM((tb // CHUNK, CHUNK, SSD_HEADS), F32),
                        pltpu.VMEM((tb // CHUNK, SSD_GROUPS, CHUNK, LANES), F32),
                        pltpu.VMEM((tb // CHUNK, SSD_GROUPS, SSD_STATE, LANES), F32),
                        pltpu.VMEM((tb // CHUNK, CHUNK, SSD_INNER), F32)],
        compiler_params=_cparams(("parallel", "arbitrary"), 56),
        name="ssd",
    )(sz, xbc, dt, conv_w, conv_b.reshape(1, SSD_CONV_DIM), dt_bias.reshape(1, SSD_HEADS),
      a_log.reshape(1, SSD_HEADS), dsk, norm_w.reshape(1, SSD_INNER), pair_idx)


def _even_layer(h, norm_w, w_in, lb_logits, a_norm_w, conv_w, conv_b, wa, ba, wx, bx, lam, w_out, bsz, seq):
    hn = _rmsnorm(h, norm_w, BF16)
    hg = 4 * HG_HEADS * HG_DK
    lru = LRU_BLOCKS * LRU_BLOCK
    w_in = w_in.astype(BF16)
    hw = HG_HEADS * HG_DK
    proj = functools.partial(_matmul, [hn], w_in, n=hw, tm=1024, tn=1024, tk=D_MODEL, vmem_mib=56)
    sq = proj(w_col0=0, epilogue=_silu, name="even_in_proj_q")
    lf = proj(w_col0=hw, epilogue=_hgrn_logf, aux=(lb_logits,), name="even_in_proj_f")
    v = proj(w_col0=2 * hw, out_dtype=BF16, name="even_in_proj_i")
    sg = proj(w_col0=3 * hw, epilogue=_silu, name="even_in_proj_ga")
    o_b = _lru_proj(hn, w_in, hg, hg + lru, conv_w, conv_b, wa, ba, wx, bx, lam, seq)
    mix = _hgrn(sq, lf, v, sg, o_b, a_norm_w, bsz, seq)
    return _matmul([mix], w_out.astype(BF16), residual=h, n=D_MODEL, tm=1024, tn=1024, tk=D_MODEL,
                   vmem_mib=56, name="even_out_proj")


def _odd_layer(h, norm_w, w_in, conv_w, conv_b, dt_bias, a_log, d_skip, ssd_norm_w, w_out, bsz, seq):
    hn = _rmsnorm(h, norm_w, BF16)
    w_in = w_in.astype(BF16)
    sz = _matmul([hn], w_in, n=SSD_INNER, tm=1024, tn=1024, tk=D_MODEL, vmem_mib=56, name="odd_in_proj_z",
                 epilogue=_silu)
    xbc = _matmul([hn], w_in, n=SSD_CONV_DIM, w_col0=SSD_INNER, tm=1024, tn=1024, tk=D_MODEL, vmem_mib=48,
                  name="odd_in_proj_xbc")
    dt = _matmul([hn], w_in, n=SSD_HEADS, w_col0=SSD_INNER + SSD_CONV_DIM, tm=1024, tn=SSD_HEADS, tk=D_MODEL,
                 vmem_mib=32, name="odd_in_proj_dt")
    y = _ssd(sz, xbc, dt, conv_w, conv_b, dt_bias, a_log, d_skip, ssd_norm_w, bsz, seq)
    return _matmul([y], w_out.astype(BF16), residual=h, n=D_MODEL, tm=1024, tn=1024, tk=D_MODEL, vmem_mib=56,
                   name="odd_out_proj")


def kernel(x, norm_w, e_w_in, lb_logits, e_a_norm_w, e_conv_w, e_conv_b, e_wa, e_ba, e_wx, e_bx, e_lambda,
           e_w_out, o_w_in, o_conv_w, o_conv_b, o_dt_bias, o_a_log, o_d, o_norm_w, o_w_out, final_norm_w):
    bsz, seq, d = x.shape
    h = x.reshape(bsz * seq, d)
    h = _even_layer(h, norm_w[0], e_w_in[0], lb_logits, e_a_norm_w[0], e_conv_w[0], e_conv_b[0], e_wa[0],
                    e_ba[0], e_wx[0], e_bx[0], e_lambda[0], e_w_out[0], bsz, seq)
    h = _odd_layer(h, norm_w[1], o_w_in[0], o_conv_w[0], o_conv_b[0], o_dt_bias[0], o_a_log[0], o_d[0],
                   o_norm_w[0], o_w_out[0], bsz, seq)
    return _rmsnorm(h, final_norm_w, x.dtype).reshape(bsz, seq, d)
```

```python
import functools

import jax
import jax.numpy as jnp
from jax import lax
from jax.experimental import pallas as pl
from jax.experimental.pallas import tpu as pltpu

F32 = jnp.float32
BF16 = jnp.bfloat16

EPS = 1e-6
CHUNK = 64
LANES = 128
SUBLANES = 8
D_MODEL = 4096
HG_HEADS = 32
HG_DK = 128
LRU_BLOCKS = 32
LRU_BLOCK = 128
LRU_C = 8.0
CONV_W = 4
SSD_INNER = 8192
SSD_HEADS = 128
SSD_HEADDIM = 64
SSD_GROUPS = 8
SSD_STATE = 128
SSD_BC = SSD_GROUPS * SSD_STATE
SSD_CONV_DIM = SSD_INNER + 2 * SSD_BC
PAIRS_PER_GROUP = SSD_HEADS // SSD_GROUPS // 2
MIB = 1024 * 1024


def _cparams(semantics, vmem_mib):
    return pltpu.CompilerParams(dimension_semantics=semantics, vmem_limit_bytes=vmem_mib * MIB)


def _sigmoid(x):
    return 0.5 + 0.5 * jnp.tanh(0.5 * x)


def _silu(x):
    return x * _sigmoid(x)


def _softplus(x):
    return jnp.maximum(x, 0.0) + jnp.log1p(jnp.exp(-jnp.abs(x)))


def _chunk_cumsum(tri, x, terms):
    acc = None
    rem = x
    for t in range(terms):
        piece = rem.astype(BF16)
        part = jnp.dot(tri, piece, preferred_element_type=F32)
        acc = part if acc is None else acc + part
        if t + 1 < terms:
            rem = rem - piece.astype(F32)
    return acc


def _causal_conv_rows(hist, x_ref, cols, n_rows, w, b):
    row = lax.broadcasted_iota(jnp.int32, hist.shape, 0)
    shifts = range(1, CONV_W)
    taps = [jnp.broadcast_to(w[CONV_W - 1 - s:CONV_W - s], hist.shape) for s in range(CONV_W)]
    bias = jnp.broadcast_to(b, hist.shape)
    prev_rot = [pltpu.roll(hist, s, 0) for s in shifts]
    for r0 in range(0, n_rows, SUBLANES):
        cur = x_ref[r0:r0 + SUBLANES, cols]
        cur_rot = [pltpu.roll(cur, s, 0) for s in shifts]
        out = bias + cur * taps[0]
        for s, c_rot, p_rot in zip(shifts, cur_rot, prev_rot):
            out = out + jnp.where(row >= s, c_rot, p_rot) * taps[s]
        prev_rot = cur_rot
        yield r0, out


NORM_LANES = 512


def _gated_rmsnorm_rows(y_ref, g_ref, w_ref, o_ref, rows, gate_first):
    width = y_ref.shape[1]
    chunks = [slice(c0, c0 + NORM_LANES) for c0 in range(0, width, NORM_LANES)]
    ssq = None
    for cs in chunks:
        v = y_ref[rows, cs] * g_ref[rows, cs] if gate_first else y_ref[rows, cs]
        ssq = v * v if ssq is None else ssq + v * v
    scale = lax.rsqrt(jnp.sum(ssq, axis=-1, keepdims=True) * (1.0 / width) + EPS)
    for cs in chunks:
        v = y_ref[rows, cs] * g_ref[rows, cs]
        o_ref[rows, cs] = (v * scale * w_ref[:, cs]).astype(o_ref.dtype)


def _tri(n):
    r = lax.broadcasted_iota(jnp.int32, (n, n), 0)
    c = lax.broadcasted_iota(jnp.int32, (n, n), 1)
    return r >= c


def _norm_kernel(x_ref, w_ref, o_ref):
    x = x_ref[...]
    ms = jnp.mean(x * x, axis=-1, keepdims=True)
    o_ref[...] = (x * lax.rsqrt(ms + EPS) * w_ref[...]).astype(o_ref.dtype)


def _rmsnorm(x, w, out_dtype, tm=256):
    m, d = x.shape
    return pl.pallas_call(
        _norm_kernel,
        grid=(m // tm,),
        in_specs=[pl.BlockSpec((tm, d), lambda i: (i, 0)),
                  pl.BlockSpec((1, d), lambda i: (0, 0))],
        out_specs=pl.BlockSpec((tm, d), lambda i: (i, 0)),
        out_shape=jax.ShapeDtypeStruct((m, d), out_dtype),
        compiler_params=_cparams(("parallel",), 32),
        name="rmsnorm",
    )(x, w.reshape(1, d))


def _matmul_kernel(*refs, k_ranges, has_res, n_aux, epilogue):
    n_a = len(k_ranges)
    a_refs = refs[:n_a]
    b_ref = refs[n_a]
    r_ref = refs[n_a + 1] if has_res else None
    aux_refs = refs[len(refs) - 1 - n_aux:len(refs) - 1]
    o_ref = refs[-1]
    nk = k_ranges[-1][1]
    if nk == 1:
        p = jnp.dot(a_refs[0][...], b_ref[...], preferred_element_type=F32)
        if has_res:
            p = p + r_ref[...]
        if epilogue is not None:
            p = epilogue(p, *[r[...] for r in aux_refs])
        o_ref[...] = p.astype(o_ref.dtype)
        return
    k = pl.program_id(2)

    def first_step(a_ref):
        p = jnp.dot(a_ref[...], b_ref[...], preferred_element_type=F32)
        o_ref[...] = p + r_ref[...] if has_res else p

    def later_step(a_ref):
        o_ref[...] += jnp.dot(a_ref[...], b_ref[...], preferred_element_type=F32)

    for a_ref, (k0, k1) in zip(a_refs, k_ranges):
        if k0 == 0:
            pl.when(k == 0)(functools.partial(first_step, a_ref))
            k0 = 1
        if k1 > k0:
            pl.when((k >= k0) & (k < k1))(functools.partial(later_step, a_ref))


MM_TILE = 1024


def _matmul_vmem_mib(tm, tn, tk, n_a, has_res, has_epilogue, out_itemsize):
    windows = n_a * tm * tk * 2 + tk * tn * 2 + tm * tn * (out_itemsize + (4 if has_res else 0))
    temporaries = tm * tn * 4 * (3 if has_epilogue else 2)
    return -(-(2 * windows + temporaries) // MIB)


def _matmul(a_list, w, residual=None, *, n, w_col0=0, tm=MM_TILE, tn=MM_TILE, tk=D_MODEL, name, epilogue=None,
            aux=(), out_dtype=F32):
    m = a_list[0].shape[0]
    vmem_mib = _matmul_vmem_mib(tm, tn, tk, len(a_list), residual is not None, epilogue is not None,
                                jnp.dtype(out_dtype).itemsize)
    k_ranges, k0 = [], 0
    for a in a_list:
        k_ranges.append((k0, k0 + a.shape[1] // tk))
        k0 = k_ranges[-1][1]
    nk = k0
    assert nk * tk == w.shape[0] and w_col0 % tn == 0
    assert nk == 1 or (epilogue is None and out_dtype == F32)
    in_specs = [pl.BlockSpec((tm, tk), lambda i, j, k, k0=k0, k1=k1: (i, jnp.clip(k - k0, 0, k1 - k0 - 1)))
                for k0, k1 in k_ranges]
    in_specs.append(pl.BlockSpec((tk, tn), lambda i, j, k: (k, w_col0 // tn + j)))
    args = list(a_list) + [w]
    if residual is not None:
        in_specs.append(pl.BlockSpec((tm, tn), lambda i, j, k: (i, j)))
        args.append(residual)
    for x in aux:
        in_specs.append(pl.BlockSpec((x.shape[0], tn), lambda i, j, k: (0, j)))
        args.append(x)
    return pl.pallas_call(
        functools.partial(_matmul_kernel, k_ranges=tuple(k_ranges), has_res=residual is not None,
                          n_aux=len(aux), epilogue=epilogue),
        grid=(m // tm, n // tn, nk),
        in_specs=in_specs,
        out_specs=pl.BlockSpec((tm, tn), lambda i, j, k: (i, j)),
        out_shape=jax.ShapeDtypeStruct((m, n), out_dtype),
        compiler_params=_cparams(("parallel", "parallel", "arbitrary"), vmem_mib),
        name=name,
    )(*args)


HG_SLAB = 512
ROW_SLAB = 16


def _hgrn_logf(f_pre, lb_logits):
    e = jnp.exp(lb_logits - jnp.max(lb_logits, axis=0, keepdims=True))
    lb = e[0:1] / jnp.sum(e, axis=0, keepdims=True)
    half_k = 0.5 * (1.0 - lb)
    return jnp.log(1.0 - (half_k - half_k * jnp.tanh(0.5 * f_pre)))


def _hgrn_kernel(sq_ref, lf_ref, v_ref, sg_ref, ob_ref, nw_ref, o_ref,
                 st_ref, acc_ref, qin_ref, kin_ref, qout_ref, kst_ref, gl_ref, *, tb):
    @pl.when(pl.program_id(1) == 0)
    def _():
        st_ref[...] = jnp.zeros_like(st_ref)

    hg_width = HG_HEADS * HG_DK
    o_ref[:, hg_width:] = ob_ref[...]
    n_chunks = tb // CHUNK
    n_slabs = hg_width // HG_SLAB
    heads_per_slab = HG_SLAB // HG_DK
    tri = _tri(CHUNK)
    tri_bf = tri.astype(BF16)
    nt = (((1,), (1,)), ((), ()))
    tn = (((0,), (0,)), ((), ()))

    def phase_a(j, p):
        cols = pl.ds(pl.multiple_of(j * HG_SLAB, HG_SLAB), HG_SLAB)
        for c in range(n_chunks):
            rows = pl.ds(c * CHUNK, CHUNK)
            logf = lf_ref[rows, cols]
            k = 1.0 - jnp.exp(logf)
            b = _chunk_cumsum(tri_bf, logf, terms=2)
            ref = b[CHUNK // 2 - 1:CHUNK // 2]
            blast = b[CHUNK - 1:CHUNK]
            d = b - ref
            q_in = sq_ref[rows, cols] * jnp.exp(d)
            k_in = k * jnp.exp(-d)
            qin_ref[p, rows, :] = q_in.astype(BF16)
            kin_ref[p, rows, :] = k_in.astype(BF16)
            qout_ref[p, rows, :] = (q_in * jnp.exp(ref)).astype(BF16)
            kst_ref[p, rows, :] = (k_in * jnp.exp(blast - ref)).astype(BF16)
            gl_ref[p, c:c + 1, :] = jnp.exp(blast)

    def phase_b(j, p):
        states = [st_ref[j * heads_per_slab + hh] for hh in range(heads_per_slab)]
        for hh in range(heads_per_slab):
            lanes = slice(hh * HG_DK, (hh + 1) * HG_DK)
            out_cols = pl.ds(pl.multiple_of((j * heads_per_slab + hh) * HG_DK, HG_DK), HG_DK)
            s_t = states[hh]
            for c in range(n_chunks):
                rows = pl.ds(c * CHUNK, CHUNK)
                v = v_ref[rows, out_cols]
                scores = lax.dot_general(qin_ref[p, rows, lanes], kin_ref[p, rows, lanes], nt,
                                         preferred_element_type=F32)
                scores = jnp.where(tri, scores, 0.0).astype(BF16)
                lhs = jnp.concatenate([qout_ref[p, rows, lanes], scores], axis=1)
                rhs = jnp.concatenate([s_t.astype(BF16).T, v], axis=0)
                acc_ref[rows, out_cols] = jnp.dot(lhs, rhs, preferred_element_type=F32)
                s_t = (s_t * gl_ref[p, c:c + 1, lanes]
                       + lax.dot_general(v, kst_ref[p, rows, lanes], tn, preferred_element_type=F32))
            states[hh] = s_t
        for hh in range(heads_per_slab):
            st_ref[j * heads_per_slab + hh] = states[hh]

    phase_a(0, 0)

    def skew_body(jj, carry):
        j = 2 * jj
        phase_a(j + 1, 1)
        phase_b(j, 0)
        phase_a(j + 2, 0)
        phase_b(j + 1, 1)
        return carry

    lax.fori_loop(0, n_slabs // 2 - 1, skew_body, 0)
    phase_a(n_slabs - 1, 1)
    phase_b(n_slabs - 2, 0)
    phase_b(n_slabs - 1, 1)

    def norm_body(r, carry):
        rows = pl.ds(pl.multiple_of(r * ROW_SLAB, ROW_SLAB), ROW_SLAB)
        _gated_rmsnorm_rows(acc_ref, sg_ref, nw_ref, o_ref, rows, gate_first=False)
        return carry

    lax.fori_loop(0, tb // ROW_SLAB, norm_body, 0)


def _hgrn(sq, lf, v, sg, o_b, a_norm_w, bsz, seq, tb=128):
    width = HG_HEADS * HG_DK
    nb = seq // tb
    rows = pl.BlockSpec((tb, width), lambda b, s: (b * nb + s, 0))
    rows_b = pl.BlockSpec((tb, o_b.shape[1]), lambda b, s: (b * nb + s, 0))
    return pl.pallas_call(
        functools.partial(_hgrn_kernel, tb=tb),
        grid=(bsz, nb),
        in_specs=[rows, rows, rows, rows, rows_b, pl.BlockSpec((1, width), lambda b, s: (0, 0))],
        out_specs=pl.BlockSpec((tb, width + o_b.shape[1]), lambda b, s: (b * nb + s, 0)),
        out_shape=jax.ShapeDtypeStruct((bsz * seq, width + o_b.shape[1]), BF16),
        scratch_shapes=[pltpu.VMEM((HG_HEADS, HG_DK, HG_DK), F32),
                        pltpu.VMEM((tb, width), F32)]
                       + [pltpu.VMEM((2, tb, HG_SLAB), BF16)] * 4
                       + [pltpu.VMEM((2, tb // CHUNK, HG_SLAB), F32)],
        compiler_params=_cparams(("parallel", "arbitrary"), 40),
        name="hgrn2",
    )(sq, lf, v, sg, o_b, a_norm_w.reshape(1, width))


LRU_TILE = 512


def _lru_block(xc, gate_in, wg, ba, bx, lam, h, row):
    gates = jnp.dot(xc.astype(BF16), wg, preferred_element_type=F32)
    quarter = (0.25 * LRU_C) * _softplus(-lam)
    t_r = jnp.tanh(0.5 * (gates[:, :LRU_BLOCK] + ba))
    tau = jnp.tanh(-quarter - quarter * t_r)
    rho = 1.0 / (1.0 - tau)
    a = (1.0 + tau) * rho
    neg_tau = -tau
    root = jnp.where(neg_tau > 0.0, neg_tau * lax.rsqrt(neg_tau), 0.0)
    t_i = jnp.tanh(0.5 * (gates[:, LRU_BLOCK:] + bx))
    u = (rho * root) * ((1.0 + t_i) * xc)
    outs = []
    for r8 in range(xc.shape[0] // SUBLANES):
        a8 = a[r8 * SUBLANES:(r8 + 1) * SUBLANES]
        u8 = u[r8 * SUBLANES:(r8 + 1) * SUBLANES]
        for sh in (1, 2, 4):
            keep = row >= sh
            a_s = jnp.where(keep, pltpu.roll(a8, sh, 0), 1.0)
            u_s = jnp.where(keep, pltpu.roll(u8, sh, 0), 0.0)
            u8 = a8 * u_s + u8
            a8 = a8 * a_s
        h8 = a8 * h + u8
        h = h8[SUBLANES - 1:SUBLANES]
        outs.append(h8)
    half_g = 0.5 * gate_in
    return (jnp.concatenate(outs, axis=0) * half_g) * (1.0 + jnp.tanh(half_g)), h


def _lru_proj_kernel(a_ref, wxb_ref, wgb_ref, cw_ref, cb_ref, wg_ref, bg_ref, lam_ref, o_ref,
                     p_ref, hist_ref, h_ref, *, tm, tiles_per_seq):
    i = pl.program_id(0)
    j = pl.program_id(1)

    @pl.when(lax.rem(i, tiles_per_seq) == 0)
    def _():
        hist_ref[j] = jnp.zeros(hist_ref.shape[1:], F32)
        h_ref[j] = jnp.zeros(h_ref.shape[1:], F32)

    a = a_ref[...]
    p_ref[:, 0:LRU_TILE] = jnp.dot(a, wxb_ref[...], preferred_element_type=F32)
    p_ref[:, LRU_TILE:2 * LRU_TILE] = jnp.dot(a, wgb_ref[...], preferred_element_type=F32)
    row = lax.broadcasted_iota(jnp.int32, (SUBLANES, LRU_BLOCK), 0)
    hist = hist_ref[j]
    h_in = h_ref[j]
    h_out = []
    for blk in range(LRU_TILE // LRU_BLOCK):
        lanes = slice(blk * LRU_BLOCK, (blk + 1) * LRU_BLOCK)
        n = j * (LRU_TILE // LRU_BLOCK) + blk
        cols = pl.ds(pl.multiple_of(n * LRU_BLOCK, LRU_BLOCK), LRU_BLOCK)
        xc = jnp.concatenate(
            [acc for _, acc in _causal_conv_rows(hist[:, lanes], p_ref, lanes, tm, cw_ref[:, cols], cb_ref[:, cols])],
            axis=0)
        out, h_last = _lru_block(xc, p_ref[:, LRU_TILE + blk * LRU_BLOCK:LRU_TILE + (blk + 1) * LRU_BLOCK],
                                 wg_ref[n], bg_ref[0:1, cols], bg_ref[1:2, cols], lam_ref[:, cols],
                                 h_in[:, lanes], row)
        o_ref[:, lanes] = out.astype(o_ref.dtype)
        h_out.append(h_last)
    hist_ref[j] = p_ref[tm - SUBLANES:tm, 0:LRU_TILE]
    h_ref[j] = jnp.concatenate(h_out, axis=1)


def _lru_proj(hn, w, xb_col0, gb_col0, conv_w, conv_b, wa, ba, wx, bx, lam, seq, tm=1024):
    m, k = hn.shape
    width = LRU_BLOCKS * LRU_BLOCK
    n_tiles = width // LRU_TILE
    assert xb_col0 % LRU_TILE == 0 and gb_col0 % LRU_TILE == 0
    assert seq % tm == 0, "row tiles must not straddle sequences"
    wg = jnp.concatenate([wa, wx], axis=-1).astype(BF16)
    bg = jnp.stack([ba, bx], axis=0)

    def full(shape):
        return pl.BlockSpec(shape, lambda i, j: (0,) * len(shape))

    return pl.pallas_call(
        functools.partial(_lru_proj_kernel, tm=tm, tiles_per_seq=seq // tm),
        grid=(m // tm, n_tiles),
        in_specs=[pl.BlockSpec((tm, k), lambda i, j: (i, 0)),
                  pl.BlockSpec((k, LRU_TILE), lambda i, j: (0, xb_col0 // LRU_TILE + j)),
                  pl.BlockSpec((k, LRU_TILE), lambda i, j: (0, gb_col0 // LRU_TILE + j)),
                  full(conv_w.shape), full((1, width)), full(wg.shape), full(bg.shape), full((1, width))],
        out_specs=pl.BlockSpec((tm, LRU_TILE), lambda i, j: (i, j)),
        out_shape=jax.ShapeDtypeStruct((m, width), BF16),
        scratch_shapes=[pltpu.VMEM((tm, 2 * LRU_TILE), F32),
                        pltpu.VMEM((n_tiles, SUBLANES, LRU_TILE), F32),
                        pltpu.VMEM((n_tiles, 1, LRU_TILE), F32)],
        compiler_params=_cparams(("arbitrary", "arbitrary"), 48),
        name="even_in_proj_rglru",
    )(hn, w, w, conv_w, conv_b.reshape(1, width), wg, bg, lam.reshape(1, width))


def _ssd_kernel(sz_ref, xbc_ref, dt_ref, cw_ref, cb_ref, dtb_ref, alog_ref, dsk_ref, nw_ref, pidx_ref, o_ref,
                hist_ref, xc_ref, st_ref, st2_ref, y_ref, tt_ref, cs_ref, cbd_ref, bt_ref, e_ref, *, tb):
    n_chunks = tb // CHUNK

    @pl.when(pl.program_id(1) == 0)
    def _():
        hist_ref[...] = jnp.zeros_like(hist_ref)
        st_ref[...] = jnp.zeros_like(st_ref)

    tri_bf = _tri(CHUNK).astype(BF16)
    lane = lax.broadcasted_iota(jnp.int32, (CHUNK, LANES), 1)
    rowi = lax.broadcasted_iota(jnp.int32, (CHUNK, LANES), 0)
    lane_hi = (lane >= SSD_HEADDIM).astype(jnp.int32)
    causal2 = rowi >= (lane - SSD_HEADDIM * lane_hi)
    lane1_lo = lax.broadcasted_iota(jnp.int32, (1, LANES), 1) < SSD_HEADDIM
    lane_lo = lane < SSD_HEADDIM
    nt = (((1,), (1,)), ((), ()))
    a_neg = -jnp.exp(alog_ref[...])

    for c in range(n_chunks):
        dt = _softplus(dt_ref[c * CHUNK:(c + 1) * CHUNK, :] + dtb_ref[...])
        cs = _chunk_cumsum(tri_bf, dt * a_neg, terms=3)
        w = dt * jnp.exp(cs[CHUNK - 1:CHUNK] - cs)
        cs_ref[c] = cs
        for t, arr in enumerate((cs - jnp.log(dt), w)):
            tt_ref[c, t] = jnp.concatenate([arr, arr], axis=0).T

    pairs_per_job = PAIRS_PER_GROUP
    jobs_per_chunk = SSD_HEADS // 2 // pairs_per_job

    def expand_pairs(c, first_pair):
        cs = cs_ref[c]
        for i in range(pairs_per_job):
            pair = first_pair + i
            idx = jnp.broadcast_to(pidx_ref[pl.ds(pair, 1), :], (CHUNK, LANES))
            e_ref[c, :, pl.ds(pl.multiple_of(pair * LANES, LANES), LANES)] = jnp.take_along_axis(
                cs, idx, axis=1, mode="promise_in_bounds")

    slab = 512
    n_expand = n_chunks * jobs_per_chunk
    assert SSD_CONV_DIM // slab >= n_expand

    def conv_body(j, carry):
        job = lax.rem(j, n_expand)
        expand_pairs(job // jobs_per_chunk, lax.rem(job, jobs_per_chunk) * pairs_per_job)
        cols = pl.ds(pl.multiple_of(j * slab, slab), slab)
        for r0, acc in _causal_conv_rows(hist_ref[:, cols], xbc_ref, cols, tb, cw_ref[:, cols], cb_ref[:, cols]):
            xc_ref[r0:r0 + SUBLANES, cols] = _silu(acc)
        return carry

    lax.fori_loop(0, SSD_CONV_DIM // slab, conv_body, 0)

    for c in range(n_chunks):
        for g in range(SSD_GROUPS):
            b_g = xc_ref[c * CHUNK:(c + 1) * CHUNK, SSD_INNER + g * SSD_STATE:SSD_INNER + (g + 1) * SSD_STATE]
            c_g = xc_ref[c * CHUNK:(c + 1) * CHUNK,
                         SSD_INNER + SSD_BC + g * SSD_STATE:SSD_INNER + SSD_BC + (g + 1) * SSD_STATE]
            b2 = jnp.concatenate([b_g, b_g], axis=0)
            cbd_ref[c, g] = lax.dot_general(c_g.astype(BF16), b2.astype(BF16), nt,
                                            preferred_element_type=F32)
            bt_ref[c, g] = b2.T

    for c in range(n_chunks):
        r0 = c * CHUNK
        st_in, st_out = (st_ref, st2_ref) if c % 2 == 0 else (st2_ref, st_ref)

        def group_body(g, carry, c=c, r0=r0, st_in=st_in, st_out=st_out):
            c_g = xc_ref[r0:r0 + CHUNK,
                         pl.ds(pl.multiple_of(SSD_INNER + SSD_BC + g * SSD_STATE, SSD_STATE), SSD_STATE)]
            c_bf = c_g.astype(BF16)
            cb2 = cbd_ref[c, g]
            bt2 = bt_ref[c, g]
            for i in range(PAIRS_PER_GROUP):
                h_a = g * (2 * PAIRS_PER_GROUP) + 2 * i
                cols = pl.ds(pl.multiple_of(h_a * SSD_HEADDIM, LANES), LANES)

                def pair_row(t, h_a=h_a):
                    return jnp.where(lane1_lo, tt_ref[c, t, pl.ds(h_a, 1), :], tt_ref[c, t, pl.ds(h_a + 1, 1), :])

                e2 = e_ref[c, :, cols]
                att = cb2 * jnp.where(causal2, jnp.exp(e2 - pair_row(0)), 0.0)
                lhs = jnp.concatenate([att, bt2 * pair_row(1)], axis=0).astype(BF16)
                x2 = xc_ref[r0:r0 + CHUNK, cols]
                rhs = jnp.concatenate([jnp.where(lane_lo, x2, 0.0), jnp.where(lane_lo, 0.0, x2)],
                                      axis=0).astype(BF16)
                res = jnp.dot(lhs, rhs, preferred_element_type=F32)
                s_p = st_in[:, cols]
                e_out = jnp.exp(e2)
                y_off = e_out * jnp.dot(c_bf, s_p.astype(BF16), preferred_element_type=F32)
                y_ref[r0:r0 + CHUNK, cols] = res[0:CHUNK] + y_off + x2 * dsk_ref[:, cols]
                st_out[:, cols] = s_p * e_out[CHUNK - 1:CHUNK] + res[CHUNK:3 * CHUNK]
            return carry

        lax.fori_loop(0, SSD_GROUPS, group_body, 0)

    hist_ref[...] = xbc_ref[tb - SUBLANES:tb, :]

    def norm_body(r, carry):
        rows = pl.ds(pl.multiple_of(r * ROW_SLAB, ROW_SLAB), ROW_SLAB)
        _gated_rmsnorm_rows(y_ref, sz_ref, nw_ref, o_ref, rows, gate_first=True)
        return carry

    lax.fori_loop(0, tb // ROW_SLAB, norm_body, 0)


def _ssd(sz, xbc, dt, conv_w, conv_b, dt_bias, a_log, d_skip, norm_w, bsz, seq, tb=128):
    assert (tb // CHUNK) % 2 == 0, "the state ping-pong needs an even number of chunks per block"
    nb = seq // tb
    dsk = jnp.repeat(d_skip, SSD_HEADDIM).reshape(1, SSD_INNER)
    pair_idx = (2 * jnp.arange(SSD_HEADS // 2, dtype=jnp.int32)[:, None]
                + (jnp.arange(LANES, dtype=jnp.int32) // SSD_HEADDIM)[None, :])

    def rows(width):
        return pl.BlockSpec((tb, width), lambda b, s: (b * nb + s, 0))

    def full(shape):
        return pl.BlockSpec(shape, lambda b, s: (0,) * len(shape))

    return pl.pallas_call(
        functools.partial(_ssd_kernel, tb=tb),
        grid=(bsz, nb),
        in_specs=[rows(SSD_INNER), rows(SSD_CONV_DIM), rows(SSD_HEADS),
                  full(conv_w.shape), full((1, SSD_CONV_DIM)),
                  full((1, SSD_HEADS)), full((1, SSD_HEADS)), full((1, SSD_INNER)), full((1, SSD_INNER)),
                  full((SSD_HEADS // 2, LANES))],
        out_specs=rows(SSD_INNER),
        out_shape=jax.ShapeDtypeStruct((bsz * seq, SSD_INNER), BF16),
        scratch_shapes=[pltpu.VMEM((SUBLANES, SSD_CONV_DIM), F32),
                        pltpu.VMEM((tb, SSD_CONV_DIM), F32),
                        pltpu.VMEM((SSD_STATE, SSD_INNER), F32),
                        pltpu.VMEM((SSD_STATE, SSD_INNER), F32),
                        pltpu.VMEM((tb, SSD_INNER), F32),
                        pltpu.VMEM((tb // CHUNK, 2, SSD_HEADS, LANES), F32),
                        pltpu.VMEM((tb // CHUNK, CHUNK, SSD_HEADS), F32),
                        pltpu.VMEM((tb // CHUNK, SSD_GROUPS, CHUNK, LANES), F32),
                        pltpu.VMEM((tb // CHUNK, SSD_GROUPS, SSD_STATE, LANES), F32),
                        pltpu.VMEM((tb // CHUNK, CHUNK, SSD_INNER), F32)],
        compiler_params=_cparams(("parallel", "arbitrary"), 56),
        name="ssd",
    )(sz, xbc, dt, conv_w, conv_b.reshape(1, SSD_CONV_DIM), dt_bias.reshape(1, SSD_HEADS),
      a_log.reshape(1, SSD_HEADS), dsk, norm_w.reshape(1, SSD_INNER), pair_idx)


def _even_layer(h, norm_w, w_in, lb_logits, a_norm_w, conv_w, conv_b, wa, ba, wx, bx, lam, w_out, bsz, seq):
    hn = _rmsnorm(h, norm_w, BF16)
    hw = HG_HEADS * HG_DK
    lru = LRU_BLOCKS * LRU_BLOCK
    w_in = w_in.astype(BF16)
    proj = functools.partial(_matmul, [hn], w_in, n=hw)
    sq = proj(w_col0=0, epilogue=_silu, name="even_in_proj_q")
    lf = proj(w_col0=hw, epilogue=_hgrn_logf, aux=(lb_logits,), name="even_in_proj_f")
    v = proj(w_col0=2 * hw, out_dtype=BF16, name="even_in_proj_i")
    sg = proj(w_col0=3 * hw, epilogue=_silu, name="even_in_proj_ga")
    o_b = _lru_proj(hn, w_in, 4 * hw, 4 * hw + lru, conv_w, conv_b, wa, ba, wx, bx, lam, seq)
    mix = _hgrn(sq, lf, v, sg, o_b, a_norm_w, bsz, seq)
    return _matmul([mix], w_out.astype(BF16), residual=h, n=D_MODEL, name="even_out_proj")


def _odd_layer(h, norm_w, w_in, conv_w, conv_b, dt_bias, a_log, d_skip, ssd_norm_w, w_out, bsz, seq):
    hn = _rmsnorm(h, norm_w, BF16)
    w_in = w_in.astype(BF16)
    sz = _matmul([hn], w_in, n=SSD_INNER, name="odd_in_proj_z", epilogue=_silu)
    xbc = _matmul([hn], w_in, n=SSD_CONV_DIM, w_col0=SSD_INNER, name="odd_in_proj_xbc")
    dt = _matmul([hn], w_in, n=SSD_HEADS, w_col0=SSD_INNER + SSD_CONV_DIM, tn=SSD_HEADS, name="odd_in_proj_dt")
    y = _ssd(sz, xbc, dt, conv_w, conv_b, dt_bias, a_log, d_skip, ssd_norm_w, bsz, seq)
    return _matmul([y], w_out.astype(BF16), residual=h, n=D_MODEL, name="odd_out_proj")


def kernel(x, norm_w, e_w_in, lb_logits, e_a_norm_w, e_conv_w, e_conv_b, e_wa, e_ba, e_wx, e_bx, e_lambda,
           e_w_out, o_w_in, o_conv_w, o_conv_b, o_dt_bias, o_a_log, o_d, o_norm_w, o_w_out, final_norm_w):
    bsz, seq, d = x.shape
    h = x.reshape(bsz * seq, d)
    h = _even_layer(h, norm_w[0], e_w_in[0], lb_logits, e_a_norm_w[0], e_conv_w[0], e_conv_b[0], e_wa[0],
                    e_ba[0], e_wx[0], e_bx[0], e_lambda[0], e_w_out[0], bsz, seq)
    h = _odd_layer(h, norm_w[1], o_w_in[0], o_conv_w[0], o_conv_b[0], o_dt_bias[0], o_a_log[0], o_d[0],
                   o_norm_w[0], o_w_out[0], bsz, seq)
    return _rmsnorm(h, final_norm_w, x.dtype).reshape(bsz, seq, d)
```

```python
import functools

import jax
import jax.numpy as jnp
from jax import lax
from jax.experimental import pallas as pl
from jax.experimental.pallas import tpu as pltpu

F32 = jnp.float32
BF16 = jnp.bfloat16

EPS = 1e-6
CHUNK = 64
LANES = 128
SUBLANES = 8
D_MODEL = 4096
HG_HEADS = 32
HG_DK = 128
LRU_BLOCKS = 32
LRU_BLOCK = 128
LRU_C = 8.0
CONV_W = 4
SSD_INNER = 8192
SSD_HEADS = 128
SSD_HEADDIM = 64
SSD_GROUPS = 8
SSD_STATE = 128
SSD_BC = SSD_GROUPS * SSD_STATE
SSD_CONV_DIM = SSD_INNER + 2 * SSD_BC
PAIRS_PER_GROUP = SSD_HEADS // SSD_GROUPS // 2
MIB = 1024 * 1024


def _cparams(semantics, vmem_mib):
    return pltpu.CompilerParams(dimension_semantics=semantics, vmem_limit_bytes=vmem_mib * MIB)


def _sigmoid(x):
    return 0.5 + 0.5 * jnp.tanh(0.5 * x)


def _silu(x):
    return x * _sigmoid(x)


def _softplus(x):
    return jnp.maximum(x, 0.0) + jnp.log1p(jnp.exp(-jnp.abs(x)))


def _chunk_cumsum(tri, x, terms):
    acc = None
    rem = x
    for t in range(terms):
        piece = rem.astype(BF16)
        part = jnp.dot(tri, piece, preferred_element_type=F32)
        acc = part if acc is None else acc + part
        if t + 1 < terms:
            rem = rem - piece.astype(F32)
    return acc


def _causal_conv_rows(hist, x_ref, cols, n_rows, w, b):
    row = lax.broadcasted_iota(jnp.int32, hist.shape, 0)
    shifts = range(1, CONV_W)
    taps = [jnp.broadcast_to(w[CONV_W - 1 - s:CONV_W - s], hist.shape) for s in range(CONV_W)]
    bias = jnp.broadcast_to(b, hist.shape)
    prev_rot = [pltpu.roll(hist, s, 0) for s in shifts]
    for r0 in range(0, n_rows, SUBLANES):
        cur = x_ref[r0:r0 + SUBLANES, cols]
        cur_rot = [pltpu.roll(cur, s, 0) for s in shifts]
        out = bias + cur * taps[0]
        for s, c_rot, p_rot in zip(shifts, cur_rot, prev_rot):
            out = out + jnp.where(row >= s, c_rot, p_rot) * taps[s]
        prev_rot = cur_rot
        yield r0, out


NORM_LANES = 512


def _gated_rmsnorm_rows(y_ref, g_ref, w_ref, o_ref, rows, gate_first):
    width = y_ref.shape[1]
    chunks = [slice(c0, c0 + NORM_LANES) for c0 in range(0, width, NORM_LANES)]
    ssq = None
    for cs in chunks:
        v = y_ref[rows, cs] * g_ref[rows, cs] if gate_first else y_ref[rows, cs]
        ssq = v * v if ssq is None else ssq + v * v
    scale = lax.rsqrt(jnp.sum(ssq, axis=-1, keepdims=True) * (1.0 / width) + EPS)
    for cs in chunks:
        v = y_ref[rows, cs] * g_ref[rows, cs]
        o_ref[rows, cs] = (v * scale * w_ref[:, cs]).astype(o_ref.dtype)


def _tri(n):
    r = lax.broadcasted_iota(jnp.int32, (n, n), 0)
    c = lax.broadcasted_iota(jnp.int32, (n, n), 1)
    return r >= c


def _norm_kernel(x_ref, w_ref, o_ref):
    x = x_ref[...]
    ms = jnp.mean(x * x, axis=-1, keepdims=True)
    o_ref[...] = (x * lax.rsqrt(ms + EPS) * w_ref[...]).astype(o_ref.dtype)


def _rmsnorm(x, w, out_dtype, tm=256):
    m, d = x.shape
    return pl.pallas_call(
        _norm_kernel,
        grid=(m // tm,),
        in_specs=[pl.BlockSpec((tm, d), lambda i: (i, 0)),
                  pl.BlockSpec((1, d), lambda i: (0, 0))],
        out_specs=pl.BlockSpec((tm, d), lambda i: (i, 0)),
        out_shape=jax.ShapeDtypeStruct((m, d), out_dtype),
        compiler_params=_cparams(("parallel",), 32),
        name="rmsnorm",
    )(x, w.reshape(1, d))


def _matmul_kernel(*refs, k_ranges, has_res, n_aux, epilogue):
    n_a = len(k_ranges)
    a_refs = refs[:n_a]
    b_ref = refs[n_a]
    r_ref = refs[n_a + 1] if has_res else None
    aux_refs = refs[len(refs) - 1 - n_aux:len(refs) - 1]
    o_ref = refs[-1]
    nk = k_ranges[-1][1]
    if nk == 1:
        p = jnp.dot(a_refs[0][...], b_ref[...], preferred_element_type=F32)
        if has_res:
            p = p + r_ref[...]
        if epilogue is not None:
            p = epilogue(p, *[r[...] for r in aux_refs])
        o_ref[...] = p.astype(o_ref.dtype)
        return
    k = pl.program_id(2)

    def first_step(a_ref):
        p = jnp.dot(a_ref[...], b_ref[...], preferred_element_type=F32)
        o_ref[...] = p + r_ref[...] if has_res else p

    def later_step(a_ref):
        o_ref[...] += jnp.dot(a_ref[...], b_ref[...], preferred_element_type=F32)

    for a_ref, (k0, k1) in zip(a_refs, k_ranges):
        if k0 == 0:
            pl.when(k == 0)(functools.partial(first_step, a_ref))
            k0 = 1
        if k1 > k0:
            pl.when((k >= k0) & (k < k1))(functools.partial(later_step, a_ref))


MM_TILE = 1024


def _matmul_vmem_mib(tm, tn, tk, n_a, has_res, has_epilogue, out_itemsize):
    windows = n_a * tm * tk * 2 + tk * tn * 2 + tm * tn * (out_itemsize + (4 if has_res else 0))
    temporaries = tm * tn * 4 * (3 if has_epilogue else 2)
    return -(-(2 * windows + temporaries) // MIB)


def _matmul(a_list, w, residual=None, *, n, w_col0=0, tm=MM_TILE, tn=MM_TILE, tk=D_MODEL, name, epilogue=None,
            aux=(), out_dtype=F32):
    m = a_list[0].shape[0]
    vmem_mib = _matmul_vmem_mib(tm, tn, tk, len(a_list), residual is not None, epilogue is not None,
                                jnp.dtype(out_dtype).itemsize)
    k_ranges, k0 = [], 0
    for a in a_list:
        k_ranges.append((k0, k0 + a.shape[1] // tk))
        k0 = k_ranges[-1][1]
    nk = k0
    assert nk * tk == w.shape[0] and w_col0 % tn == 0
    assert nk == 1 or (epilogue is None and out_dtype == F32)
    in_specs = [pl.BlockSpec((tm, tk), lambda i, j, k, k0=k0, k1=k1: (i, jnp.clip(k - k0, 0, k1 - k0 - 1)))
                for k0, k1 in k_ranges]
    in_specs.append(pl.BlockSpec((tk, tn), lambda i, j, k: (k, w_col0 // tn + j)))
    args = list(a_list) + [w]
    if residual is not None:
        in_specs.append(pl.BlockSpec((tm, tn), lambda i, j, k: (i, j)))
        args.append(residual)
    for x in aux:
        in_specs.append(pl.BlockSpec((x.shape[0], tn), lambda i, j, k: (0, j)))
        args.append(x)
    return pl.pallas_call(
        functools.partial(_matmul_kernel, k_ranges=tuple(k_ranges), has_res=residual is not None,
                          n_aux=len(aux), epilogue=epilogue),
        grid=(m // tm, n // tn, nk),
        in_specs=in_specs,
        out_specs=pl.BlockSpec((tm, tn), lambda i, j, k: (i, j)),
        out_shape=jax.ShapeDtypeStruct((m, n), out_dtype),
        compiler_params=_cparams(("parallel", "parallel", "arbitrary"), vmem_mib),
        name=name,
    )(*args)


HG_SLAB = 512
ROW_SLAB = 16


def _hgrn_logf(f_pre, lb_logits):
    e = jnp.exp(lb_logits - jnp.max(lb_logits, axis=0, keepdims=True))
    lb = e[0:1] / jnp.sum(e, axis=0, keepdims=True)
    half_k = 0.5 * (1.0 - lb)
    return jnp.log(1.0 - (half_k - half_k * jnp.tanh(0.5 * f_pre)))


def _hgrn_kernel(sq_ref, lf_ref, v_ref, sg_ref, ob_ref, nw_ref, o_ref,
                 st_ref, acc_ref, qin_ref, kin_ref, qout_ref, kst_ref, gl_ref, *, tb):
    @pl.when(pl.program_id(1) == 0)
    def _():
        st_ref[...] = jnp.zeros_like(st_ref)

    hg_width = HG_HEADS * HG_DK
    o_ref[:, hg_width:] = ob_ref[...]
    n_chunks = tb // CHUNK
    n_slabs = hg_width // HG_SLAB
    heads_per_slab = HG_SLAB // HG_DK
    tri = _tri(CHUNK)
    tri_bf = tri.astype(BF16)
    nt = (((1,), (1,)), ((), ()))
    tn = (((0,), (0,)), ((), ()))

    def phase_a(j, p):
        cols = pl.ds(pl.multiple_of(j * HG_SLAB, HG_SLAB), HG_SLAB)
        for c in range(n_chunks):
            rows = pl.ds(c * CHUNK, CHUNK)
            logf = lf_ref[rows, cols]
            k = 1.0 - jnp.exp(logf)
            b = _chunk_cumsum(tri_bf, logf, terms=2)
            ref = b[CHUNK // 2 - 1:CHUNK // 2]
            blast = b[CHUNK - 1:CHUNK]
            d = b - ref
            q_in = sq_ref[rows, cols] * jnp.exp(d)
            k_in = k * jnp.exp(-d)
            qin_ref[p, rows, :] = q_in.astype(BF16)
            kin_ref[p, rows, :] = k_in.astype(BF16)
            qout_ref[p, rows, :] = (q_in * jnp.exp(ref)).astype(BF16)
            kst_ref[p, rows, :] = (k_in * jnp.exp(blast - ref)).astype(BF16)
            gl_ref[p, c:c + 1, :] = jnp.exp(blast)

    def phase_b(j, p):
        states = [st_ref[j * heads_per_slab + hh] for hh in range(heads_per_slab)]
        for hh in range(heads_per_slab):
            lanes = slice(hh * HG_DK, (hh + 1) * HG_DK)
            out_cols = pl.ds(pl.multiple_of((j * heads_per_slab + hh) * HG_DK, HG_DK), HG_DK)
            s_t = states[hh]
            for c in range(n_chunks):
                rows = pl.ds(c * CHUNK, CHUNK)
                v = v_ref[rows, out_cols]
                scores = lax.dot_general(qin_ref[p, rows, lanes], kin_ref[p, rows, lanes], nt,
                                         preferred_element_type=F32)
                scores = jnp.where(tri, scores, 0.0).astype(BF16)
                lhs = jnp.concatenate([qout_ref[p, rows, lanes], scores], axis=1)
                rhs = jnp.concatenate([s_t.astype(BF16).T, v], axis=0)
                acc_ref[rows, out_cols] = jnp.dot(lhs, rhs, preferred_element_type=F32)
                s_t = (s_t * gl_ref[p, c:c + 1, lanes]
                       + lax.dot_general(v, kst_ref[p, rows, lanes], tn, preferred_element_type=F32))
            states[hh] = s_t
        for hh in range(heads_per_slab):
            st_ref[j * heads_per_slab + hh] = states[hh]

    phase_a(0, 0)

    def skew_body(jj, carry):
        j = 2 * jj
        phase_a(j + 1, 1)
        phase_b(j, 0)
        phase_a(j + 2, 0)
        phase_b(j + 1, 1)
        return carry

    lax.fori_loop(0, n_slabs // 2 - 1, skew_body, 0)
    phase_a(n_slabs - 1, 1)
    phase_b(n_slabs - 2, 0)
    phase_b(n_slabs - 1, 1)

    def norm_body(r, carry):
        rows = pl.ds(pl.multiple_of(r * ROW_SLAB, ROW_SLAB), ROW_SLAB)
        _gated_rmsnorm_rows(acc_ref, sg_ref, nw_ref, o_ref, rows, gate_first=False)
        return carry

    lax.fori_loop(0, tb // ROW_SLAB, norm_body, 0)


def _hgrn(sq, lf, v, sg, o_b, a_norm_w, bsz, seq, tb=256):
    width = HG_HEADS * HG_DK
    nb = seq // tb
    rows = pl.BlockSpec((tb, width), lambda b, s: (b * nb + s, 0))
    rows_b = pl.BlockSpec((tb, o_b.shape[1]), lambda b, s: (b * nb + s, 0))
    return pl.pallas_call(
        functools.partial(_hgrn_kernel, tb=tb),
        grid=(bsz, nb),
        in_specs=[rows, rows, rows, rows, rows_b, pl.BlockSpec((1, width), lambda b, s: (0, 0))],
        out_specs=pl.BlockSpec((tb, width + o_b.shape[1]), lambda b, s: (b * nb + s, 0)),
        out_shape=jax.ShapeDtypeStruct((bsz * seq, width + o_b.shape[1]), BF16),
        scratch_shapes=[pltpu.VMEM((HG_HEADS, HG_DK, HG_DK), F32),
                        pltpu.VMEM((tb, width), F32)]
                       + [pltpu.VMEM((2, tb, HG_SLAB), BF16)] * 4
                       + [pltpu.VMEM((2, tb // CHUNK, HG_SLAB), F32)],
        compiler_params=_cparams(("parallel", "arbitrary"), 56),
        name="hgrn2",
    )(sq, lf, v, sg, o_b, a_norm_w.reshape(1, width))


LRU_TILE = 512


def _lru_block(xc, gate_in, wg, ba, bx, lam, h, row):
    gates = jnp.dot(xc.astype(BF16), wg, preferred_element_type=F32)
    quarter = (0.25 * LRU_C) * _softplus(-lam)
    t_r = jnp.tanh(0.5 * (gates[:, :LRU_BLOCK] + ba))
    tau = jnp.tanh(-quarter - quarter * t_r)
    rho = 1.0 / (1.0 - tau)
    a = (1.0 + tau) * rho
    neg_tau = -tau
    root = jnp.where(neg_tau > 0.0, neg_tau * lax.rsqrt(neg_tau), 0.0)
    t_i = jnp.tanh(0.5 * (gates[:, LRU_BLOCK:] + bx))
    u = (rho * root) * ((1.0 + t_i) * xc)
    outs = []
    for r8 in range(xc.shape[0] // SUBLANES):
        a8 = a[r8 * SUBLANES:(r8 + 1) * SUBLANES]
        u8 = u[r8 * SUBLANES:(r8 + 1) * SUBLANES]
        for sh in (1, 2, 4):
            keep = row >= sh
            a_s = jnp.where(keep, pltpu.roll(a8, sh, 0), 1.0)
            u_s = jnp.where(keep, pltpu.roll(u8, sh, 0), 0.0)
            u8 = a8 * u_s + u8
            a8 = a8 * a_s
        h8 = a8 * h + u8
        h = h8[SUBLANES - 1:SUBLANES]
        outs.append(h8)
    half_g = 0.5 * gate_in
    return (jnp.concatenate(outs, axis=0) * half_g) * (1.0 + jnp.tanh(half_g)), h


def _lru_proj_kernel(a_ref, wxb_ref, wgb_ref, cw_ref, cb_ref, wg_ref, bg_ref, lam_ref, o_ref,
                     p_ref, hist_ref, h_ref, *, tm, tiles_per_seq):
    i = pl.program_id(0)
    j = pl.program_id(1)

    @pl.when(lax.rem(i, tiles_per_seq) == 0)
    def _():
        hist_ref[j] = jnp.zeros(hist_ref.shape[1:], F32)
        h_ref[j] = jnp.zeros(h_ref.shape[1:], F32)

    a = a_ref[...]
    p_ref[:, 0:LRU_TILE] = jnp.dot(a, wxb_ref[...], preferred_element_type=F32)
    p_ref[:, LRU_TILE:2 * LRU_TILE] = jnp.dot(a, wgb_ref[...], preferred_element_type=F32)
    row = lax.broadcasted_iota(jnp.int32, (SUBLANES, LRU_BLOCK), 0)
    hist = hist_ref[j]
    h_in = h_ref[j]
    h_out = []
    for blk in range(LRU_TILE // LRU_BLOCK):
        lanes = slice(blk * LRU_BLOCK, (blk + 1) * LRU_BLOCK)
        n = j * (LRU_TILE // LRU_BLOCK) + blk
        cols = pl.ds(pl.multiple_of(n * LRU_BLOCK, LRU_BLOCK), LRU_BLOCK)
        xc = jnp.concatenate(
            [acc for _, acc in _causal_conv_rows(hist[:, lanes], p_ref, lanes, tm, cw_ref[:, cols], cb_ref[:, cols])],
            axis=0)
        out, h_last = _lru_block(xc, p_ref[:, LRU_TILE + blk * LRU_BLOCK:LRU_TILE + (blk + 1) * LRU_BLOCK],
                                 wg_ref[n], bg_ref[0:1, cols], bg_ref[1:2, cols], lam_ref[:, cols],
                                 h_in[:, lanes], row)
        o_ref[:, lanes] = out.astype(o_ref.dtype)
        h_out.append(h_last)
    hist_ref[j] = p_ref[tm - SUBLANES:tm, 0:LRU_TILE]
    h_ref[j] = jnp.concatenate(h_out, axis=1)


def _lru_proj(hn, w, xb_col0, gb_col0, conv_w, conv_b, wa, ba, wx, bx, lam, seq, tm=1024):
    m, k = hn.shape
    width = LRU_BLOCKS * LRU_BLOCK
    n_tiles = width // LRU_TILE
    assert xb_col0 % LRU_TILE == 0 and gb_col0 % LRU_TILE == 0
    assert seq % tm == 0, "row tiles must not straddle sequences"
    wg = jnp.concatenate([wa, wx], axis=-1).astype(BF16)
    bg = jnp.stack([ba, bx], axis=0)

    def full(shape):
        return pl.BlockSpec(shape, lambda i, j: (0,) * len(shape))

    return pl.pallas_call(
        functools.partial(_lru_proj_kernel, tm=tm, tiles_per_seq=seq // tm),
        grid=(m // tm, n_tiles),
        in_specs=[pl.BlockSpec((tm, k), lambda i, j: (i, 0)),
                  pl.BlockSpec((k, LRU_TILE), lambda i, j: (0, xb_col0 // LRU_TILE + j)),
                  pl.BlockSpec((k, LRU_TILE), lambda i, j: (0, gb_col0 // LRU_TILE + j)),
                  full(conv_w.shape), full((1, width)), full(wg.shape), full(bg.shape), full((1, width))],
        out_specs=pl.BlockSpec((tm, LRU_TILE), lambda i, j: (i, j)),
        out_shape=jax.ShapeDtypeStruct((m, width), BF16),
        scratch_shapes=[pltpu.VMEM((tm, 2 * LRU_TILE), F32),
                        pltpu.VMEM((n_tiles, SUBLANES, LRU_TILE), F32),
                        pltpu.VMEM((n_tiles, 1, LRU_TILE), F32)],
        compiler_params=_cparams(("arbitrary", "arbitrary"), 48),
        name="even_in_proj_rglru",
    )(hn, w, w, conv_w, conv_b.reshape(1, width), wg, bg, lam.reshape(1, width))


def _ssd_kernel(sz_ref, xbc_ref, dt_ref, cw_ref, cb_ref, dtb_ref, alog_ref, dsk_ref, nw_ref, pidx_ref, o_ref,
                hist_ref, xc_ref, st_ref, st2_ref, y_ref, tt_ref, cs_ref, cbd_ref, bt_ref, e_ref, *, tb):
    n_chunks = tb // CHUNK

    @pl.when(pl.program_id(1) == 0)
    def _():
        hist_ref[...] = jnp.zeros_like(hist_ref)
        st_ref[...] = jnp.zeros_like(st_ref)

    tri_bf = _tri(CHUNK).astype(BF16)
    lane = lax.broadcasted_iota(jnp.int32, (CHUNK, LANES), 1)
    rowi = lax.broadcasted_iota(jnp.int32, (CHUNK, LANES), 0)
    lane_hi = (lane >= SSD_HEADDIM).astype(jnp.int32)
    causal2 = rowi >= (lane - SSD_HEADDIM * lane_hi)
    lane1_lo = lax.broadcasted_iota(jnp.int32, (1, LANES), 1) < SSD_HEADDIM
    lane_lo = lane < SSD_HEADDIM
    nt = (((1,), (1,)), ((), ()))
    a_neg = -jnp.exp(alog_ref[...])

    for c in range(n_chunks):
        dt = _softplus(dt_ref[c * CHUNK:(c + 1) * CHUNK, :] + dtb_ref[...])
        cs = _chunk_cumsum(tri_bf, dt * a_neg, terms=3)
        w = dt * jnp.exp(cs[CHUNK - 1:CHUNK] - cs)
        cs_ref[c] = cs
        for t, arr in enumerate((cs - jnp.log(dt), w)):
            tt_ref[c, t] = jnp.concatenate([arr, arr], axis=0).T

    pairs_per_job = PAIRS_PER_GROUP
    jobs_per_chunk = SSD_HEADS // 2 // pairs_per_job

    def expand_pairs(c, first_pair):
        cs = cs_ref[c]
        for i in range(pairs_per_job):
            pair = first_pair + i
            idx = jnp.broadcast_to(pidx_ref[pl.ds(pair, 1), :], (CHUNK, LANES))
            e_ref[c, :, pl.ds(pl.multiple_of(pair * LANES, LANES), LANES)] = jnp.take_along_axis(
                cs, idx, axis=1, mode="promise_in_bounds")

    slab = 512
    n_expand = n_chunks * jobs_per_chunk
    assert SSD_CONV_DIM // slab >= n_expand

    def conv_body(j, carry):
        job = lax.rem(j, n_expand)
        expand_pairs(job // jobs_per_chunk, lax.rem(job, jobs_per_chunk) * pairs_per_job)
        cols = pl.ds(pl.multiple_of(j * slab, slab), slab)
        for r0, acc in _causal_conv_rows(hist_ref[:, cols], xbc_ref, cols, tb, cw_ref[:, cols], cb_ref[:, cols]):
            xc_ref[r0:r0 + SUBLANES, cols] = _silu(acc)
        return carry

    lax.fori_loop(0, SSD_CONV_DIM // slab, conv_body, 0)

    for c in range(n_chunks):
        for g in range(SSD_GROUPS):
            b_g = xc_ref[c * CHUNK:(c + 1) * CHUNK, SSD_INNER + g * SSD_STATE:SSD_INNER + (g + 1) * SSD_STATE]
            c_g = xc_ref[c * CHUNK:(c + 1) * CHUNK,
                         SSD_INNER + SSD_BC + g * SSD_STATE:SSD_INNER + SSD_BC + (g + 1) * SSD_STATE]
            b2 = jnp.concatenate([b_g, b_g], axis=0)
            cbd_ref[c, g] = lax.dot_general(c_g.astype(BF16), b2.astype(BF16), nt,
                                            preferred_element_type=F32)
            bt_ref[c, g] = b2.T

    for c in range(n_chunks):
        r0 = c * CHUNK
        st_in, st_out = (st_ref, st2_ref) if c % 2 == 0 else (st2_ref, st_ref)

        def group_body(g, carry, c=c, r0=r0, st_in=st_in, st_out=st_out):
            c_g = xc_ref[r0:r0 + CHUNK,
                         pl.ds(pl.multiple_of(SSD_INNER + SSD_BC + g * SSD_STATE, SSD_STATE), SSD_STATE)]
            c_bf = c_g.astype(BF16)
            cb2 = cbd_ref[c, g]
            bt2 = bt_ref[c, g]
            for i in range(PAIRS_PER_GROUP):
                h_a = g * (2 * PAIRS_PER_GROUP) + 2 * i
                cols = pl.ds(pl.multiple_of(h_a * SSD_HEADDIM, LANES), LANES)

                def pair_row(t, h_a=h_a):
                    return jnp.where(lane1_lo, tt_ref[c, t, pl.ds(h_a, 1), :], tt_ref[c, t, pl.ds(h_a + 1, 1), :])

                e2 = e_ref[c, :, cols]
                att = cb2 * jnp.where(causal2, jnp.exp(e2 - pair_row(0)), 0.0)
                lhs = jnp.concatenate([att, bt2 * pair_row(1)], axis=0).astype(BF16)
                x2 = xc_ref[r0:r0 + CHUNK, cols]
                rhs = jnp.concatenate([jnp.where(lane_lo, x2, 0.0), jnp.where(lane_lo, 0.0, x2)],
                                      axis=0).astype(BF16)
                res = jnp.dot(lhs, rhs, preferred_element_type=F32)
                s_p = st_in[:, cols]
                e_out = jnp.exp(e2)
                y_off = e_out * jnp.dot(c_bf, s_p.astype(BF16), preferred_element_type=F32)
                y_ref[r0:r0 + CHUNK, cols] = res[0:CHUNK] + y_off + x2 * dsk_ref[:, cols]
                st_out[:, cols] = s_p * e_out[CHUNK - 1:CHUNK] + res[CHUNK:3 * CHUNK]
            return carry

        lax.fori_loop(0, SSD_GROUPS, group_body, 0)

    hist_ref[...] = xbc_ref[tb - SUBLANES:tb, :]

    def norm_body(r, carry):
        rows = pl.ds(pl.multiple_of(r * ROW_SLAB, ROW_SLAB), ROW_SLAB)
        _gated_rmsnorm_rows(y_ref, sz_ref, nw_ref, o_ref, rows, gate_first=True)
        return carry

    lax.fori_loop(0, tb // ROW_SLAB, norm_body, 0)


def _ssd(sz, xbc, dt, conv_w, conv_b, dt_bias, a_log, d_skip, norm_w, bsz, seq, tb=128):
    assert (tb // CHUNK) % 2 == 0, "the state ping-pong needs an even number of chunks per block"
    nb = seq // tb
    dsk = jnp.repeat(d_skip, SSD_HEADDIM).reshape(1, SSD_INNER)
    pair_idx = (2 * jnp.arange(SSD_HEADS // 2, dtype=jnp.int32)[:, None]
                + (jnp.arange(LANES, dtype=jnp.int32) // SSD_HEADDIM)[None, :])

    def rows(width):
        return pl.BlockSpec((tb, width), lambda b, s: (b * nb + s, 0))

    def full(shape):
        return pl.BlockSpec(shape, lambda b, s: (0,) * len(shape))

    return pl.pallas_call(
        functools.partial(_ssd_kernel, tb=tb),
        grid=(bsz, nb),
        in_specs=[rows(SSD_INNER), rows(SSD_CONV_DIM), rows(SSD_HEADS),
                  full(conv_w.shape), full((1, SSD_CONV_DIM)),
                  full((1, SSD_HEADS)), full((1, SSD_HEADS)), full((1, SSD_INNER)), full((1, SSD_INNER)),
                  full((SSD_HEADS // 2, LANES))],
        out_specs=rows(SSD_INNER),
        out_shape=jax.ShapeDtypeStruct((bsz * seq, SSD_INNER), BF16),
        scratch_shapes=[pltpu.VMEM((SUBLANES, SSD_CONV_DIM), F32),
                        pltpu.VMEM((tb, SSD_CONV_DIM), F32),
                        pltpu.VMEM((SSD_STATE, SSD_INNER), F32),
                        pltpu.VMEM((SSD_STATE, SSD_INNER), F32),
                        pltpu.VMEM((tb, SSD_INNER), F32),
                        pltpu.VMEM((tb // CHUNK, 2, SSD_HEADS, LANES), F32),
                        pltpu.VMEM((tb // CHUNK, CHUNK, SSD_HEADS), F32),
                        pltpu.VMEM((tb // CHUNK, SSD_GROUPS, CHUNK, LANES), F32),
                        pltpu.VMEM((tb // CHUNK, SSD_GROUPS, SSD_STATE, LANES), F32),
                        pltpu.VMEM((tb // CHUNK, CHUNK, SSD_INNER), F32)],
        compiler_params=_cparams(("parallel", "arbitrary"), 56),
        name="ssd",
    )(sz, xbc, dt, conv_w, conv_b.reshape(1, SSD_CONV_DIM), dt_bias.reshape(1, SSD_HEADS),
      a_log.reshape(1, SSD_HEADS), dsk, norm_w.reshape(1, SSD_INNER), pair_idx)


def _even_layer(h, norm_w, w_in, lb_logits, a_norm_w, conv_w, conv_b, wa, ba, wx, bx, lam, w_out, bsz, seq):
    hn = _rmsnorm(h, norm_w, BF16)
    hw = HG_HEADS * HG_DK
    lru = LRU_BLOCKS * LRU_BLOCK
    w_in = w_in.astype(BF16)
    proj = functools.partial(_matmul, [hn], w_in, n=hw)
    sq = proj(w_col0=0, epilogue=_silu, name="even_in_proj_q")
    lf = proj(w_col0=hw, epilogue=_hgrn_logf, aux=(lb_logits,), name="even_in_proj_f")
    v = proj(w_col0=2 * hw, out_dtype=BF16, name="even_in_proj_i")
    sg = proj(w_col0=3 * hw, epilogue=_silu, name="even_in_proj_ga")
    o_b = _lru_proj(hn, w_in, 4 * hw, 4 * hw + lru, conv_w, conv_b, wa, ba, wx, bx, lam, seq)
    mix = _hgrn(sq, lf, v, sg, o_b, a_norm_w, bsz, seq)
    return _matmul([mix], w_out.astype(BF16), residual=h, n=D_MODEL, name="even_out_proj")


def _odd_layer(h, norm_w, w_in, conv_w, conv_b, dt_bias, a_log, d_skip, ssd_norm_w, w_out, bsz, seq):
    hn = _rmsnorm(h, norm_w, BF16)
    w_in = w_in.astype(BF16)
    sz = _matmul([hn], w_in, n=SSD_INNER, name="odd_in_proj_z", epilogue=_silu)
    xbc = _matmul([hn], w_in, n=SSD_CONV_DIM, w_col0=SSD_INNER, name="odd_in_proj_xbc")
    dt = _matmul([hn], w_in, n=SSD_HEADS, w_col0=SSD_INNER + SSD_CONV_DIM, tn=SSD_HEADS, name="odd_in_proj_dt")
    y = _ssd(sz, xbc, dt, conv_w, conv_b, dt_bias, a_log, d_skip, ssd_norm_w, bsz, seq)
    return _matmul([y], w_out.astype(BF16), residual=h, n=D_MODEL, name="odd_out_proj")


def kernel(x, norm_w, e_w_in, lb_logits, e_a_norm_w, e_conv_w, e_conv_b, e_wa, e_ba, e_wx, e_bx, e_lambda,
           e_w_out, o_w_in, o_conv_w, o_conv_b, o_dt_bias, o_a_log, o_d, o_norm_w, o_w_out, final_norm_w):
    bsz, seq, d = x.shape
    h = x.reshape(bsz * seq, d)
    h = _even_layer(h, norm_w[0], e_w_in[0], lb_logits, e_a_norm_w[0], e_conv_w[0], e_conv_b[0], e_wa[0],
                    e_ba[0], e_wx[0], e_bx[0], e_lambda[0], e_w_out[0], bsz, seq)
    h = _odd_layer(h, norm_w[1], o_w_in[0], o_conv_w[0], o_conv_b[0], o_dt_bias[0], o_a_log[0], o_d[0],
                   o_norm_w[0], o_w_out[0], bsz, seq)
    return _rmsnorm(h, final_norm_w, x.dtype).reshape(bsz, seq, d)
```

```python
import functools

import jax
import jax.numpy as jnp
from jax import lax
from jax.experimental import pallas as pl
from jax.experimental.pallas import tpu as pltpu

F32 = jnp.float32
BF16 = jnp.bfloat16

EPS = 1e-6
CHUNK = 64
LANES = 128
SUBLANES = 8
D_MODEL = 4096
HG_HEADS = 32
HG_DK = 128
LRU_BLOCKS = 32
LRU_BLOCK = 128
LRU_C = 8.0
CONV_W = 4
SSD_INNER = 8192
SSD_HEADS = 128
SSD_HEADDIM = 64
SSD_GROUPS = 8
SSD_STATE = 128
SSD_BC = SSD_GROUPS * SSD_STATE
SSD_CONV_DIM = SSD_INNER + 2 * SSD_BC
PAIRS_PER_GROUP = SSD_HEADS // SSD_GROUPS // 2
MIB = 1024 * 1024


def _cparams(semantics, vmem_mib):
    return pltpu.CompilerParams(dimension_semantics=semantics, vmem_limit_bytes=vmem_mib * MIB)


def _sigmoid(x):
    return 0.5 + 0.5 * jnp.tanh(0.5 * x)


def _silu(x):
    return x * _sigmoid(x)


def _softplus(x):
    return jnp.maximum(x, 0.0) + jnp.log1p(jnp.exp(-jnp.abs(x)))


def _chunk_cumsum(tri, x, terms):
    acc = None
    rem = x
    for t in range(terms):
        piece = rem.astype(BF16)
        part = jnp.dot(tri, piece, preferred_element_type=F32)
        acc = part if acc is None else acc + part
        if t + 1 < terms:
            rem = rem - piece.astype(F32)
    return acc


def _causal_conv_rows(hist, x_ref, cols, n_rows, w, b):
    row = lax.broadcasted_iota(jnp.int32, hist.shape, 0)
    shifts = range(1, CONV_W)
    taps = [jnp.broadcast_to(w[CONV_W - 1 - s:CONV_W - s], hist.shape) for s in range(CONV_W)]
    bias = jnp.broadcast_to(b, hist.shape)
    prev_rot = [pltpu.roll(hist, s, 0) for s in shifts]
    for r0 in range(0, n_rows, SUBLANES):
        cur = x_ref[r0:r0 + SUBLANES, cols]
        cur_rot = [pltpu.roll(cur, s, 0) for s in shifts]
        out = bias + cur * taps[0]
        for s, c_rot, p_rot in zip(shifts, cur_rot, prev_rot):
            out = out + jnp.where(row >= s, c_rot, p_rot) * taps[s]
        prev_rot = cur_rot
        yield r0, out


NORM_LANES = 512


def _gated_rmsnorm_rows(y_ref, g_ref, w_ref, o_ref, rows, gate_first):
    width = y_ref.shape[1]
    chunks = [slice(c0, c0 + NORM_LANES) for c0 in range(0, width, NORM_LANES)]
    ssq = None
    for cs in chunks:
        v = y_ref[rows, cs] * g_ref[rows, cs] if gate_first else y_ref[rows, cs]
        ssq = v * v if ssq is None else ssq + v * v
    scale = lax.rsqrt(jnp.sum(ssq, axis=-1, keepdims=True) * (1.0 / width) + EPS)
    for cs in chunks:
        v = y_ref[rows, cs] * g_ref[rows, cs]
        o_ref[rows, cs] = (v * scale * w_ref[:, cs]).astype(o_ref.dtype)


def _tri(n):
    r = lax.broadcasted_iota(jnp.int32, (n, n), 0)
    c = lax.broadcasted_iota(jnp.int32, (n, n), 1)
    return r >= c


def _norm_kernel(x_ref, w_ref, o_ref):
    x = x_ref[...]
    ms = jnp.mean(x * x, axis=-1, keepdims=True)
    o_ref[...] = (x * lax.rsqrt(ms + EPS) * w_ref[...]).astype(o_ref.dtype)


def _rmsnorm(x, w, out_dtype, tm=256):
    m, d = x.shape
    return pl.pallas_call(
        _norm_kernel,
        grid=(m // tm,),
        in_specs=[pl.BlockSpec((tm, d), lambda i: (i, 0)),
                  pl.BlockSpec((1, d), lambda i: (0, 0))],
        out_specs=pl.BlockSpec((tm, d), lambda i: (i, 0)),
        out_shape=jax.ShapeDtypeStruct((m, d), out_dtype),
        compiler_params=_cparams(("parallel",), 32),
        name="rmsnorm",
    )(x, w.reshape(1, d))


def _matmul_kernel(*refs, k_ranges, has_res, n_aux, epilogue):
    n_a = len(k_ranges)
    a_refs = refs[:n_a]
    b_ref = refs[n_a]
    r_ref = refs[n_a + 1] if has_res else None
    aux_refs = refs[len(refs) - 1 - n_aux:len(refs) - 1]
    o_ref = refs[-1]
    nk = k_ranges[-1][1]
    if nk == 1:
        p = jnp.dot(a_refs[0][...], b_ref[...], preferred_element_type=F32)
        if has_res:
            p = p + r_ref[...]
        if epilogue is not None:
            p = epilogue(p, *[r[...] for r in aux_refs])
        o_ref[...] = p.astype(o_ref.dtype)
        return
    k = pl.program_id(2)

    def first_step(a_ref):
        p = jnp.dot(a_ref[...], b_ref[...], preferred_element_type=F32)
        o_ref[...] = p + r_ref[...] if has_res else p

    def later_step(a_ref):
        o_ref[...] += jnp.dot(a_ref[...], b_ref[...], preferred_element_type=F32)

    for a_ref, (k0, k1) in zip(a_refs, k_ranges):
        if k0 == 0:
            pl.when(k == 0)(functools.partial(first_step, a_ref))
            k0 = 1
        if k1 > k0:
            pl.when((k >= k0) & (k < k1))(functools.partial(later_step, a_ref))


MM_TILE = 1024


def _matmul_vmem_mib(tm, tn, tk, n_a, has_res, has_epilogue, out_itemsize):
    windows = n_a * tm * tk * 2 + tk * tn * 2 + tm * tn * (out_itemsize + (4 if has_res else 0))
    temporaries = tm * tn * 4 * (3 if has_epilogue else 2)
    return -(-(2 * windows + temporaries) // MIB)


def _matmul(a_list, w, residual=None, *, n, w_col0=0, tm=MM_TILE, tn=MM_TILE, tk=D_MODEL, name, epilogue=None,
            aux=(), out_dtype=F32):
    m = a_list[0].shape[0]
    vmem_mib = _matmul_vmem_mib(tm, tn, tk, len(a_list), residual is not None, epilogue is not None,
                                jnp.dtype(out_dtype).itemsize)
    k_ranges, k0 = [], 0
    for a in a_list:
        k_ranges.append((k0, k0 + a.shape[1] // tk))
        k0 = k_ranges[-1][1]
    nk = k0
    assert nk * tk == w.shape[0] and w_col0 % tn == 0
    assert nk == 1 or (epilogue is None and out_dtype == F32)
    in_specs = [pl.BlockSpec((tm, tk), lambda i, j, k, k0=k0, k1=k1: (i, jnp.clip(k - k0, 0, k1 - k0 - 1)))
                for k0, k1 in k_ranges]
    in_specs.append(pl.BlockSpec((tk, tn), lambda i, j, k: (k, w_col0 // tn + j)))
    args = list(a_list) + [w]
    if residual is not None:
        in_specs.append(pl.BlockSpec((tm, tn), lambda i, j, k: (i, j)))
        args.append(residual)
    for x in aux:
        in_specs.append(pl.BlockSpec((x.shape[0], tn), lambda i, j, k: (0, j)))
        args.append(x)
    return pl.pallas_call(
        functools.partial(_matmul_kernel, k_ranges=tuple(k_ranges), has_res=residual is not None,
                          n_aux=len(aux), epilogue=epilogue),
        grid=(m // tm, n // tn, nk),
        in_specs=in_specs,
        out_specs=pl.BlockSpec((tm, tn), lambda i, j, k: (i, j)),
        out_shape=jax.ShapeDtypeStruct((m, n), out_dtype),
        compiler_params=_cparams(("parallel", "parallel", "arbitrary"), vmem_mib),
        name=name,
    )(*args)


HG_SLAB = 1024
ROW_SLAB = 16


def _hgrn_logf(f_pre, lb_logits):
    e = jnp.exp(lb_logits - jnp.max(lb_logits, axis=0, keepdims=True))
    lb = e[0:1] / jnp.sum(e, axis=0, keepdims=True)
    half_k = 0.5 * (1.0 - lb)
    return jnp.log(1.0 - (half_k - half_k * jnp.tanh(0.5 * f_pre)))


def _hgrn_kernel(sq_ref, lf_ref, v_ref, sg_ref, ob_ref, nw_ref, o_ref,
                 st_ref, acc_ref, qin_ref, kin_ref, qout_ref, kst_ref, gl_ref, *, tb):
    @pl.when(pl.program_id(1) == 0)
    def _():
        st_ref[...] = jnp.zeros_like(st_ref)

    hg_width = HG_HEADS * HG_DK
    o_ref[:, hg_width:] = ob_ref[...]
    n_chunks = tb // CHUNK
    n_slabs = hg_width // HG_SLAB
    heads_per_slab = HG_SLAB // HG_DK
    tri = _tri(CHUNK)
    tri_bf = tri.astype(BF16)
    nt = (((1,), (1,)), ((), ()))
    tn = (((0,), (0,)), ((), ()))

    def phase_a(j, p):
        cols = pl.ds(pl.multiple_of(j * HG_SLAB, HG_SLAB), HG_SLAB)
        for c in range(n_chunks):
            rows = pl.ds(c * CHUNK, CHUNK)
            logf = lf_ref[rows, cols]
            k = 1.0 - jnp.exp(logf)
            b = _chunk_cumsum(tri_bf, logf, terms=2)
            ref = b[CHUNK // 2 - 1:CHUNK // 2]
            blast = b[CHUNK - 1:CHUNK]
            d = b - ref
            q_in = sq_ref[rows, cols] * jnp.exp(d)
            k_in = k * jnp.exp(-d)
            qin_ref[p, rows, :] = q_in.astype(BF16)
            kin_ref[p, rows, :] = k_in.astype(BF16)
            qout_ref[p, rows, :] = (q_in * jnp.exp(ref)).astype(BF16)
            kst_ref[p, rows, :] = (k_in * jnp.exp(blast - ref)).astype(BF16)
            gl_ref[p, c:c + 1, :] = jnp.exp(blast)

    def phase_b(j, p):
        states = [st_ref[j * heads_per_slab + hh] for hh in range(heads_per_slab)]
        for hh in range(heads_per_slab):
            lanes = slice(hh * HG_DK, (hh + 1) * HG_DK)
            out_cols = pl.ds(pl.multiple_of((j * heads_per_slab + hh) * HG_DK, HG_DK), HG_DK)
            s_t = states[hh]
            for c in range(n_chunks):
                rows = pl.ds(c * CHUNK, CHUNK)
                v = v_ref[rows, out_cols]
                scores = lax.dot_general(qin_ref[p, rows, lanes], kin_ref[p, rows, lanes], nt,
                                         preferred_element_type=F32)
                scores = jnp.where(tri, scores, 0.0).astype(BF16)
                lhs = jnp.concatenate([qout_ref[p, rows, lanes], scores], axis=1)
                rhs = jnp.concatenate([s_t.astype(BF16).T, v], axis=0)
                acc_ref[rows, out_cols] = jnp.dot(lhs, rhs, preferred_element_type=F32)
                s_t = (s_t * gl_ref[p, c:c + 1, lanes]
                       + lax.dot_general(v, kst_ref[p, rows, lanes], tn, preferred_element_type=F32))
            states[hh] = s_t
        for hh in range(heads_per_slab):
            st_ref[j * heads_per_slab + hh] = states[hh]

    phase_a(0, 0)

    def skew_body(jj, carry):
        j = 2 * jj
        phase_a(j + 1, 1)
        phase_b(j, 0)
        phase_a(j + 2, 0)
        phase_b(j + 1, 1)
        return carry

    lax.fori_loop(0, n_slabs // 2 - 1, skew_body, 0)
    phase_a(n_slabs - 1, 1)
    phase_b(n_slabs - 2, 0)
    phase_b(n_slabs - 1, 1)

    def norm_body(r, carry):
        rows = pl.ds(pl.multiple_of(r * ROW_SLAB, ROW_SLAB), ROW_SLAB)
        _gated_rmsnorm_rows(acc_ref, sg_ref, nw_ref, o_ref, rows, gate_first=False)
        return carry

    lax.fori_loop(0, tb // ROW_SLAB, norm_body, 0)


def _hgrn(sq, lf, v, sg, o_b, a_norm_w, bsz, seq, tb=256):
    width = HG_HEADS * HG_DK
    nb = seq // tb
    rows = pl.BlockSpec((tb, width), lambda b, s: (b * nb + s, 0))
    rows_b = pl.BlockSpec((tb, o_b.shape[1]), lambda b, s: (b * nb + s, 0))
    return pl.pallas_call(
        functools.partial(_hgrn_kernel, tb=tb),
        grid=(bsz, nb),
        in_specs=[rows, rows, rows, rows, rows_b, pl.BlockSpec((1, width), lambda b, s: (0, 0))],
        out_specs=pl.BlockSpec((tb, width + o_b.shape[1]), lambda b, s: (b * nb + s, 0)),
        out_shape=jax.ShapeDtypeStruct((bsz * seq, width + o_b.shape[1]), BF16),
        scratch_shapes=[pltpu.VMEM((HG_HEADS, HG_DK, HG_DK), F32),
                        pltpu.VMEM((tb, width), F32)]
                       + [pltpu.VMEM((2, tb, HG_SLAB), BF16)] * 4
                       + [pltpu.VMEM((2, tb // CHUNK, HG_SLAB), F32)],
        compiler_params=_cparams(("parallel", "arbitrary"), 56),
        name="hgrn2",
    )(sq, lf, v, sg, o_b, a_norm_w.reshape(1, width))


LRU_TILE = 512


def _lru_block(xc, gate_in, wg, ba, bx, lam, h, row):
    gates = jnp.dot(xc.astype(BF16), wg, preferred_element_type=F32)
    quarter = (0.25 * LRU_C) * _softplus(-lam)
    t_r = jnp.tanh(0.5 * (gates[:, :LRU_BLOCK] + ba))
    tau = jnp.tanh(-quarter - quarter * t_r)
    rho = 1.0 / (1.0 - tau)
    a = (1.0 + tau) * rho
    neg_tau = -tau
    root = jnp.where(neg_tau > 0.0, neg_tau * lax.rsqrt(neg_tau), 0.0)
    t_i = jnp.tanh(0.5 * (gates[:, LRU_BLOCK:] + bx))
    u = (rho * root) * ((1.0 + t_i) * xc)
    outs = []
    for r8 in range(xc.shape[0] // SUBLANES):
        a8 = a[r8 * SUBLANES:(r8 + 1) * SUBLANES]
        u8 = u[r8 * SUBLANES:(r8 + 1) * SUBLANES]
        for sh in (1, 2, 4):
            keep = row >= sh
            a_s = jnp.where(keep, pltpu.roll(a8, sh, 0), 1.0)
            u_s = jnp.where(keep, pltpu.roll(u8, sh, 0), 0.0)
            u8 = a8 * u_s + u8
            a8 = a8 * a_s
        h8 = a8 * h + u8
        h = h8[SUBLANES - 1:SUBLANES]
        outs.append(h8)
    half_g = 0.5 * gate_in
    return (jnp.concatenate(outs, axis=0) * half_g) * (1.0 + jnp.tanh(half_g)), h


def _lru_proj_kernel(a_ref, wxb_ref, wgb_ref, cw_ref, cb_ref, wg_ref, bg_ref, lam_ref, o_ref,
                     p_ref, hist_ref, h_ref, *, tm, tiles_per_seq):
    i = pl.program_id(0)
    j = pl.program_id(1)

    @pl.when(lax.rem(i, tiles_per_seq) == 0)
    def _():
        hist_ref[j] = jnp.zeros(hist_ref.shape[1:], F32)
        h_ref[j] = jnp.zeros(h_ref.shape[1:], F32)

    a = a_ref[...]
    p_ref[:, 0:LRU_TILE] = jnp.dot(a, wxb_ref[...], preferred_element_type=F32)
    p_ref[:, LRU_TILE:2 * LRU_TILE] = jnp.dot(a, wgb_ref[...], preferred_element_type=F32)
    row = lax.broadcasted_iota(jnp.int32, (SUBLANES, LRU_BLOCK), 0)
    hist = hist_ref[j]
    h_in = h_ref[j]
    h_out = []
    for blk in range(LRU_TILE // LRU_BLOCK):
        lanes = slice(blk * LRU_BLOCK, (blk + 1) * LRU_BLOCK)
        n = j * (LRU_TILE // LRU_BLOCK) + blk
        cols = pl.ds(pl.multiple_of(n * LRU_BLOCK, LRU_BLOCK), LRU_BLOCK)
        xc = jnp.concatenate(
            [acc for _, acc in _causal_conv_rows(hist[:, lanes], p_ref, lanes, tm, cw_ref[:, cols], cb_ref[:, cols])],
            axis=0)
        out, h_last = _lru_block(xc, p_ref[:, LRU_TILE + blk * LRU_BLOCK:LRU_TILE + (blk + 1) * LRU_BLOCK],
                                 wg_ref[n], bg_ref[0:1, cols], bg_ref[1:2, cols], lam_ref[:, cols],
                                 h_in[:, lanes], row)
        o_ref[:, lanes] = out.astype(o_ref.dtype)
        h_out.append(h_last)
    hist_ref[j] = p_ref[tm - SUBLANES:tm, 0:LRU_TILE]
    h_ref[j] = jnp.concatenate(h_out, axis=1)


def _lru_proj(hn, w, xb_col0, gb_col0, conv_w, conv_b, wa, ba, wx, bx, lam, seq, tm=1024):
    m, k = hn.shape
    width = LRU_BLOCKS * LRU_BLOCK
    n_tiles = width // LRU_TILE
    assert xb_col0 % LRU_TILE == 0 and gb_col0 % LRU_TILE == 0
    assert seq % tm == 0, "row tiles must not straddle sequences"
    wg = jnp.concatenate([wa, wx], axis=-1).astype(BF16)
    bg = jnp.stack([ba, bx], axis=0)

    def full(shape):
        return pl.BlockSpec(shape, lambda i, j: (0,) * len(shape))

    return pl.pallas_call(
        functools.partial(_lru_proj_kernel, tm=tm, tiles_per_seq=seq // tm),
        grid=(m // tm, n_tiles),
        in_specs=[pl.BlockSpec((tm, k), lambda i, j: (i, 0)),
                  pl.BlockSpec((k, LRU_TILE), lambda i, j: (0, xb_col0 // LRU_TILE + j)),
                  pl.BlockSpec((k, LRU_TILE), lambda i, j: (0, gb_col0 // LRU_TILE + j)),
                  full(conv_w.shape), full((1, width)), full(wg.shape), full(bg.shape), full((1, width))],
        out_specs=pl.BlockSpec((tm, LRU_TILE), lambda i, j: (i, j)),
        out_shape=jax.ShapeDtypeStruct((m, width), BF16),
        scratch_shapes=[pltpu.VMEM((tm, 2 * LRU_TILE), F32),
                        pltpu.VMEM((n_tiles, SUBLANES, LRU_TILE), F32),
                        pltpu.VMEM((n_tiles, 1, LRU_TILE), F32)],
        compiler_params=_cparams(("arbitrary", "arbitrary"), 48),
        name="even_in_proj_rglru",
    )(hn, w, w, conv_w, conv_b.reshape(1, width), wg, bg, lam.reshape(1, width))


def _ssd_kernel(sz_ref, xbc_ref, dt_ref, cw_ref, cb_ref, dtb_ref, alog_ref, dsk_ref, nw_ref, pidx_ref, o_ref,
                hist_ref, xc_ref, st_ref, st2_ref, y_ref, tt_ref, cs_ref, cbd_ref, bt_ref, e_ref, *, tb):
    n_chunks = tb // CHUNK

    @pl.when(pl.program_id(1) == 0)
    def _():
        hist_ref[...] = jnp.zeros_like(hist_ref)
        st_ref[...] = jnp.zeros_like(st_ref)

    tri_bf = _tri(CHUNK).astype(BF16)
    lane = lax.broadcasted_iota(jnp.int32, (CHUNK, LANES), 1)
    rowi = lax.broadcasted_iota(jnp.int32, (CHUNK, LANES), 0)
    lane_hi = (lane >= SSD_HEADDIM).astype(jnp.int32)
    causal2 = rowi >= (lane - SSD_HEADDIM * lane_hi)
    lane1_lo = lax.broadcasted_iota(jnp.int32, (1, LANES), 1) < SSD_HEADDIM
    lane_lo = lane < SSD_HEADDIM
    nt = (((1,), (1,)), ((), ()))
    a_neg = -jnp.exp(alog_ref[...])

    for c in range(n_chunks):
        dt = _softplus(dt_ref[c * CHUNK:(c + 1) * CHUNK, :] + dtb_ref[...])
        cs = _chunk_cumsum(tri_bf, dt * a_neg, terms=3)
        w = dt * jnp.exp(cs[CHUNK - 1:CHUNK] - cs)
        cs_ref[c] = cs
        for t, arr in enumerate((cs - jnp.log(dt), w)):
            tt_ref[c, t] = jnp.concatenate([arr, arr], axis=0).T

    pairs_per_job = PAIRS_PER_GROUP
    jobs_per_chunk = SSD_HEADS // 2 // pairs_per_job

    def expand_pairs(c, first_pair):
        cs = cs_ref[c]
        for i in range(pairs_per_job):
            pair = first_pair + i
            idx = jnp.broadcast_to(pidx_ref[pl.ds(pair, 1), :], (CHUNK, LANES))
            e_ref[c, :, pl.ds(pl.multiple_of(pair * LANES, LANES), LANES)] = jnp.take_along_axis(
                cs, idx, axis=1, mode="promise_in_bounds")

    slab = 512
    n_expand = n_chunks * jobs_per_chunk
    assert SSD_CONV_DIM // slab >= n_expand

    def conv_body(j, carry):
        job = lax.rem(j, n_expand)
        expand_pairs(job // jobs_per_chunk, lax.rem(job, jobs_per_chunk) * pairs_per_job)
        cols = pl.ds(pl.multiple_of(j * slab, slab), slab)
        for r0, acc in _causal_conv_rows(hist_ref[:, cols], xbc_ref, cols, tb, cw_ref[:, cols], cb_ref[:, cols]):
            xc_ref[r0:r0 + SUBLANES, cols] = _silu(acc)
        return carry

    lax.fori_loop(0, SSD_CONV_DIM // slab, conv_body, 0)

    for c in range(n_chunks):
        for g in range(SSD_GROUPS):
            b_g = xc_ref[c * CHUNK:(c + 1) * CHUNK, SSD_INNER + g * SSD_STATE:SSD_INNER + (g + 1) * SSD_STATE]
            c_g = xc_ref[c * CHUNK:(c + 1) * CHUNK,
                         SSD_INNER + SSD_BC + g * SSD_STATE:SSD_INNER + SSD_BC + (g + 1) * SSD_STATE]
            b2 = jnp.concatenate([b_g, b_g], axis=0)
            cbd_ref[c, g] = lax.dot_general(c_g.astype(BF16), b2.astype(BF16), nt,
                                            preferred_element_type=F32)
            bt_ref[c, g] = b2.T

    for c in range(n_chunks):
        r0 = c * CHUNK
        st_in, st_out = (st_ref, st2_ref) if c % 2 == 0 else (st2_ref, st_ref)

        def group_body(g, carry, c=c, r0=r0, st_in=st_in, st_out=st_out):
            c_g = xc_ref[r0:r0 + CHUNK,
                         pl.ds(pl.multiple_of(SSD_INNER + SSD_BC + g * SSD_STATE, SSD_STATE), SSD_STATE)]
            c_bf = c_g.astype(BF16)
            cb2 = cbd_ref[c, g]
            bt2 = bt_ref[c, g]
            for i in range(PAIRS_PER_GROUP):
                h_a = g * (2 * PAIRS_PER_GROUP) + 2 * i
                cols = pl.ds(pl.multiple_of(h_a * SSD_HEADDIM, LANES), LANES)

                def pair_row(t, h_a=h_a):
                    return jnp.where(lane1_lo, tt_ref[c, t, pl.ds(h_a, 1), :], tt_ref[c, t, pl.ds(h_a + 1, 1), :])

                e2 = e_ref[c, :, cols]
                att = cb2 * jnp.where(causal2, jnp.exp(e2 - pair_row(0)), 0.0)
                lhs = jnp.concatenate([att, bt2 * pair_row(1)], axis=0).astype(BF16)
                x2 = xc_ref[r0:r0 + CHUNK, cols]
                rhs = jnp.concatenate([jnp.where(lane_lo, x2, 0.0), jnp.where(lane_lo, 0.0, x2)],
                                      axis=0).astype(BF16)
                res = jnp.dot(lhs, rhs, preferred_element_type=F32)
                s_p = st_in[:, cols]
                e_out = jnp.exp(e2)
                y_off = e_out * jnp.dot(c_bf, s_p.astype(BF16), preferred_element_type=F32)
                y_ref[r0:r0 + CHUNK, cols] = res[0:CHUNK] + y_off + x2 * dsk_ref[:, cols]
                st_out[:, cols] = s_p * e_out[CHUNK - 1:CHUNK] + res[CHUNK:3 * CHUNK]
            return carry

        lax.fori_loop(0, SSD_GROUPS, group_body, 0)

    hist_ref[...] = xbc_ref[tb - SUBLANES:tb, :]

    def norm_body(r, carry):
        rows = pl.ds(pl.multiple_of(r * ROW_SLAB, ROW_SLAB), ROW_SLAB)
        _gated_rmsnorm_rows(y_ref, sz_ref, nw_ref, o_ref, rows, gate_first=True)
        return carry

    lax.fori_loop(0, tb // ROW_SLAB, norm_body, 0)


def _ssd(sz, xbc, dt, conv_w, conv_b, dt_bias, a_log, d_skip, norm_w, bsz, seq, tb=128):
    assert (tb // CHUNK) % 2 == 0, "the state ping-pong needs an even number of chunks per block"
    nb = seq // tb
    dsk = jnp.repeat(d_skip, SSD_HEADDIM).reshape(1, SSD_INNER)
    pair_idx = (2 * jnp.arange(SSD_HEADS // 2, dtype=jnp.int32)[:, None]
                + (jnp.arange(LANES, dtype=jnp.int32) // SSD_HEADDIM)[None, :])

    def rows(width):
        return pl.BlockSpec((tb, width), lambda b, s: (b * nb + s, 0))

    def full(shape):
        return pl.BlockSpec(shape, lambda b, s: (0,) * len(shape))

    return pl.pallas_call(
        functools.partial(_ssd_kernel, tb=tb),
        grid=(bsz, nb),
        in_specs=[rows(SSD_INNER), rows(SSD_CONV_DIM), rows(SSD_HEADS),
                  full(conv_w.shape), full((1, SSD_CONV_DIM)),
                  full((1, SSD_HEADS)), full((1, SSD_HEADS)), full((1, SSD_INNER)), full((1, SSD_INNER)),
                  full((SSD_HEADS // 2, LANES))],
        out_specs=rows(SSD_INNER),
        out_shape=jax.ShapeDtypeStruct((bsz * seq, SSD_INNER), BF16),
        scratch_shapes=[pltpu.VMEM((SUBLANES, SSD_CONV_DIM), F32),
                        pltpu.VMEM((tb, SSD_CONV_DIM), F32),
                        pltpu.VMEM((SSD_STATE, SSD_INNER), F32),
                        pltpu.VMEM((SSD_STATE, SSD_INNER), F32),
                        pltpu.VMEM((tb, SSD_INNER), F32),
                        pltpu.VMEM((tb // CHUNK, 2, SSD_HEADS, LANES), F32),
                        pltpu.VMEM((tb // CHUNK, CHUNK, SSD_HEADS), F32),
                        pltpu.VMEM((tb // CHUNK, SSD_GROUPS, CHUNK, LANES), F32),
                        pltpu.VMEM((tb // CHUNK, SSD_GROUPS, SSD_STATE, LANES), F32),
                        pltpu.VMEM((tb // CHUNK, CHUNK, SSD_INNER), F32)],
        compiler_params=_cparams(("parallel", "arbitrary"), 56),
        name="ssd",
    )(sz, xbc, dt, conv_w, conv_b.reshape(1, SSD_CONV_DIM), dt_bias.reshape(1, SSD_HEADS),
      a_log.reshape(1, SSD_HEADS), dsk, norm_w.reshape(1, SSD_INNER), pair_idx)


def _even_layer(h, norm_w, w_in, lb_logits, a_norm_w, conv_w, conv_b, wa, ba, wx, bx, lam, w_out, bsz, seq):
    hn = _rmsnorm(h, norm_w, BF16)
    hw = HG_HEADS * HG_DK
    lru = LRU_BLOCKS * LRU_BLOCK
    w_in = w_in.astype(BF16)
    proj = functools.partial(_matmul, [hn], w_in, n=hw)
    sq = proj(w_col0=0, epilogue=_silu, name="even_in_proj_q")
    lf = proj(w_col0=hw, epilogue=_hgrn_logf, aux=(lb_logits,), name="even_in_proj_f")
    v = proj(w_col0=2 * hw, out_dtype=BF16, name="even_in_proj_i")
    sg = proj(w_col0=3 * hw, epilogue=_silu, name="even_in_proj_ga")
    o_b = _lru_proj(hn, w_in, 4 * hw, 4 * hw + lru, conv_w, conv_b, wa, ba, wx, bx, lam, seq)
    mix = _hgrn(sq, lf, v, sg, o_b, a_norm_w, bsz, seq)
    return _matmul([mix], w_out.astype(BF16), residual=h, n=D_MODEL, name="even_out_proj")


def _odd_layer(h, norm_w, w_in, conv_w, conv_b, dt_bias, a_log, d_skip, ssd_norm_w, w_out, bsz, seq):
    hn = _rmsnorm(h, norm_w, BF16)
    w_in = w_in.astype(BF16)
    sz = _matmul([hn], w_in, n=SSD_INNER, name="odd_in_proj_z", epilogue=_silu)
    xbc = _matmul([hn], w_in, n=SSD_CONV_DIM, w_col0=SSD_INNER, name="odd_in_proj_xbc")
    dt = _matmul([hn], w_in, n=SSD_HEADS, w_col0=SSD_INNER + SSD_CONV_DIM, tn=SSD_HEADS, name="odd_in_proj_dt")
    y = _ssd(sz, xbc, dt, conv_w, conv_b, dt_bias, a_log, d_skip, ssd_norm_w, bsz, seq)
    return _matmul([y], w_out.astype(BF16), residual=h, n=D_MODEL, name="odd_out_proj")


def kernel(x, norm_w, e_w_in, lb_logits, e_a_norm_w, e_conv_w, e_conv_b, e_wa, e_ba, e_wx, e_bx, e_lambda,
           e_w_out, o_w_in, o_conv_w, o_conv_b, o_dt_bias, o_a_log, o_d, o_norm_w, o_w_out, final_norm_w):
    bsz, seq, d = x.shape
    h = x.reshape(bsz * seq, d)
    h = _even_layer(h, norm_w[0], e_w_in[0], lb_logits, e_a_norm_w[0], e_conv_w[0], e_conv_b[0], e_wa[0],
                    e_ba[0], e_wx[0], e_bx[0], e_lambda[0], e_w_out[0], bsz, seq)
    h = _odd_layer(h, norm_w[1], o_w_in[0], o_conv_w[0], o_conv_b[0], o_dt_bias[0], o_a_log[0], o_d[0],
                   o_norm_w[0], o_w_out[0], bsz, seq)
    return _rmsnorm(h, final_norm_w, x.dtype).reshape(bsz, seq, d)
```
